```python
import math
import jax
import jax.numpy as jnp
from jax import lax
import numpy as np

D_MODEL = 2048
BATCH = 4
SEQ = 4096
DEPTH = 2
DEC_BATCH = 8
DEC_SEQ = 16
PAST_LEN = 1024

CHUNK = 64
Q_BLOCK = 128
HEAD_DIM = 128
N_HEADS = D_MODEL // HEAD_DIM
H_FOX = N_HEADS // 2
H_DIFF = N_HEADS // 4
H_BAND = N_HEADS
N_PREV_CHUNKS = 8
BAND_PAST = N_PREV_CHUNKS * CHUNK
BAND_LEN = BAND_PAST + CHUNK
REL_CLIP = 128
D_FF = 5632
N_EXPERTS = 8
TOP_K = 2
N_EVEN = (DEPTH + 1) // 2
N_ODD = DEPTH // 2
EPS = 1e-6
FOX_W = 3 * H_FOX * HEAD_DIM + H_FOX
DIFF_W = 3 * H_DIFF * 2 * HEAD_DIM
IN_EVEN = FOX_W + DIFF_W
MIX_EVEN = H_FOX * HEAD_DIM + H_DIFF * 2 * HEAD_DIM

kernel_name = "fox_diff_chunkband_moe_stream_step"


def rmsnorm(x, g):
    xf = x.astype(jnp.float32)
    y = xf * lax.rsqrt(jnp.mean(xf * xf, axis=-1, keepdims=True) + EPS)
    return (y * g.astype(jnp.float32)).astype(x.dtype)


def masked_softmax(s, mask):
    return jax.nn.softmax(jnp.where(mask, s, -jnp.inf), axis=-1)


def alibi_slopes(n):
    return 2.0 ** (-8.0 * jnp.arange(1, n + 1, dtype=jnp.float32) / n)


def unblock(o):
    return jnp.moveaxis(o, 0, 1).reshape((o.shape[1], o.shape[0] * o.shape[2]) + o.shape[3:])


def swiglu(u, wg, wu, wd):
    return (jax.nn.silu(u @ wg) * (u @ wu)) @ wd


def fox_attend(q, k, v, cq, ck, q_pos, k_pos):
    s = jnp.einsum("bqhd,bkhd->bhqk", q, k, preferred_element_type=jnp.float32) * (HEAD_DIM ** -0.5)
    s = s + jnp.swapaxes(cq, 1, 2)[..., :, None] - jnp.swapaxes(ck, 1, 2)[..., None, :]
    mask = k_pos[None, :] <= q_pos[:, None]
    p = masked_softmax(s, mask)
    return jnp.einsum("bhqk,bkhd->bqhd", p.astype(v.dtype), v)


def fox_prompt(q, k, v, logf):
    t = q.shape[1]
    c = jnp.cumsum(logf, axis=1)
    pos = jnp.arange(t)

    def block(i):
        st = i * Q_BLOCK
        qb = lax.dynamic_slice_in_dim(q, st, Q_BLOCK, 1)
        cb = lax.dynamic_slice_in_dim(c, st, Q_BLOCK, 1)
        return fox_attend(qb, k, v, cb, c, st + jnp.arange(Q_BLOCK), pos)

    return unblock(lax.map(block, jnp.arange(t // Q_BLOCK)))


def fox_sample(q, k, v, logf, ck, cv, clf):
    p_len, t = ck.shape[1], q.shape[1]
    kk = jnp.concatenate([ck, k], axis=1)
    vv = jnp.concatenate([cv, v], axis=1)
    c = jnp.cumsum(jnp.concatenate([clf.astype(jnp.float32), logf], axis=1), axis=1)
    return fox_attend(q, kk, vv, c[:, p_len:], c, p_len + jnp.arange(t), jnp.arange(p_len + t))


def diff_attend(q, k, v, lam, q_pos, k_pos):
    s = jnp.einsum("bqhmd,bkhmd->bhmqk", q, k, preferred_element_type=jnp.float32) * (HEAD_DIM ** -0.5)
    dist = jnp.abs(q_pos[:, None] - k_pos[None, :]).astype(jnp.float32)
    s = s - alibi_slopes(H_DIFF)[None, :, None, None, None] * dist
    mask = (k_pos // CHUNK)[None, :] <= (q_pos // CHUNK)[:, None]
    p = masked_softmax(s, mask)
    w = p[:, :, 0] - lam * p[:, :, 1]
    return jnp.einsum("bhqk,bkhe->bqhe", w.astype(v.dtype), v)


def diff_prompt(q, k, v, lam):
    t = q.shape[1]
    pos = jnp.arange(t)

    def block(i):
        st = i * Q_BLOCK
        qb = lax.dynamic_slice_in_dim(q, st, Q_BLOCK, 1)
        return diff_attend(qb, k, v, lam, st + jnp.arange(Q_BLOCK), pos)

    return unblock(lax.map(block, jnp.arange(t // Q_BLOCK)))


def diff_sample(q, k, v, lam, ck, cv):
    p_len, t = ck.shape[1], q.shape[1]
    kk = jnp.concatenate([ck, k], axis=1)
    vv = jnp.concatenate([cv, v], axis=1)
    return diff_attend(q, kk, vv, lam, p_len + jnp.arange(t), jnp.arange(p_len + t))


def band_attend(q, k, v, table, q_pos, k_pos):
    s = jnp.einsum("bqhd,bkhd->bhqk", q, k, preferred_element_type=jnp.float32) * (HEAD_DIM ** -0.5)
    rel = jnp.clip(q_pos[:, None] - k_pos[None, :], -REL_CLIP, REL_CLIP) + REL_CLIP
    s = s + table.astype(jnp.float32)[:, rel][None]
    qc = (q_pos // CHUNK)[:, None]
    kc = (k_pos // CHUNK)[None, :]
    mask = (k_pos[None, :] >= 0) & (kc <= qc) & (kc >= qc - N_PREV_CHUNKS)
    p = masked_softmax(s, mask)
    return jnp.einsum("bhqk,bkhd->bqhd", p.astype(v.dtype), v)


def band_prompt(q, k, v, table):
    t = q.shape[1]
    pad = ((0, 0), (BAND_PAST, 0), (0, 0), (0, 0))
    kp, vp = jnp.pad(k, pad), jnp.pad(v, pad)

    def block(c):
        st = c * CHUNK
        qb = lax.dynamic_slice_in_dim(q, st, CHUNK, 1)
        kb = lax.dynamic_slice_in_dim(kp, st, BAND_LEN, 1)
        vb = lax.dynamic_slice_in_dim(vp, st, BAND_LEN, 1)
        return band_attend(qb, kb, vb, table, st + jnp.arange(CHUNK), st - BAND_PAST + jnp.arange(BAND_LEN))

    return unblock(lax.map(block, jnp.arange(t // CHUNK)))


def band_sample(q, k, v, table, ck, cv, past_len):
    pc, t = ck.shape[1], q.shape[1]
    kk = jnp.concatenate([ck, k], axis=1)
    vv = jnp.concatenate([cv, v], axis=1)
    return band_attend(q, kk, vv, table, past_len + jnp.arange(t), past_len - pc + jnp.arange(pc + t))


def split_even(z, b_f):
    b, t, _ = z.shape
    hf = H_FOX * HEAD_DIM
    hd = H_DIFF * 2 * HEAD_DIM
    qa, ka, va, fa, qb, kb, vb = jnp.split(
        z, [hf, 2 * hf, 3 * hf, FOX_W, FOX_W + hd, FOX_W + 2 * hd], axis=-1)
    qa = qa.reshape(b, t, H_FOX, HEAD_DIM)
    ka = ka.reshape(b, t, H_FOX, HEAD_DIM)
    va = va.reshape(b, t, H_FOX, HEAD_DIM)
    logf = jax.nn.log_sigmoid(fa.astype(jnp.float32) + b_f.astype(jnp.float32))
    qb = qb.reshape(b, t, H_DIFF, 2, HEAD_DIM)
    kb = kb.reshape(b, t, H_DIFF, 2, HEAD_DIM)
    vb = vb.reshape(b, t, H_DIFF, 2 * HEAD_DIM)
    return qa, ka, va, logf, qb, kb, vb


def moe(u, w_router, wg, wu, wd):
    logits = jnp.einsum("btd,de->bte", u, w_router, preferred_element_type=jnp.float32)
    top_v, top_i = lax.top_k(logits, TOP_K)
    gates = jax.nn.softmax(top_v, axis=-1)
    combine = jnp.sum(jax.nn.one_hot(top_i, N_EXPERTS, dtype=jnp.float32) * gates[..., None], axis=-2)
    y = jnp.zeros_like(u)
    for e in range(N_EXPERTS):
        y = y + combine[..., e:e + 1].astype(u.dtype) * swiglu(u, wg[e], wu[e], wd[e])
    return y


def trunk(x, caches, w, norm_final):
    (norm_mix_even, w_in_even, b_forget, lam_q1, lam_k1, lam_q2, lam_k2, subln_g, w_out_even,
     norm_ffn_even, w_gate, w_up, w_down, norm_mix_odd, w_in_odd, rel_bias, w_out_odd,
     norm_ffn_odd, w_router, w_gate_e, w_up_e, w_down_e) = w
    b, t, _ = x.shape
    fk, fv, fl, dk, dv, bk, bv = [], [], [], [], [], [], []
    h = x
    for layer in range(DEPTH):
        i = layer // 2
        if layer % 2 == 0:
            u = rmsnorm(h, norm_mix_even[i])
            qa, ka, va, logf, qb, kb, vb = split_even(u @ w_in_even[i], b_forget[i])
            lam_init = 0.8 - 0.6 * math.exp(-0.3 * layer)
            lam = (jnp.exp(jnp.sum(lam_q1[i].astype(jnp.float32) * lam_k1[i].astype(jnp.float32)))
                   - jnp.exp(jnp.sum(lam_q2[i].astype(jnp.float32) * lam_k2[i].astype(jnp.float32)))
                   + lam_init)
            if caches is None:
                oa = fox_prompt(qa, ka, va, logf)
                ob = diff_prompt(qb, kb, vb, lam)
            else:
                oa = fox_sample(qa, ka, va, logf, caches[0][i], caches[1][i], caches[2][i])
                ob = diff_sample(qb, kb, vb, lam, caches[3][i], caches[4][i])
            ob = rmsnorm(ob, subln_g[i]) * (1.0 - lam_init)
            o = jnp.concatenate([oa.reshape(b, t, -1), ob.reshape(b, t, -1)], axis=-1)
            h = h + o @ w_out_even[i]
            fk.append(ka)
            fv.append(va)
            fl.append(logf)
            dk.append(kb)
            dv.append(vb)
            u = rmsnorm(h, norm_ffn_even[i])
            h = h + swiglu(u, w_gate[i], w_up[i], w_down[i])
        else:
            u = rmsnorm(h, norm_mix_odd[i])
            z = (u @ w_in_odd[i]).reshape(b, t, 3, H_BAND, HEAD_DIM)
            q, k, v = z[:, :, 0], z[:, :, 1], z[:, :, 2]
            if caches is None:
                o = band_prompt(q, k, v, rel_bias[i])
                keep = min(BAND_PAST, t)
                bk.append(k[:, t - keep:])
                bv.append(v[:, t - keep:])
            else:
                past_len = caches[0].shape[2]
                o = band_sample(q, k, v, rel_bias[i], caches[5][i], caches[6][i], past_len)
                bk.append(k)
                bv.append(v)
            h = h + o.reshape(b, t, -1) @ w_out_odd[i]
            u = rmsnorm(h, norm_ffn_odd[i])
            h = h + moe(u, w_router[i], w_gate_e[i], w_up_e[i], w_down_e[i])
    new = (jnp.stack(fk), jnp.stack(fv), jnp.stack(fl), jnp.stack(dk), jnp.stack(dv),
           jnp.stack(bk), jnp.stack(bv))
    return rmsnorm(h, norm_final), new


def setup_inputs(seed: int = 0) -> dict:
    key = jax.random.key(seed)
    ks = iter(jax.random.split(key, 40))

    def nrm(shape, scale):
        return scale * jax.random.normal(next(ks), shape, jnp.float32)

    band_past = min(BAND_PAST, PAST_LEN)
    d = D_MODEL
    return {
        "x_prompt": nrm((BATCH, SEQ, d), 1.0),
        "x_sample": nrm((DEC_BATCH, DEC_SEQ, d), 1.0),
        "cache_fox_k": nrm((N_EVEN, DEC_BATCH, PAST_LEN, H_FOX, HEAD_DIM), 1.0),
        "cache_fox_v": nrm((N_EVEN, DEC_BATCH, PAST_LEN, H_FOX, HEAD_DIM), 1.0),
        "cache_fox_logf": jax.nn.log_sigmoid(2.0 + nrm((N_EVEN, DEC_BATCH, PAST_LEN, H_FOX), 1.0)),
        "cache_diff_k": nrm((N_EVEN, DEC_BATCH, PAST_LEN, H_DIFF, 2, HEAD_DIM), 1.0),
        "cache_diff_v": nrm((N_EVEN, DEC_BATCH, PAST_LEN, H_DIFF, 2 * HEAD_DIM), 1.0),
        "cache_band_k": nrm((N_ODD, DEC_BATCH, band_past, H_BAND, HEAD_DIM), 1.0),
        "cache_band_v": nrm((N_ODD, DEC_BATCH, band_past, H_BAND, HEAD_DIM), 1.0),
        "norm_mix_even": 1.0 + nrm((N_EVEN, d), 0.1),
        "w_in_even": nrm((N_EVEN, d, IN_EVEN), d ** -0.5),
        "b_forget": 2.0 + nrm((N_EVEN, H_FOX), 0.5),
        "lam_q1": nrm((N_EVEN, HEAD_DIM), 0.1),
        "lam_k1": nrm((N_EVEN, HEAD_DIM), 0.1),
        "lam_q2": nrm((N_EVEN, HEAD_DIM), 0.1),
        "lam_k2": nrm((N_EVEN, HEAD_DIM), 0.1),
        "subln_g": 1.0 + nrm((N_EVEN, 2 * HEAD_DIM), 0.1),
        "w_out_even": nrm((N_EVEN, MIX_EVEN, d), MIX_EVEN ** -0.5),
        "norm_ffn_even": 1.0 + nrm((N_EVEN, d), 0.1),
        "w_gate": nrm((N_EVEN, d, D_FF), d ** -0.5),
        "w_up": nrm((N_EVEN, d, D_FF), d ** -0.5),
        "w_down": nrm((N_EVEN, D_FF, d), D_FF ** -0.5),
        "norm_mix_odd": 1.0 + nrm((N_ODD, d), 0.1),
        "w_in_odd": nrm((N_ODD, d, 3 * H_BAND * HEAD_DIM), d ** -0.5),
        "rel_bias": nrm((N_ODD, H_BAND, 2 * REL_CLIP + 1), 0.5),
        "w_out_odd": nrm((N_ODD, H_BAND * HEAD_DIM, d), (H_BAND * HEAD_DIM) ** -0.5),
        "norm_ffn_odd": 1.0 + nrm((N_ODD, d), 0.1),
        "w_router": nrm((N_ODD, d, N_EXPERTS), d ** -0.5),
        "w_gate_e": nrm((N_ODD, N_EXPERTS, d, D_FF), d ** -0.5),
        "w_up_e": nrm((N_ODD, N_EXPERTS, d, D_FF), d ** -0.5),
        "w_down_e": nrm((N_ODD, N_EXPERTS, D_FF, d), D_FF ** -0.5),
        "norm_final": 1.0 + nrm((d,), 0.1),
    }


def reference(x_prompt, x_sample, cache_fox_k, cache_fox_v, cache_fox_logf, cache_diff_k, cache_diff_v,
              cache_band_k, cache_band_v, norm_mix_even, w_in_even, b_forget, lam_q1, lam_k1, lam_q2,
              lam_k2, subln_g, w_out_even, norm_ffn_even, w_gate, w_up, w_down, norm_mix_odd, w_in_odd,
              rel_bias, w_out_odd, norm_ffn_odd, w_router, w_gate_e, w_up_e, w_down_e, norm_final):
    w = (norm_mix_even, w_in_even, b_forget, lam_q1, lam_k1, lam_q2, lam_k2, subln_g, w_out_even,
         norm_ffn_even, w_gate, w_up, w_down, norm_mix_odd, w_in_odd, rel_bias, w_out_odd,
         norm_ffn_odd, w_router, w_gate_e, w_up_e, w_down_e)
    y_prompt, (fk_p, fv_p, fl_p, dk_p, dv_p, bk_p, bv_p) = trunk(x_prompt, None, w, norm_final)
    caches = (cache_fox_k, cache_fox_v, cache_fox_logf, cache_diff_k, cache_diff_v, cache_band_k, cache_band_v)
    y_sample, (fk_s, fv_s, fl_s, dk_s, dv_s, bk_s, bv_s) = trunk(x_sample, caches, w, norm_final)
    return (y_prompt, y_sample, fk_p, fk_s, fv_p, fv_s, fl_p, fl_s, dk_p, dk_s, dv_p, dv_s,
            bk_p, bk_s, bv_p, bv_s)
```

```python
import functools
import math

import jax
import jax.numpy as jnp
from jax import lax
from jax.experimental import pallas as pl
from jax.experimental.pallas import tpu as pltpu

F32 = jnp.float32
BF16 = jnp.bfloat16
I32 = jnp.int32

D_MODEL = 2048
HEAD_DIM = 128
CHUNK = 64
H_FOX = 8
H_DIFF = 4
H_BAND = 16
N_PREV_CHUNKS = 8
BAND_PAST = N_PREV_CHUNKS * CHUNK
REL_CLIP = 128
D_FF = 5632
N_EXPERTS = 8
EPS = 1e-6
SCALE = HEAD_DIM ** -0.5
LAM_INIT_L0 = 0.8 - 0.6 * math.exp(-0.3 * 0)
NEG_INF = float("-inf")

LANES = 128
VMEM_LIMIT = 56 * 1024 * 1024
MOE_TILE = 512


def _params(sem):
    return pltpu.CompilerParams(dimension_semantics=sem, vmem_limit_bytes=VMEM_LIMIT)


def _rms(x, g):
    return (x * lax.rsqrt(jnp.mean(x * x, axis=-1, keepdims=True) + EPS)) * g


def _log_sigmoid(x):
    return jnp.minimum(x, 0.0) - jnp.log1p(jnp.exp(-jnp.abs(x)))


def _norm_proj_kernel(*refs, n_groups, nb, has_forget):
    if has_forget:
        x_ref, g_ref, w_ref, wf_ref, bf_ref = refs[:5]
        outs = refs[5:5 + n_groups]
        logf_ref = refs[5 + n_groups]
        xn_ref = refs[6 + n_groups]
    else:
        x_ref, g_ref, w_ref = refs[:3]
        outs = refs[3:3 + n_groups]
        xn_ref = refs[3 + n_groups]
    j = pl.program_id(1)

    @pl.when(j == 0)
    def _():
        xn = _rms(x_ref[...], g_ref[...]).astype(BF16)
        xn_ref[...] = xn
        if has_forget:
            fa = jnp.dot(xn, wf_ref[...], preferred_element_type=F32)
            logf_ref[...] = _log_sigmoid(fa + bf_ref[...])[:, :H_FOX]

    z = jnp.dot(xn_ref[...], w_ref[...], preferred_element_type=F32)
    for g in range(n_groups):
        @pl.when(j // nb == g)
        def _(g=g):
            outs[g][...] = z.astype(outs[g].dtype)


def norm_proj(x, g, w, group_dtypes, group_width, wf=None, bf=None, tm=512, tn=512):
    m, d = x.shape
    n = w.shape[1]
    n_groups = len(group_dtypes)
    assert n == n_groups * group_width and group_width % tn == 0 and m % tm == 0
    nb = group_width // tn
    has_forget = wf is not None
    in_specs = [
        pl.BlockSpec((tm, d), lambda i, j: (i, 0)),
        pl.BlockSpec((1, d), lambda i, j: (0, 0)),
        pl.BlockSpec((d, tn), lambda i, j: (0, j)),
    ]
    args = [x, g.reshape(1, d), w]
    if has_forget:
        in_specs += [pl.BlockSpec((d, LANES), lambda i, j: (0, 0)),
                     pl.BlockSpec((1, LANES), lambda i, j: (0, 0))]
        args += [wf, bf]
    out_specs = [pl.BlockSpec((tm, tn), lambda i, j, g=g: (i, jnp.clip(j - g * nb, 0, nb - 1)))
                 for g in range(n_groups)]
    out_shape = [jax.ShapeDtypeStruct((m, group_width), dt) for dt in group_dtypes]
    if has_forget:
        out_specs.append(pl.BlockSpec((tm, H_FOX), lambda i, j: (i, 0)))
        out_shape.append(jax.ShapeDtypeStruct((m, H_FOX), F32))
    return pl.pallas_call(
        functools.partial(_norm_proj_kernel, n_groups=n_groups, nb=nb, has_forget=has_forget),
        grid=(m // tm, n // tn),
        in_specs=in_specs,
        out_specs=out_specs,
        out_shape=out_shape,
        scratch_shapes=[pltpu.VMEM((tm, d), BF16)],
        compiler_params=_params(("parallel", "arbitrary")),
        name="norm_proj",
    )(*args)


def _proj_res_kernel(*refs, n_in):
    xs = refs[:n_in]
    ws = refs[n_in:2 * n_in]
    h_ref = refs[2 * n_in]
    o_ref = refs[2 * n_in + 1]
    acc = h_ref[...]
    for x_ref, w_ref in zip(xs, ws):
        acc = acc + jnp.dot(x_ref[...], w_ref[...], preferred_element_type=F32)
    o_ref[...] = acc


def proj_residual(xs, ws, h, tm=512, tn=1024):
    m, n = h.shape
    assert m % tm == 0 and n % tn == 0
    n_in = len(xs)
    in_specs = [pl.BlockSpec((tm, x.shape[1]), lambda i, j: (i, 0)) for x in xs]
    in_specs += [pl.BlockSpec((w.shape[0], tn), lambda i, j: (0, j)) for w in ws]
    in_specs.append(pl.BlockSpec((tm, tn), lambda i, j: (i, j)))
    return pl.pallas_call(
        functools.partial(_proj_res_kernel, n_in=n_in),
        grid=(m // tm, n // tn),
        in_specs=in_specs,
        out_specs=pl.BlockSpec((tm, tn), lambda i, j: (i, j)),
        out_shape=jax.ShapeDtypeStruct((m, n), F32),
        compiler_params=_params(("parallel", "arbitrary")),
        name="proj_residual",
    )(*xs, *ws, h)


def _swiglu_step(x, wg_ref, wu_ref, wd_ref):
    a = jnp.dot(x, wg_ref[...], preferred_element_type=F32)
    b = jnp.dot(x, wu_ref[...], preferred_element_type=F32)
    mid = (a * jax.nn.sigmoid(a)) * b
    return jnp.dot(mid.astype(BF16), wd_ref[...], preferred_element_type=F32)


def _ffn_dense_kernel(h_ref, g_ref, wg_ref, wu_ref, wd_ref, o_ref, xn_ref, acc_ref):
    j = pl.program_id(1)

    @pl.when(j == 0)
    def _():
        h = h_ref[...]
        xn_ref[...] = _rms(h, g_ref[...]).astype(BF16)
        acc_ref[...] = h

    acc_ref[...] += _swiglu_step(xn_ref[...], wg_ref, wu_ref, wd_ref)

    @pl.when(j == pl.num_programs(1) - 1)
    def _():
        o_ref[...] = acc_ref[...]


def ffn_dense(h, g, wg, wu, wd, tm=512, tf=512):
    m, d = h.shape
    f = wg.shape[1]
    assert m % tm == 0 and f % tf == 0
    return pl.pallas_call(
        _ffn_dense_kernel,
        grid=(m // tm, f // tf),
        in_specs=[
            pl.BlockSpec((tm, d), lambda i, j: (i, 0)),
            pl.BlockSpec((1, d), lambda i, j: (0, 0)),
            pl.BlockSpec((d, tf), lambda i, j: (0, j)),
            pl.BlockSpec((d, tf), lambda i, j: (0, j)),
            pl.BlockSpec((tf, d), lambda i, j: (j, 0)),
        ],
        out_specs=pl.BlockSpec((tm, d), lambda i, j: (i, 0)),
        out_shape=jax.ShapeDtypeStruct((m, d), F32),
        scratch_shapes=[pltpu.VMEM((tm, d), BF16), pltpu.VMEM((tm, d), F32)],
        compiler_params=_params(("parallel", "arbitrary")),
        name="ffn_dense",
    )(h, g.reshape(1, d), wg, wu, wd)


def _ffn_moe_kernel(te_ref, nu_ref, h_ref, g_ref, wg_ref, wu_ref, wd_ref, o_ref, xn_ref, acc_ref):
    t = pl.program_id(0)
    j = pl.program_id(1)

    @pl.when(t < nu_ref[0])
    def _():
        @pl.when(j == 0)
        def _():
            xn_ref[...] = _rms(h_ref[...], g_ref[...]).astype(BF16)
            acc_ref[...] = jnp.zeros_like(acc_ref)

        acc_ref[...] += _swiglu_step(xn_ref[...], wg_ref, wu_ref, wd_ref)

        @pl.when(j == pl.num_programs(1) - 1)
        def _():
            o_ref[...] = acc_ref[...]


def ffn_moe(tile_expert, n_used, hs, g, wg, wu, wd, tf=512):
    p, d = hs.shape
    f = wg.shape[2]
    tm = MOE_TILE
    nt, nj = p // tm, f // tf

    def row_map(t, j, te, nu):
        return (jnp.minimum(t, nu[0] - 1), 0)

    def jj(t, j, nu):
        return jnp.where(t < nu[0], j, nj - 1)

    grid_spec = pltpu.PrefetchScalarGridSpec(
        num_scalar_prefetch=2,
        grid=(nt, nj),
        in_specs=[
            pl.BlockSpec((tm, d), row_map),
            pl.BlockSpec((1, d), lambda t, j, te, nu: (0, 0)),
            pl.BlockSpec((None, d, tf), lambda t, j, te, nu: (te[t], 0, jj(t, j, nu))),
            pl.BlockSpec((None, d, tf), lambda t, j, te, nu: (te[t], 0, jj(t, j, nu))),
            pl.BlockSpec((None, tf, d), lambda t, j, te, nu: (te[t], jj(t, j, nu), 0)),
        ],
        out_specs=pl.BlockSpec((tm, d), row_map),
        scratch_shapes=[pltpu.VMEM((tm, d), BF16), pltpu.VMEM((tm, d), F32)],
    )
    return pl.pallas_call(
        _ffn_moe_kernel,
        grid_spec=grid_spec,
        out_shape=jax.ShapeDtypeStruct((p, d), F32),
        compiler_params=_params(("arbitrary", "arbitrary")),
        name="ffn_moe",
    )(tile_expert, n_used, hs, g.reshape(1, d), wg, wu, wd)


def _router_kernel(h_ref, g_ref, wr_ref, c0_ref, idx_ref, gate_ref, cnt_ref, base_ref, *, tm):
    i = pl.program_id(0)

    @pl.when(i == 0)
    def _():
        base_ref[...] = c0_ref[...]

    u = _rms(h_ref[...], g_ref[...])
    logits = jnp.dot(u, wr_ref[...], preferred_element_type=F32, precision=lax.Precision.HIGHEST)
    lane = lax.broadcasted_iota(I32, (tm, LANES), 1)
    logits = jnp.where(lane < N_EXPERTS, logits, NEG_INF)
    m1 = jnp.max(logits, axis=1, keepdims=True)
    i1 = jnp.min(jnp.where(logits == m1, lane, LANES), axis=1, keepdims=True)
    rest = jnp.where(lane == i1, NEG_INF, logits)
    m2 = jnp.max(rest, axis=1, keepdims=True)
    i2 = jnp.min(jnp.where(rest == m2, lane, LANES), axis=1, keepdims=True)
    e2 = jnp.exp(m2 - m1)
    g1 = 1.0 / (1.0 + e2)
    g2 = e2 / (1.0 + e2)

    sel1 = lane == i1
    sel2 = lane == i2
    onehot = jnp.where(sel1 | sel2, 1.0, 0.0)
    r = lax.broadcasted_iota(I32, (tm, tm), 0)
    c = lax.broadcasted_iota(I32, (tm, tm), 1)
    strict_lower = jnp.where(c < r, 1.0, 0.0).astype(BF16)
    before = jnp.dot(strict_lower, onehot.astype(BF16), preferred_element_type=F32) + base_ref[...]
    r1 = jnp.sum(jnp.where(sel1, before, 0.0), axis=1, keepdims=True)
    r2 = jnp.sum(jnp.where(sel2, before, 0.0), axis=1, keepdims=True)
    base_ref[...] += jnp.sum(onehot, axis=0, keepdims=True)

    idx = jnp.where(lane == 0, i1, jnp.where(lane == 1, i2,
          jnp.where(lane == 2, r1.astype(I32), r2.astype(I32))))
    idx_ref[...] = idx[:, :8]
    gate_ref[...] = jnp.where(lane == 0, g1, g2)[:, :8]

    @pl.when(i == pl.num_programs(0) - 1)
    def _():
        cnt_ref[...] = base_ref[...]


def router(h, g, wr, count0, tm):
    m, d = h.shape
    assert m % tm == 0
    return pl.pallas_call(
        functools.partial(_router_kernel, tm=tm),
        grid=(m // tm,),
        in_specs=[
            pl.BlockSpec((tm, d), lambda i: (i, 0)),
            pl.BlockSpec((1, d), lambda i: (0, 0)),
            pl.BlockSpec((d, LANES), lambda i: (0, 0)),
            pl.BlockSpec((1, LANES), lambda i: (0, 0)),
        ],
        out_specs=[
            pl.BlockSpec((tm, 8), lambda i: (i, 0)),
            pl.BlockSpec((tm, 8), lambda i: (i, 0)),
            pl.BlockSpec((1, LANES), lambda i: (0, 0)),
        ],
        out_shape=[
            jax.ShapeDtypeStruct((m, 8), I32),
            jax.ShapeDtypeStruct((m, 8), F32),
            jax.ShapeDtypeStruct((1, LANES), F32),
        ],
        scratch_shapes=[pltpu.VMEM((1, LANES), F32)],
        compiler_params=_params(("arbitrary",)),
        name="router",
    )(h, g.reshape(1, d), wr, count0)


def _combine_kernel(h_ref, y1_ref, y2_ref, gate_ref, g_ref, o_ref):
    gates = gate_ref[...]
    g1 = gates[:, 0:1]
    g2 = gates[:, 1:2]
    h = h_ref[...] + (g1 * y1_ref[...] + g2 * y2_ref[...])
    o_ref[...] = _rms(h, g_ref[...])


def combine_norm(h, y1, y2, gates, g, tm):
    m, d = h.shape
    row = pl.BlockSpec((tm, d), lambda i: (i, 0))
    return pl.pallas_call(
        _combine_kernel,
        grid=(m // tm,),
        in_specs=[row, row, row, pl.BlockSpec((tm, 8), lambda i: (i, 0)),
                  pl.BlockSpec((1, d), lambda i: (0, 0))],
        out_specs=row,
        out_shape=jax.ShapeDtypeStruct((m, d), F32),
        compiler_params=_params(("parallel",)),
        name="combine_norm",
    )(h, y1, y2, gates, g.reshape(1, d))


def _cumsum_kernel(x_ref, o_ref, *, tb, n_blk):
    r = lax.broadcasted_iota(I32, (tb, tb), 0)
    c = lax.broadcasted_iota(I32, (tb, tb), 1)
    upper = jnp.where(r <= c, 1.0, 0.0)
    carry = jnp.zeros((H_FOX, 1), F32)
    for b in range(n_blk):
        blk = x_ref[0, :, b * tb:(b + 1) * tb]
        cs = jnp.dot(blk, upper, preferred_element_type=F32, precision=lax.Precision.HIGHEST) + carry
        o_ref[0, :, b * tb:(b + 1) * tb] = cs
        carry = cs[:, tb - 1:tb]


def cumsum_time(x, tb):
    b, h, t = x.shape
    assert t % tb == 0
    spec = pl.BlockSpec((1, h, t), lambda i: (i, 0, 0))
    return pl.pallas_call(
        functools.partial(_cumsum_kernel, tb=tb, n_blk=t // tb),
        grid=(b,),
        in_specs=[spec],
        out_specs=spec,
        out_shape=jax.ShapeDtypeStruct((b, h, t), F32),
        compiler_params=_params(("parallel",)),
        name="cumsum_time",
    )(x)


def _fox_kernel(q_ref, k_ref, v_ref, cq_ref, ck_ref, o_ref, kb_ref, vb_ref, *, tq, tk, q_off, n_q):
    h = pl.program_id(1)
    qi = pl.program_id(2)

    @pl.when(qi == 0)
    def _():
        kb_ref[...] = k_ref[...].astype(BF16)
        vb_ref[...] = v_ref[...].astype(BF16)

    qpos0 = q_off if n_q == 1 else q_off + qi * tq
    q = q_ref[...]
    cq_row = cq_ref[0, h, q_off // tq + qi]
    eye = lax.broadcasted_iota(I32, (tq, tq), 0) == lax.broadcasted_iota(I32, (tq, tq), 1)
    cq = jnp.sum(jnp.where(eye, cq_row, 0.0), axis=1, keepdims=True)
    qpos = qpos0 + lax.broadcasted_iota(I32, (tq, tk), 0)
    kiota = lax.broadcasted_iota(I32, (tq, tk), 1)
    n_kv = (qpos0 + tq + tk - 1) // tk

    def body(kv, carry):
        m, l, acc = carry
        k0 = pl.multiple_of(kv * tk, tk)
        kblk = kb_ref[pl.ds(k0, tk), :]
        s = lax.dot_general(q, kblk, (((1,), (1,)), ((), ())), preferred_element_type=F32) * SCALE
        ck = ck_ref[0, h, kv]
        s = s + cq - ck
        s = jnp.where(k0 + kiota <= qpos, s, NEG_INF)
        m_new = jnp.maximum(m, jnp.max(s, axis=1, keepdims=True))
        alpha = jnp.exp(m - m_new)
        p = jnp.exp(s - m_new)
        l = alpha * l + jnp.sum(p, axis=1, keepdims=True)
        acc = alpha * acc + jnp.dot(p.astype(BF16), vb_ref[pl.ds(k0, tk), :], preferred_element_type=F32)
        return m_new, l, acc

    init = (jnp.full((tq, 1), NEG_INF, F32), jnp.zeros((tq, 1), F32), jnp.zeros((tq, HEAD_DIM), F32))
    _, l, acc = lax.fori_loop(0, n_kv, body, init)
    o_ref[...] = (acc / l).astype(o_ref.dtype)


def fox_attention(q, k, v, c, n_batch, t_q, t_k, q_off, tq, tk):
    n_q = t_q // tq
    tc = c.shape[2]
    assert t_q % tq == 0 and t_k % tk == 0 and q_off % tq == 0 and tc % tq == 0 and tc % tk == 0
    c_q = c.reshape(n_batch, H_FOX, tc // tq, 1, tq)
    c_k = c.reshape(n_batch, H_FOX, tc // tk, 1, tk)
    return pl.pallas_call(
        functools.partial(_fox_kernel, tq=tq, tk=tk, q_off=q_off, n_q=n_q),
        grid=(n_batch, H_FOX, n_q),
        in_specs=[
            pl.BlockSpec((tq, HEAD_DIM), lambda b, h, i: (b * n_q + i, h)),
            pl.BlockSpec((t_k, HEAD_DIM), lambda b, h, i: (b, h)),
            pl.BlockSpec((t_k, HEAD_DIM), lambda b, h, i: (b, h)),
            pl.BlockSpec((1, H_FOX, tc // tq, 1, tq), lambda b, h, i: (b, 0, 0, 0, 0)),
            pl.BlockSpec((1, H_FOX, tc // tk, 1, tk), lambda b, h, i: (b, 0, 0, 0, 0)),
        ],
        out_specs=pl.BlockSpec((tq, HEAD_DIM), lambda b, h, i: (b * n_q + i, h)),
        out_shape=jax.ShapeDtypeStruct((n_batch * t_q, H_FOX * HEAD_DIM), BF16),
        scratch_shapes=[pltpu.VMEM((t_k, HEAD_DIM), BF16), pltpu.VMEM((t_k, HEAD_DIM), BF16)],
        compiler_params=_params(("parallel", "parallel", "arbitrary")),
        name="fox_attention",
    )(q, k, v, c_q, c_k)


def _diff_kernel(slope_ref, q_ref, k_ref, v_ref, lq1_ref, lk1_ref, lq2_ref, lk2_ref, sg_ref, o_ref,
                 kb_ref, vb_ref, *, tq, tk, q_off, n_q, n_valid):
    h = pl.program_id(1)
    qi = pl.program_id(2)
    dv = 2 * HEAD_DIM

    @pl.when(qi == 0)
    def _():
        kb_ref[...] = k_ref[...].astype(BF16)
        vb_ref[...] = v_ref[...].astype(BF16)

    lam = (jnp.exp(jnp.sum(lq1_ref[...] * lk1_ref[...], axis=1, keepdims=True))
           - jnp.exp(jnp.sum(lq2_ref[...] * lk2_ref[...], axis=1, keepdims=True)) + LAM_INIT_L0)
    slope = slope_ref[h]
    qpos0 = q_off if n_q == 1 else q_off + qi * tq
    q = q_ref[...]
    qpos = qpos0 + lax.broadcasted_iota(I32, (tq, tk), 0)
    kiota = lax.broadcasted_iota(I32, (tq, tk), 1)
    vis_end = jnp.minimum(((qpos0 + tq - 1) // CHUNK + 1) * CHUNK, n_valid)
    n_kv = (vis_end + tk - 1) // tk

    def body(kv, carry):
        k0 = pl.multiple_of(kv * tk, tk)
        kpos = k0 + kiota
        bias = slope * jnp.abs(qpos - kpos).astype(F32)
        mask = ((kpos >> 6) <= (qpos >> 6)) & (kpos < n_valid)
        vblk = vb_ref[pl.ds(k0, tk), :]
        new = []
        for mp in range(2):
            m, l, acc = carry[mp]
            kblk = kb_ref[pl.ds(k0, tk), mp * HEAD_DIM:(mp + 1) * HEAD_DIM]
            s = lax.dot_general(q[:, mp * HEAD_DIM:(mp + 1) * HEAD_DIM], kblk, (((1,), (1,)), ((), ())),
                                preferred_element_type=F32) * SCALE
            s = jnp.where(mask, s - bias, NEG_INF)
            m_new = jnp.maximum(m, jnp.max(s, axis=1, keepdims=True))
            alpha = jnp.exp(m - m_new)
            p = jnp.exp(s - m_new)
            l = alpha * l + jnp.sum(p, axis=1, keepdims=True)
            acc = alpha * acc + jnp.dot(p.astype(BF16), vblk, preferred_element_type=F32)
            new.append((m_new, l, acc))
        return tuple(new)

    one = (jnp.full((tq, 1), NEG_INF, F32), jnp.zeros((tq, 1), F32), jnp.zeros((tq, dv), F32))
    (_, l0, a0), (_, l1, a1) = lax.fori_loop(0, n_kv, body, (one, one))
    o = a0 / l0 - lam * (a1 / l1)
    o_ref[...] = (_rms(o, sg_ref[...]) * (1.0 - LAM_INIT_L0)).astype(o_ref.dtype)


def diff_attention(q, k, v, lam_vecs, subln_g, n_batch, t_q, t_k, q_off, n_valid, tq, tk):
    n_q = t_q // tq
    assert t_q % tq == 0 and t_k % tk == 0
    dv = 2 * HEAD_DIM
    slopes = 2.0 ** (-8.0 * jnp.arange(1, H_DIFF + 1, dtype=F32) / H_DIFF)
    vec = pl.BlockSpec((1, HEAD_DIM), lambda b, h, i, s: (0, 0))
    grid_spec = pltpu.PrefetchScalarGridSpec(
        num_scalar_prefetch=1,
        grid=(n_batch, H_DIFF, n_q),
        in_specs=[
            pl.BlockSpec((tq, dv), lambda b, h, i, s: (b * n_q + i, h)),
            pl.BlockSpec((t_k, dv), lambda b, h, i, s: (b, h)),
            pl.BlockSpec((t_k, dv), lambda b, h, i, s: (b, h)),
            vec, vec, vec, vec,
            pl.BlockSpec((1, dv), lambda b, h, i, s: (0, 0)),
        ],
        out_specs=pl.BlockSpec((tq, dv), lambda b, h, i, s: (b * n_q + i, h)),
        scratch_shapes=[pltpu.VMEM((t_k, dv), BF16), pltpu.VMEM((t_k, dv), BF16)],
    )
    return pl.pallas_call(
        functools.partial(_diff_kernel, tq=tq, tk=tk, q_off=q_off, n_q=n_q, n_valid=n_valid),
        grid_spec=grid_spec,
        out_shape=jax.ShapeDtypeStruct((n_batch * t_q, H_DIFF * dv), BF16),
        compiler_params=_params(("parallel", "parallel", "arbitrary")),
        name="diff_attention",
    )(slopes, q, k, v, *[x.reshape(1, HEAD_DIM) for x in lam_vecs], subln_g.reshape(1, dv))


REL_PAD = 384


def _band_bias_kernel(t_ref, o_ref, *, nk):
    blk = pl.program_id(0)
    row = lax.broadcasted_iota(I32, (REL_PAD, nk), 0)
    j = lax.broadcasted_iota(I32, (REL_PAD, nk), 1)
    for r in range(8):
        i = blk * 8 + r
        idx = jnp.clip(i - j + BAND_PAST, -REL_CLIP, REL_CLIP) + REL_CLIP
        onehot = jnp.where(row == idx, 1.0, 0.0)
        o_ref[:, r, :] = jnp.dot(t_ref[...], onehot, preferred_element_type=F32,
                                 precision=lax.Precision.HIGHEST)


def band_bias(table_padded, tq, nk):
    return pl.pallas_call(
        functools.partial(_band_bias_kernel, nk=nk),
        grid=(tq // 8,),
        in_specs=[pl.BlockSpec((H_BAND, REL_PAD), lambda i: (0, 0))],
        out_specs=pl.BlockSpec((H_BAND, 8, nk), lambda i: (0, i, 0)),
        out_shape=jax.ShapeDtypeStruct((H_BAND, tq, nk), F32),
        compiler_params=_params(("parallel",)),
        name="band_bias",
    )(table_padded)


def _band_kernel(q_ref, k_ref, v_ref, b_ref, o_ref, kb_ref, vb_ref, *, tq, nk, t_k, front, q_off, n_valid):
    g = pl.program_id(2)

    @pl.when(g == 0)
    def _():
        if front:
            kb_ref[0:front, :] = jnp.zeros((front, HEAD_DIM), BF16)
            vb_ref[0:front, :] = jnp.zeros((front, HEAD_DIM), BF16)
        kb_ref[front:front + t_k, :] = k_ref[...].astype(BF16)
        vb_ref[front:front + t_k, :] = v_ref[...].astype(BF16)

    w0 = pl.multiple_of(g * tq, tq)
    st = q_off + g * tq
    kwin = kb_ref[pl.ds(w0, nk), :]
    vwin = vb_ref[pl.ds(w0, nk), :]
    s = lax.dot_general(q_ref[...], kwin, (((1,), (1,)), ((), ())), preferred_element_type=F32) * SCALE
    s = s + b_ref[0]
    jj = lax.broadcasted_iota(I32, (tq, nk), 1)
    qpos = st + lax.broadcasted_iota(I32, (tq, nk), 0)
    kpos = st - BAND_PAST + jj
    qc = qpos >> 6
    kc = kpos >> 6
    mask = (kpos >= 0) & (kc <= qc) & (kc >= qc - N_PREV_CHUNKS) & (jj < n_valid)
    s = jnp.where(mask, s, NEG_INF)
    m = jnp.max(s, axis=1, keepdims=True)
    p = jnp.exp(s - m)
    l = jnp.sum(p, axis=1, keepdims=True)
    o = jnp.dot(p.astype(BF16), vwin, preferred_element_type=F32)
    o_ref[...] = (o / l).astype(o_ref.dtype)


def band_attention(q, k, v, bias, n_batch, t_q, t_k, front, q_off, n_valid, tq):
    assert CHUNK == 64
    n_g = t_q // tq
    nk = bias.shape[2]
    assert front + t_k == (n_g - 1) * tq + nk
    return pl.pallas_call(
        functools.partial(_band_kernel, tq=tq, nk=nk, t_k=t_k, front=front, q_off=q_off, n_valid=n_valid),
        grid=(n_batch, H_BAND, n_g),
        in_specs=[
            pl.BlockSpec((tq, HEAD_DIM), lambda b, h, g: (b * n_g + g, h)),
            pl.BlockSpec((t_k, HEAD_DIM), lambda b, h, g: (b, h)),
            pl.BlockSpec((t_k, HEAD_DIM), lambda b, h, g: (b, h)),
            pl.BlockSpec((1, tq, nk), lambda b, h, g: (h, 0, 0)),
        ],
        out_specs=pl.BlockSpec((tq, HEAD_DIM), lambda b, h, g: (b * n_g + g, h)),
        out_shape=jax.ShapeDtypeStruct((n_batch * t_q, H_BAND * HEAD_DIM), BF16),
        scratch_shapes=[pltpu.VMEM((front + t_k, HEAD_DIM), BF16), pltpu.VMEM((front + t_k, HEAD_DIM), BF16)],
        compiler_params=_params(("parallel", "parallel", "arbitrary")),
        name="band_attention",
    )(q, k, v, bias)


def _pad_rows(x, rows):
    return jnp.pad(x, ((0, 0), (0, rows - x.shape[1]), (0, 0)))


def _even_layer(h, n_batch, t, caches, w, sample):
    (g_mix, w_main, w_f, b_f, lam_vecs, subln_g, w_out_a, w_out_b, g_ffn, wg, wu, wd) = w
    hw = H_FOX * HEAD_DIM
    tm = 512 if not sample else h.shape[0]
    qa, ka, va, qb, kb, vb, logf = norm_proj(
        h, g_mix, w_main, [BF16, F32, F32, BF16, F32, F32], hw, wf=w_f, bf=b_f, tm=tm)
    logf_t = jnp.swapaxes(logf.reshape(n_batch, t, H_FOX), 1, 2)
    if not sample:
        c = cumsum_time(logf_t, 512)
        oa = fox_attention(qa, ka, va, c, n_batch, t, t, 0, 256, 512)
        ob = diff_attention(qb, kb, vb, lam_vecs, subln_g, n_batch, t, t, 0, t, 256, 512)
    else:
        cfk, cfv, cfl, cdk, cdv = caches
        past = cfk.shape[1]
        t_k = past + t
        t_pad = -(-t_k // 384) * 384
        lf = jnp.concatenate([jnp.swapaxes(cfl, 1, 2), logf_t], axis=2)
        c = cumsum_time(jnp.pad(lf, ((0, 0), (0, 0), (0, t_pad - t_k))), 384)

        def cat(cache, new):
            full = jnp.concatenate([cache.reshape(n_batch, past, hw), new.reshape(n_batch, t, hw)], axis=1)
            return _pad_rows(full, t_pad).reshape(n_batch * t_pad, hw)

        oa = fox_attention(qa, cat(cfk, ka), cat(cfv, va), c, n_batch, t, t_pad, past, t, 384)
        ob = diff_attention(qb, cat(cdk, kb), cat(cdv, vb), lam_vecs, subln_g, n_batch, t, t_pad, past,
                            t_k, t, 384)
    h = proj_residual([oa, ob], [w_out_a, w_out_b], h, tm=tm)
    h = ffn_dense(h, g_ffn, wg, wu, wd, tm=tm)
    return h, (ka, va, logf, kb, vb)


def _odd_attention(h, n_batch, t, caches, w, sample):
    (g_mix, w_in, table, w_out) = w
    d = D_MODEL
    tm = 512 if not sample else h.shape[0]
    q, k, v = norm_proj(h, g_mix, w_in, [BF16, F32, F32], d, tm=tm)
    if not sample:
        tq = 2 * CHUNK
        bias = band_bias(table, tq, BAND_PAST + tq)
        o = band_attention(q, k, v, bias, n_batch, t, t, BAND_PAST, 0, BAND_PAST + tq, tq)
    else:
        cbk, cbv, past_len = caches
        pc = cbk.shape[1]
        assert pc == BAND_PAST and t <= CHUNK
        nk = -(-(pc + t) // LANES) * LANES
        bias = band_bias(table, t, nk)

        def cat(cache, new):
            full = jnp.concatenate([cache.reshape(n_batch, pc, d), new.reshape(n_batch, t, d)], axis=1)
            return _pad_rows(full, nk).reshape(n_batch * nk, d)

        o = band_attention(q, cat(cbk, k), cat(cbv, v), bias, n_batch, t, nk, 0, past_len, pc + t, t)
    h = proj_residual([o], [w_out], h, tm=tm)
    return h, (k, v)


def kernel(x_prompt, x_sample, cache_fox_k, cache_fox_v, cache_fox_logf, cache_diff_k, cache_diff_v, cache_band_k, cache_band_v, norm_mix_even, w_in_even, b_forget, lam_q1, lam_k1, lam_q2, lam_k2, subln_g, w_out_even, norm_ffn_even, w_gate, w_up, w_down, norm_mix_odd, w_in_odd, rel_bias, w_out_odd, norm_ffn_odd, w_router, w_gate_e, w_up_e, w_down_e, norm_final):
    bp, tp, d = x_prompt.shape
    bs, ts, _ = x_sample.shape
    past = cache_fox_k.shape[2]
    hw = H_FOX * HEAD_DIM
    fox_w = 3 * hw + H_FOX

    w_in0 = w_in_even[0]
    w_main = jnp.concatenate([w_in0[:, :3 * hw], w_in0[:, fox_w:]], axis=1).astype(BF16)
    w_f = jnp.pad(w_in0[:, 3 * hw:fox_w], ((0, 0), (0, LANES - H_FOX))).astype(BF16)
    b_f = jnp.pad(b_forget[0], (0, LANES - H_FOX)).reshape(1, LANES)
    lam_vecs = (lam_q1[0], lam_k1[0], lam_q2[0], lam_k2[0])
    w_out0 = w_out_even[0].astype(BF16)
    even_w = (norm_mix_even[0], w_main, w_f, b_f, lam_vecs, subln_g[0], w_out0[:hw], w_out0[hw:],
              norm_ffn_even[0], w_gate[0].astype(BF16), w_up[0].astype(BF16), w_down[0].astype(BF16))
    table = jnp.pad(rel_bias[0], ((0, 0), (0, REL_PAD - rel_bias.shape[2])))
    odd_w = (norm_mix_odd[0], w_in_odd[0].astype(BF16), table, w_out_odd[0].astype(BF16))
    wr = jnp.pad(w_router[0], ((0, 0), (0, LANES - N_EXPERTS)))
    wge, wue, wde = w_gate_e[0].astype(BF16), w_up_e[0].astype(BF16), w_down_e[0].astype(BF16)

    mp, ms = bp * tp, bs * ts
    h_p = x_prompt.reshape(mp, d)
    h_s = x_sample.reshape(ms, d)

    h_p, (fk_p, fv_p, fl_p, dk_p, dv_p) = _even_layer(h_p, bp, tp, None, even_w, False)
    caches_even = (cache_fox_k[0], cache_fox_v[0], cache_fox_logf[0], cache_diff_k[0], cache_diff_v[0])
    h_s, (fk_s, fv_s, fl_s, dk_s, dv_s) = _even_layer(h_s, bs, ts, caches_even, even_w, True)

    h_p, (bk_p, bv_p) = _odd_attention(h_p, bp, tp, None, odd_w, False)
    h_s, (bk_s, bv_s) = _odd_attention(h_s, bs, ts, (cache_band_k[0], cache_band_v[0], past), odd_w, True)

    zero_counts = jnp.zeros((1, LANES), F32)
    idx_p, gate_p, cnt_p = router(h_p, norm_ffn_odd[0], wr, zero_counts, 512)
    idx_s, gate_s, cnt = router(h_s, norm_ffn_odd[0], wr, cnt_p, ms)
    idx = jnp.concatenate([idx_p, idx_s], axis=0)
    m_all = mp + ms
    counts = cnt[0, :N_EXPERTS].astype(I32)
    n_tiles_e = (counts + MOE_TILE - 1) // MOE_TILE
    tile_end = jnp.cumsum(n_tiles_e)
    row_off = (tile_end - n_tiles_e) * MOE_TILE
    n_used = tile_end[-1:]
    nt = (2 * m_all) // MOE_TILE + N_EXPERTS
    tile_ids = jnp.arange(nt, dtype=I32)
    tile_expert = jnp.minimum(jnp.searchsorted(tile_end, tile_ids, side="right"), N_EXPERTS - 1).astype(I32)
    tile_expert = jnp.where(tile_ids < n_used[0], tile_expert, tile_expert[jnp.maximum(n_used[0] - 1, 0)])
    pos1 = row_off[idx[:, 0]] + idx[:, 2]
    pos2 = row_off[idx[:, 1]] + idx[:, 3]
    tokens = jnp.arange(m_all, dtype=I32)
    row_token = jnp.zeros((nt * MOE_TILE,), I32).at[pos1].set(tokens).at[pos2].set(tokens)
    h_all = jnp.concatenate([h_p, h_s], axis=0)
    hs = jnp.take(h_all, row_token, axis=0)
    y = ffn_moe(tile_expert, n_used.astype(I32), hs, norm_ffn_odd[0], wge, wue, wde)
    y1 = jnp.take(y, pos1, axis=0)
    y2 = jnp.take(y, pos2, axis=0)
    y_p = combine_norm(h_p, y1[:mp], y2[:mp], gate_p, norm_final, 512)
    y_s = combine_norm(h_s, y1[mp:], y2[mp:], gate_s, norm_final, ms)

    keep = min(BAND_PAST, tp)
    bk_p = bk_p.reshape(bp, tp, H_BAND, HEAD_DIM)[:, tp - keep:][None]
    bv_p = bv_p.reshape(bp, tp, H_BAND, HEAD_DIM)[:, tp - keep:][None]
    return (
        y_p.reshape(bp, tp, d), y_s.reshape(bs, ts, d),
        fk_p.reshape(1, bp, tp, H_FOX, HEAD_DIM), fk_s.reshape(1, bs, ts, H_FOX, HEAD_DIM),
        fv_p.reshape(1, bp, tp, H_FOX, HEAD_DIM), fv_s.reshape(1, bs, ts, H_FOX, HEAD_DIM),
        fl_p.reshape(1, bp, tp, H_FOX), fl_s.reshape(1, bs, ts, H_FOX),
        dk_p.reshape(1, bp, tp, H_DIFF, 2, HEAD_DIM), dk_s.reshape(1, bs, ts, H_DIFF, 2, HEAD_DIM),
        dv_p.reshape(1, bp, tp, H_DIFF, 2 * HEAD_DIM), dv_s.reshape(1, bs, ts, H_DIFF, 2 * HEAD_DIM),
        bk_p, bk_s.reshape(1, bs, ts, H_BAND, HEAD_DIM),
        bv_p, bv_s.reshape(1, bs, ts, H_BAND, HEAD_DIM),
    )
```

```python
import functools
import math

import jax
import jax.numpy as jnp
from jax import lax
from jax.experimental import pallas as pl
from jax.experimental.pallas import tpu as pltpu

F32 = jnp.float32
BF16 = jnp.bfloat16
I32 = jnp.int32

D_MODEL = 2048
HEAD_DIM = 128
CHUNK = 64
H_FOX = 8
H_DIFF = 4
H_BAND = 16
N_PREV_CHUNKS = 8
BAND_PAST = N_PREV_CHUNKS * CHUNK
REL_CLIP = 128
D_FF = 5632
N_EXPERTS = 8
EPS = 1e-6
SCALE = HEAD_DIM ** -0.5
LAM_INIT_L0 = 0.8 - 0.6 * math.exp(-0.3 * 0)
NEG_INF = float("-inf")

LANES = 128
VMEM_LIMIT = 56 * 1024 * 1024
MOE_TILE = 1024


def _params(sem):
    return pltpu.CompilerParams(dimension_semantics=sem, vmem_limit_bytes=VMEM_LIMIT)


def _rms(x, g):
    return (x * lax.rsqrt(jnp.mean(x * x, axis=-1, keepdims=True) + EPS)) * g


def _log_sigmoid(x):
    return jnp.minimum(x, 0.0) - jnp.log1p(jnp.exp(-jnp.abs(x)))


def _norm_proj_kernel(*refs, n_groups, nb, has_forget):
    if has_forget:
        x_ref, g_ref, w_ref, wf_ref, bf_ref = refs[:5]
        outs = refs[5:5 + n_groups]
        logf_ref = refs[5 + n_groups]
        xn_ref = refs[6 + n_groups]
    else:
        x_ref, g_ref, w_ref = refs[:3]
        outs = refs[3:3 + n_groups]
        xn_ref = refs[3 + n_groups]
    j = pl.program_id(1)

    @pl.when(j == 0)
    def _():
        xn = _rms(x_ref[...], g_ref[...]).astype(BF16)
        xn_ref[...] = xn
        if has_forget:
            fa = jnp.dot(xn, wf_ref[...], preferred_element_type=F32)
            logf_ref[...] = _log_sigmoid(fa + bf_ref[...])[:, :H_FOX]

    z = jnp.dot(xn_ref[...], w_ref[...], preferred_element_type=F32)
    for g in range(n_groups):
        @pl.when(j // nb == g)
        def _(g=g):
            outs[g][...] = z.astype(outs[g].dtype)


def norm_proj(x, g, w, group_dtypes, group_width, wf=None, bf=None, tm=512, tn=512):
    m, d = x.shape
    n = w.shape[1]
    n_groups = len(group_dtypes)
    assert n == n_groups * group_width and group_width % tn == 0 and m % tm == 0
    nb = group_width // tn
    has_forget = wf is not None
    in_specs = [
        pl.BlockSpec((tm, d), lambda i, j: (i, 0)),
        pl.BlockSpec((1, d), lambda i, j: (0, 0)),
        pl.BlockSpec((d, tn), lambda i, j: (0, j)),
    ]
    args = [x, g.reshape(1, d), w]
    if has_forget:
        in_specs += [pl.BlockSpec((d, LANES), lambda i, j: (0, 0)),
                     pl.BlockSpec((1, LANES), lambda i, j: (0, 0))]
        args += [wf, bf]
    out_specs = [pl.BlockSpec((tm, tn), lambda i, j, g=g: (i, jnp.clip(j - g * nb, 0, nb - 1)))
                 for g in range(n_groups)]
    out_shape = [jax.ShapeDtypeStruct((m, group_width), dt) for dt in group_dtypes]
    if has_forget:
        out_specs.append(pl.BlockSpec((tm, H_FOX), lambda i, j: (i, 0)))
        out_shape.append(jax.ShapeDtypeStruct((m, H_FOX), F32))
    return pl.pallas_call(
        functools.partial(_norm_proj_kernel, n_groups=n_groups, nb=nb, has_forget=has_forget),
        grid=(m // tm, n // tn),
        in_specs=in_specs,
        out_specs=out_specs,
        out_shape=out_shape,
        scratch_shapes=[pltpu.VMEM((tm, d), BF16)],
        compiler_params=_params(("parallel", "arbitrary")),
        name="norm_proj",
    )(*args)


def _proj_res_kernel(*refs, n_in):
    xs = refs[:n_in]
    ws = refs[n_in:2 * n_in]
    h_ref = refs[2 * n_in]
    o_ref = refs[2 * n_in + 1]
    acc = h_ref[...]
    for x_ref, w_ref in zip(xs, ws):
        acc = acc + jnp.dot(x_ref[...], w_ref[...], preferred_element_type=F32)
    o_ref[...] = acc


def proj_residual(xs, ws, h, tm=512, tn=1024):
    m, n = h.shape
    assert m % tm == 0 and n % tn == 0
    n_in = len(xs)
    in_specs = [pl.BlockSpec((tm, x.shape[1]), lambda i, j: (i, 0)) for x in xs]
    in_specs += [pl.BlockSpec((w.shape[0], tn), lambda i, j: (0, j)) for w in ws]
    in_specs.append(pl.BlockSpec((tm, tn), lambda i, j: (i, j)))
    return pl.pallas_call(
        functools.partial(_proj_res_kernel, n_in=n_in),
        grid=(m // tm, n // tn),
        in_specs=in_specs,
        out_specs=pl.BlockSpec((tm, tn), lambda i, j: (i, j)),
        out_shape=jax.ShapeDtypeStruct((m, n), F32),
        compiler_params=_params(("parallel", "arbitrary")),
        name="proj_residual",
    )(*xs, *ws, h)


def _swiglu_step(x, wg_ref, wu_ref, wd_ref):
    a = jnp.dot(x, wg_ref[...], preferred_element_type=F32)
    b = jnp.dot(x, wu_ref[...], preferred_element_type=F32)
    mid = (a * jax.nn.sigmoid(a)) * b
    return jnp.dot(mid.astype(BF16), wd_ref[...], preferred_element_type=F32)


def _ffn_dense_kernel(h_ref, g_ref, wg_ref, wu_ref, wd_ref, o_ref, xn_ref, acc_ref):
    j = pl.program_id(1)

    @pl.when(j == 0)
    def _():
        h = h_ref[...]
        xn_ref[...] = _rms(h, g_ref[...]).astype(BF16)
        acc_ref[...] = h

    acc_ref[...] += _swiglu_step(xn_ref[...], wg_ref, wu_ref, wd_ref)

    @pl.when(j == pl.num_programs(1) - 1)
    def _():
        o_ref[...] = acc_ref[...]


def ffn_dense(h, g, wg, wu, wd, tm=512, tf=512):
    m, d = h.shape
    f = wg.shape[1]
    assert m % tm == 0 and f % tf == 0
    return pl.pallas_call(
        _ffn_dense_kernel,
        grid=(m // tm, f // tf),
        in_specs=[
            pl.BlockSpec((tm, d), lambda i, j: (i, 0)),
            pl.BlockSpec((1, d), lambda i, j: (0, 0)),
            pl.BlockSpec((d, tf), lambda i, j: (0, j)),
            pl.BlockSpec((d, tf), lambda i, j: (0, j)),
            pl.BlockSpec((tf, d), lambda i, j: (j, 0)),
        ],
        out_specs=pl.BlockSpec((tm, d), lambda i, j: (i, 0)),
        out_shape=jax.ShapeDtypeStruct((m, d), F32),
        scratch_shapes=[pltpu.VMEM((tm, d), BF16), pltpu.VMEM((tm, d), F32)],
        compiler_params=_params(("parallel", "arbitrary")),
        name="ffn_dense",
    )(h, g.reshape(1, d), wg, wu, wd)


def _ffn_moe_kernel(te_ref, nu_ref, tr_ref, x_ref, wg_ref, wu_ref, wd_ref, o_ref):
    t = pl.program_id(0)
    j = pl.program_id(1)
    half = MOE_TILE // 2

    @pl.when(t < nu_ref[0])
    def _():
        wg = wg_ref[...].astype(BF16)
        wu = wu_ref[...].astype(BF16)
        wd = wd_ref[...].astype(BF16)
        for s in range(2):
            rows = slice(s * half, (s + 1) * half)

            @pl.when(s * half < tr_ref[t])
            def _(rows=rows):
                x = x_ref[rows, :]
                a = jnp.dot(x, wg, preferred_element_type=F32)
                b = jnp.dot(x, wu, preferred_element_type=F32)
                mid = ((a * jax.nn.sigmoid(a)) * b).astype(BF16)
                y = jnp.dot(mid, wd, preferred_element_type=F32)

                @pl.when(j == 0)
                def _():
                    o_ref[rows, :] = y

                @pl.when(j > 0)
                def _():
                    o_ref[rows, :] += y

            @pl.when((s * half >= tr_ref[t]) & (j == 0))
            def _(rows=rows):
                o_ref[rows, :] = jnp.zeros((half, o_ref.shape[1]), F32)


def ffn_moe(tile_expert, n_used, tile_rows, xs, wg, wu, wd, tf=256):
    p, d = xs.shape
    f = wg.shape[2]
    tm = MOE_TILE
    nt, nj = p // tm, f // tf

    def row_map(t, j, te, nu, tr):
        return (jnp.minimum(t, nu[0] - 1), 0)

    def jj(t, j, nu):
        return jnp.where(t < nu[0], j, nj - 1)

    grid_spec = pltpu.PrefetchScalarGridSpec(
        num_scalar_prefetch=3,
        grid=(nt, nj),
        in_specs=[
            pl.BlockSpec((tm, d), row_map),
            pl.BlockSpec((None, d, tf), lambda t, j, te, nu, tr: (te[t], 0, jj(t, j, nu))),
            pl.BlockSpec((None, d, tf), lambda t, j, te, nu, tr: (te[t], 0, jj(t, j, nu))),
            pl.BlockSpec((None, tf, d), lambda t, j, te, nu, tr: (te[t], jj(t, j, nu), 0)),
        ],
        out_specs=pl.BlockSpec((tm, d), row_map),
    )
    return pl.pallas_call(
        _ffn_moe_kernel,
        grid_spec=grid_spec,
        out_shape=jax.ShapeDtypeStruct((p, d), F32),
        compiler_params=_params(("arbitrary", "arbitrary")),
        name="ffn_moe",
    )(tile_expert, n_used, tile_rows, xs, wg, wu, wd)


def _router_kernel(h_ref, g_ref, wr_ref, c0_ref, idx_ref, gate_ref, cnt_ref, u_ref, base_ref, *, tm):
    i = pl.program_id(0)

    @pl.when(i == 0)
    def _():
        base_ref[...] = c0_ref[...]

    u = _rms(h_ref[...], g_ref[...])
    u_ref[...] = u.astype(BF16)
    logits = jnp.dot(u, wr_ref[...], preferred_element_type=F32, precision=lax.Precision.HIGHEST)
    lane = lax.broadcasted_iota(I32, (tm, LANES), 1)
    logits = jnp.where(lane < N_EXPERTS, logits, NEG_INF)
    m1 = jnp.max(logits, axis=1, keepdims=True)
    i1 = jnp.min(jnp.where(logits == m1, lane, LANES), axis=1, keepdims=True)
    rest = jnp.where(lane == i1, NEG_INF, logits)
    m2 = jnp.max(rest, axis=1, keepdims=True)
    i2 = jnp.min(jnp.where(rest == m2, lane, LANES), axis=1, keepdims=True)
    e2 = jnp.exp(m2 - m1)
    g1 = 1.0 / (1.0 + e2)
    g2 = e2 / (1.0 + e2)

    sel1 = lane == i1
    sel2 = lane == i2
    onehot = jnp.where(sel1 | sel2, 1.0, 0.0)
    r = lax.broadcasted_iota(I32, (tm, tm), 0)
    c = lax.broadcasted_iota(I32, (tm, tm), 1)
    strict_lower = jnp.where(c < r, 1.0, 0.0).astype(BF16)
    before = jnp.dot(strict_lower, onehot.astype(BF16), preferred_element_type=F32) + base_ref[...]
    r1 = jnp.sum(jnp.where(sel1, before, 0.0), axis=1, keepdims=True)
    r2 = jnp.sum(jnp.where(sel2, before, 0.0), axis=1, keepdims=True)
    base_ref[...] += jnp.sum(onehot, axis=0, keepdims=True)

    idx = jnp.where(lane == 0, i1, jnp.where(lane == 1, i2,
          jnp.where(lane == 2, r1.astype(I32), r2.astype(I32))))
    idx_ref[...] = idx[:, :8]
    gate_ref[...] = jnp.where(lane == 0, g1, g2)[:, :8]

    @pl.when(i == pl.num_programs(0) - 1)
    def _():
        cnt_ref[...] = base_ref[...]


def _router_kernel_dst(h_ref, g_ref, wr_ref, c0_ref, dst_ref, *rest, tm):
    del dst_ref
    _router_kernel(h_ref, g_ref, wr_ref, c0_ref, *rest, tm=tm)


def router(h, g, wr, count0, tm, u_rows, u_row0, u_dst=None):
    m, d = h.shape
    assert m % tm == 0 and u_row0 % tm == 0
    blk0 = u_row0 // tm
    in_specs = [
        pl.BlockSpec((tm, d), lambda i: (i, 0)),
        pl.BlockSpec((1, d), lambda i: (0, 0)),
        pl.BlockSpec((d, LANES), lambda i: (0, 0)),
        pl.BlockSpec((1, LANES), lambda i: (0, 0)),
    ]
    args = [h, g.reshape(1, d), wr, count0]
    aliases = {}
    if u_dst is not None:
        in_specs.append(pl.BlockSpec(memory_space=pl.ANY))
        args.append(u_dst)
        aliases = {4: 3}
    return pl.pallas_call(
        functools.partial(_router_kernel if u_dst is None else _router_kernel_dst, tm=tm),
        grid=(m // tm,),
        in_specs=in_specs,
        out_specs=[
            pl.BlockSpec((tm, 8), lambda i: (i, 0)),
            pl.BlockSpec((tm, 8), lambda i: (i, 0)),
            pl.BlockSpec((1, LANES), lambda i: (0, 0)),
            pl.BlockSpec((tm, d), lambda i: (i + blk0, 0)),
        ],
        out_shape=[
            jax.ShapeDtypeStruct((m, 8), I32),
            jax.ShapeDtypeStruct((m, 8), F32),
            jax.ShapeDtypeStruct((1, LANES), F32),
            jax.ShapeDtypeStruct((u_rows, d), BF16),
        ],
        scratch_shapes=[pltpu.VMEM((1, LANES), F32)],
        input_output_aliases=aliases,
        compiler_params=_params(("arbitrary",)),
        name="router",
    )(*args)


def _combine_kernel(h_ref, y1_ref, y2_ref, gate_ref, g_ref, o_ref):
    gates = gate_ref[...]
    g1 = gates[:, 0:1]
    g2 = gates[:, 1:2]
    h = h_ref[...] + (g1 * y1_ref[...] + g2 * y2_ref[...])
    o_ref[...] = _rms(h, g_ref[...])


def combine_norm(h, y1, y2, gates, g, tm, y_row0):
    m, d = h.shape
    assert m % tm == 0 and y_row0 % tm == 0
    blk0 = y_row0 // tm
    row = pl.BlockSpec((tm, d), lambda i: (i, 0))
    yrow = pl.BlockSpec((tm, d), lambda i: (i + blk0, 0))
    return pl.pallas_call(
        _combine_kernel,
        grid=(m // tm,),
        in_specs=[row, yrow, yrow, pl.BlockSpec((tm, 8), lambda i: (i, 0)),
                  pl.BlockSpec((1, d), lambda i: (0, 0))],
        out_specs=row,
        out_shape=jax.ShapeDtypeStruct((m, d), F32),
        compiler_params=_params(("parallel",)),
        name="combine_norm",
    )(h, y1, y2, gates, g.reshape(1, d))


def _cumsum_kernel(x_ref, o_ref, *, tb, n_blk):
    r = lax.broadcasted_iota(I32, (tb, tb), 0)
    c = lax.broadcasted_iota(I32, (tb, tb), 1)
    upper = jnp.where(r <= c, 1.0, 0.0)
    carry = jnp.zeros((H_FOX, 1), F32)
    for b in range(n_blk):
        blk = x_ref[0, :, b * tb:(b + 1) * tb]
        cs = jnp.dot(blk, upper, preferred_element_type=F32, precision=lax.Precision.HIGHEST) + carry
        o_ref[0, :, b * tb:(b + 1) * tb] = cs
        carry = cs[:, tb - 1:tb]


def cumsum_time(x, tb):
    b, h, t = x.shape
    assert t % tb == 0
    spec = pl.BlockSpec((1, h, t), lambda i: (i, 0, 0))
    return pl.pallas_call(
        functools.partial(_cumsum_kernel, tb=tb, n_blk=t // tb),
        grid=(b,),
        in_specs=[spec],
        out_specs=spec,
        out_shape=jax.ShapeDtypeStruct((b, h, t), F32),
        compiler_params=_params(("parallel",)),
        name="cumsum_time",
    )(x)


def _fill_rows(dst_ref, parts, cols=slice(None)):
    r = 0
    for part in parts:
        n = part.shape[0]
        dst_ref[r:r + n, cols] = part[...].astype(BF16)
        r += n
    rest = dst_ref.shape[0] - r
    if rest:
        width = dst_ref[0:1, cols].shape[1]
        dst_ref[r:r + rest, cols] = jnp.zeros((rest, width), BF16)


def _fox_kernel(*refs, tq, tk, q_off, n_q, has_cache):
    if has_cache:
        q_ref, k_ref, v_ref, kc_ref, vc_ref, cq_ref, ck_ref, o_ref, kb_ref, vb_ref = refs
        k_parts, v_parts = (kc_ref, k_ref), (vc_ref, v_ref)
    else:
        q_ref, k_ref, v_ref, cq_ref, ck_ref, o_ref, kb_ref, vb_ref = refs
        k_parts, v_parts = (k_ref,), (v_ref,)
    h = pl.program_id(1)
    qi = pl.program_id(2)

    @pl.when(qi == 0)
    def _():
        _fill_rows(kb_ref, k_parts)
        _fill_rows(vb_ref, v_parts)

    qpos0 = q_off if n_q == 1 else q_off + qi * tq
    q = q_ref[...]
    cq_row = cq_ref[0, h, q_off // tq + qi]
    eye = lax.broadcasted_iota(I32, (tq, tq), 0) == lax.broadcasted_iota(I32, (tq, tq), 1)
    cq = jnp.sum(jnp.where(eye, cq_row, 0.0), axis=1, keepdims=True)
    qpos = qpos0 + lax.broadcasted_iota(I32, (tq, tk), 0)
    kiota = lax.broadcasted_iota(I32, (tq, tk), 1)
    n_kv = (qpos0 + tq + tk - 1) // tk

    def body(kv, carry):
        m, l, acc = carry
        k0 = pl.multiple_of(kv * tk, tk)
        kblk = kb_ref[pl.ds(k0, tk), :]
        s = lax.dot_general(q, kblk, (((1,), (1,)), ((), ())), preferred_element_type=F32) * SCALE
        ck = ck_ref[0, h, kv]
        s = s + cq - ck
        s = jnp.where(k0 + kiota <= qpos, s, NEG_INF)
        m_new = jnp.maximum(m, jnp.max(s, axis=1, keepdims=True))
        alpha = jnp.exp(m - m_new)
        p = jnp.exp(s - m_new)
        l = alpha * l + jnp.sum(p, axis=1, keepdims=True)
        acc = alpha * acc + jnp.dot(p.astype(BF16), vb_ref[pl.ds(k0, tk), :], preferred_element_type=F32)
        return m_new, l, acc

    init = (jnp.full((tq, 1), NEG_INF, F32), jnp.zeros((tq, 1), F32), jnp.zeros((tq, HEAD_DIM), F32))
    _, l, acc = lax.fori_loop(0, n_kv, body, init)
    o_ref[...] = (acc / l).astype(o_ref.dtype)


def fox_attention(q, k, v, c, n_batch, t_q, t_new, q_off, tq, tk, cache=None):
    n_q = t_q // tq
    tc = c.shape[2]
    assert t_q % tq == 0 and q_off % tq == 0 and tc % tq == 0 and tc % tk == 0
    c_q = c.reshape(n_batch, H_FOX, tc // tq, 1, tq)
    c_k = c.reshape(n_batch, H_FOX, tc // tk, 1, tk)
    new = pl.BlockSpec((t_new, HEAD_DIM), lambda b, h, i: (b, h))
    in_specs = [pl.BlockSpec((tq, HEAD_DIM), lambda b, h, i: (b * n_q + i, h)), new, new]
    args = [q, k, v]
    if cache is not None:
        past = cache[0].shape[1]
        assert past + t_new <= tc
        old = pl.BlockSpec((past, HEAD_DIM), lambda b, h, i: (b, h))
        in_specs += [old, old]
        args += [x.reshape(n_batch * past, H_FOX * HEAD_DIM) for x in cache]
    else:
        assert t_new == tc
    in_specs += [pl.BlockSpec((1, H_FOX, tc // tq, 1, tq), lambda b, h, i: (b, 0, 0, 0, 0)),
                 pl.BlockSpec((1, H_FOX, tc // tk, 1, tk), lambda b, h, i: (b, 0, 0, 0, 0))]
    return pl.pallas_call(
        functools.partial(_fox_kernel, tq=tq, tk=tk, q_off=q_off, n_q=n_q, has_cache=cache is not None),
        grid=(n_batch, H_FOX, n_q),
        in_specs=in_specs,
        out_specs=pl.BlockSpec((tq, HEAD_DIM), lambda b, h, i: (b * n_q + i, h)),
        out_shape=jax.ShapeDtypeStruct((n_batch * t_q, H_FOX * HEAD_DIM), BF16),
        scratch_shapes=[pltpu.VMEM((tc, HEAD_DIM), BF16), pltpu.VMEM((tc, HEAD_DIM), BF16)],
        compiler_params=_params(("parallel", "parallel", "arbitrary")),
        name="fox_attention",
    )(*args, c_q, c_k)


def _diff_kernel(slope_ref, q_ref, k_ref, v_ref, *refs, tq, tk, q_off, n_q, n_valid, has_cache):
    if has_cache:
        kc0_ref, kc1_ref, vc_ref = refs[:3]
        refs = refs[3:]
    lq1_ref, lk1_ref, lq2_ref, lk2_ref, sg_ref, o_ref, kb_ref, vb_ref = refs
    h = pl.program_id(1)
    qi = pl.program_id(2)
    dv = 2 * HEAD_DIM

    @pl.when(qi == 0)
    def _():
        if has_cache:
            for mp, kc_ref in enumerate((kc0_ref, kc1_ref)):
                cols = slice(mp * HEAD_DIM, (mp + 1) * HEAD_DIM)
                _fill_rows(kb_ref, (kc_ref, k_ref.at[:, cols]), cols)
            _fill_rows(vb_ref, (vc_ref, v_ref))
        else:
            _fill_rows(kb_ref, (k_ref,))
            _fill_rows(vb_ref, (v_ref,))

    lam = (jnp.exp(jnp.sum(lq1_ref[...] * lk1_ref[...], axis=1, keepdims=True))
           - jnp.exp(jnp.sum(lq2_ref[...] * lk2_ref[...], axis=1, keepdims=True)) + LAM_INIT_L0)
    slope = slope_ref[h]
    qpos0 = q_off if n_q == 1 else q_off + qi * tq
    q = q_ref[...]
    qpos = qpos0 + lax.broadcasted_iota(I32, (tq, tk), 0)
    kiota = lax.broadcasted_iota(I32, (tq, tk), 1)
    vis_end = jnp.minimum(((qpos0 + tq - 1) // CHUNK + 1) * CHUNK, n_valid)
    n_kv = (vis_end + tk - 1) // tk

    def body(kv, carry):
        k0 = pl.multiple_of(kv * tk, tk)
        kpos = k0 + kiota
        bias = slope * jnp.abs(qpos - kpos).astype(F32)
        mask = ((kpos >> 6) <= (qpos >> 6)) & (kpos < n_valid)
        vblk = vb_ref[pl.ds(k0, tk), :]
        new = []
        for mp in range(2):
            m, l, acc = carry[mp]
            kblk = kb_ref[pl.ds(k0, tk), mp * HEAD_DIM:(mp + 1) * HEAD_DIM]
            s = lax.dot_general(q[:, mp * HEAD_DIM:(mp + 1) * HEAD_DIM], kblk, (((1,), (1,)), ((), ())),
                                preferred_element_type=F32) * SCALE
            s = jnp.where(mask, s - bias, NEG_INF)
            m_new = jnp.maximum(m, jnp.max(s, axis=1, keepdims=True))
            alpha = jnp.exp(m - m_new)
            p = jnp.exp(s - m_new)
            l = alpha * l + jnp.sum(p, axis=1, keepdims=True)
            acc = alpha * acc + jnp.dot(p.astype(BF16), vblk, preferred_element_type=F32)
            new.append((m_new, l, acc))
        return tuple(new)

    one = (jnp.full((tq, 1), NEG_INF, F32), jnp.zeros((tq, 1), F32), jnp.zeros((tq, dv), F32))
    (_, l0, a0), (_, l1, a1) = lax.fori_loop(0, n_kv, body, (one, one))
    o = a0 / l0 - lam * (a1 / l1)
    o_ref[...] = (_rms(o, sg_ref[...]) * (1.0 - LAM_INIT_L0)).astype(o_ref.dtype)


def diff_attention(q, k, v, lam_vecs, subln_g, n_batch, t_q, t_new, t_s, q_off, tq, tk, cache=None):
    n_q = t_q // tq
    assert t_q % tq == 0 and t_s % tk == 0
    dv = 2 * HEAD_DIM
    slopes = 2.0 ** (-8.0 * jnp.arange(1, H_DIFF + 1, dtype=F32) / H_DIFF)
    vec = pl.BlockSpec((1, HEAD_DIM), lambda b, h, i, s: (0, 0))
    new = pl.BlockSpec((t_new, dv), lambda b, h, i, s: (b, h))
    in_specs = [pl.BlockSpec((tq, dv), lambda b, h, i, s: (b * n_q + i, h)), new, new]
    args = [slopes, q, k, v]
    past = 0
    if cache is not None:
        past = cache[0].shape[1]
        in_specs += [
            pl.BlockSpec((past, HEAD_DIM), lambda b, h, i, s: (b, 2 * h)),
            pl.BlockSpec((past, HEAD_DIM), lambda b, h, i, s: (b, 2 * h + 1)),
            pl.BlockSpec((past, dv), lambda b, h, i, s: (b, h)),
        ]
        ck2d = cache[0].reshape(n_batch * past, H_DIFF * dv)
        args += [ck2d, ck2d, cache[1].reshape(n_batch * past, H_DIFF * dv)]
    n_valid = past + t_new
    assert n_valid <= t_s
    in_specs += [vec, vec, vec, vec, pl.BlockSpec((1, dv), lambda b, h, i, s: (0, 0))]
    grid_spec = pltpu.PrefetchScalarGridSpec(
        num_scalar_prefetch=1,
        grid=(n_batch, H_DIFF, n_q),
        in_specs=in_specs,
        out_specs=pl.BlockSpec((tq, dv), lambda b, h, i, s: (b * n_q + i, h)),
        scratch_shapes=[pltpu.VMEM((t_s, dv), BF16), pltpu.VMEM((t_s, dv), BF16)],
    )
    return pl.pallas_call(
        functools.partial(_diff_kernel, tq=tq, tk=tk, q_off=q_off, n_q=n_q, n_valid=n_valid,
                          has_cache=cache is not None),
        grid_spec=grid_spec,
        out_shape=jax.ShapeDtypeStruct((n_batch * t_q, H_DIFF * dv), BF16),
        compiler_params=_params(("parallel", "parallel", "arbitrary")),
        name="diff_attention",
    )(*args, *[x.reshape(1, HEAD_DIM) for x in lam_vecs], subln_g.reshape(1, dv))


REL_PAD = 384


def _band_bias_kernel(t_ref, o_ref, *, nk, n_valid):
    blk = pl.program_id(0)
    row = lax.broadcasted_iota(I32, (REL_PAD, nk), 0)
    j = lax.broadcasted_iota(I32, (REL_PAD, nk), 1)
    jc = lax.broadcasted_iota(I32, (1, nk), 1) >> 6
    jv = lax.broadcasted_iota(I32, (1, nk), 1) < n_valid
    for r in range(8):
        i = blk * 8 + r
        idx = jnp.clip(i - j + BAND_PAST, -REL_CLIP, REL_CLIP) + REL_CLIP
        onehot = jnp.where(row == idx, 1.0, 0.0)
        bias = jnp.dot(t_ref[...], onehot, preferred_element_type=F32, precision=lax.Precision.HIGHEST)
        ic = i >> 6
        visible = (jc - N_PREV_CHUNKS <= ic) & (jc >= ic) & jv
        o_ref[:, r, :] = jnp.where(visible, bias, NEG_INF)


def band_bias(table_padded, tq, nk, n_valid):
    assert CHUNK == 64
    return pl.pallas_call(
        functools.partial(_band_bias_kernel, nk=nk, n_valid=n_valid),
        grid=(tq // 8,),
        in_specs=[pl.BlockSpec((H_BAND, REL_PAD), lambda i: (0, 0))],
        out_specs=pl.BlockSpec((H_BAND, 8, nk), lambda i: (0, i, 0)),
        out_shape=jax.ShapeDtypeStruct((H_BAND, tq, nk), F32),
        compiler_params=_params(("parallel",)),
        name="band_bias",
    )(table_padded)


def _band_kernel(*refs, tq, nk, n_g, q_off, has_cache):
    if has_cache:
        q_ref, k_ref, v_ref, kc_ref, vc_ref, b_ref, o_ref, kb_ref, vb_ref = refs
        _fill_rows(kb_ref, (kc_ref, k_ref))
        _fill_rows(vb_ref, (vc_ref, v_ref))
    else:
        q_ref, k_ref, v_ref, b_ref, o_ref, kb_ref, vb_ref = refs
        front = kb_ref.shape[0] - k_ref.shape[0]
        kb_ref[0:front, :] = jnp.zeros((front, HEAD_DIM), BF16)
        vb_ref[0:front, :] = jnp.zeros((front, HEAD_DIM), BF16)
        kb_ref[front:, :] = k_ref[...].astype(BF16)
        vb_ref[front:, :] = v_ref[...].astype(BF16)
    jrow = lax.broadcasted_iota(I32, (1, nk), 1)

    def group(g, carry):
        r0 = pl.multiple_of(g * tq, tq)
        st = q_off + g * tq
        kwin = kb_ref[pl.ds(r0, nk), :]
        vwin = vb_ref[pl.ds(r0, nk), :]
        s = lax.dot_general(q_ref[pl.ds(r0, tq), :], kwin, (((1,), (1,)), ((), ())),
                            preferred_element_type=F32) * SCALE
        s = s + b_ref[0] + jnp.where(jrow >= BAND_PAST - st, 0.0, NEG_INF)
        m = jnp.max(s, axis=1, keepdims=True)
        p = jnp.exp(s - m)
        l = jnp.sum(p, axis=1, keepdims=True)
        o = jnp.dot(p.astype(BF16), vwin, preferred_element_type=F32)
        o_ref[pl.ds(r0, tq), :] = (o / l).astype(o_ref.dtype)
        return carry

    if n_g == 1:
        group(0, 0)
    else:
        lax.fori_loop(0, n_g, group, 0)


def band_attention(q, k, v, bias, n_batch, t_q, t_new, q_off, tq, cache=None):
    n_g = t_q // tq
    nk = bias.shape[2]
    t_s = (n_g - 1) * tq + nk
    rows = pl.BlockSpec((t_q, HEAD_DIM), lambda b, h: (b, h))
    new = pl.BlockSpec((t_new, HEAD_DIM), lambda b, h: (b, h))
    in_specs = [rows, new, new]
    args = [q, k, v]
    if cache is not None:
        assert cache[0].shape[1] == BAND_PAST and BAND_PAST + t_new <= t_s and n_g == 1
        old = pl.BlockSpec((BAND_PAST, HEAD_DIM), lambda b, h: (b, h))
        in_specs += [old, old]
        args += [x.reshape(n_batch * BAND_PAST, H_BAND * HEAD_DIM) for x in cache]
    else:
        assert BAND_PAST + t_new == t_s
    in_specs.append(pl.BlockSpec((1, tq, nk), lambda b, h: (h, 0, 0)))
    return pl.pallas_call(
        functools.partial(_band_kernel, tq=tq, nk=nk, n_g=n_g, q_off=q_off, has_cache=cache is not None),
        grid=(n_batch, H_BAND),
        in_specs=in_specs,
        out_specs=rows,
        out_shape=jax.ShapeDtypeStruct((n_batch * t_q, H_BAND * HEAD_DIM), BF16),
        scratch_shapes=[pltpu.VMEM((t_s, HEAD_DIM), BF16), pltpu.VMEM((t_s, HEAD_DIM), BF16)],
        compiler_params=_params(("parallel", "parallel")),
        name="band_attention",
    )(*args, bias)


def _even_layer(h, n_batch, t, caches, w, sample):
    (g_mix, w_main, w_f, b_f, lam_vecs, subln_g, w_out_a, w_out_b, g_ffn, wg, wu, wd) = w
    hw = H_FOX * HEAD_DIM
    tm = 512 if not sample else h.shape[0]
    qa, ka, va, qb, kb, vb, logf = norm_proj(
        h, g_mix, w_main, [BF16, F32, F32, BF16, F32, F32], hw, wf=w_f, bf=b_f, tm=tm)
    logf_t = jnp.swapaxes(logf.reshape(n_batch, t, H_FOX), 1, 2)
    if not sample:
        c = cumsum_time(logf_t, 512)
        oa = fox_attention(qa, ka, va, c, n_batch, t, t, 0, 256, 512)
        ob = diff_attention(qb, kb, vb, lam_vecs, subln_g, n_batch, t, t, t, 0, 256, 512)
    else:
        cfk, cfv, cfl, cdk, cdv = caches
        past = cfk.shape[1]
        t_k = past + t
        t_pad = -(-t_k // 384) * 384
        lf = jnp.concatenate([jnp.swapaxes(cfl, 1, 2), logf_t], axis=2)
        c = cumsum_time(jnp.pad(lf, ((0, 0), (0, 0), (0, t_pad - t_k))), 384)
        oa = fox_attention(qa, ka, va, c, n_batch, t, t, past, t, 384, cache=(cfk, cfv))
        ob = diff_attention(qb, kb, vb, lam_vecs, subln_g, n_batch, t, t, t_pad, past, t, 384, cache=(cdk, cdv))
    h = proj_residual([oa, ob], [w_out_a, w_out_b], h, tm=tm)
    h = ffn_dense(h, g_ffn, wg, wu, wd, tm=tm)
    return h, (ka, va, logf, kb, vb)


def _odd_attention(h, n_batch, t, caches, w, sample):
    (g_mix, w_in, table, w_out) = w
    d = D_MODEL
    tm = 512 if not sample else h.shape[0]
    q, k, v = norm_proj(h, g_mix, w_in, [BF16, F32, F32], d, tm=tm)
    if not sample:
        tq = 2 * CHUNK
        bias = band_bias(table, tq, BAND_PAST + tq, BAND_PAST + tq)
        o = band_attention(q, k, v, bias, n_batch, t, t, 0, tq)
    else:
        cbk, cbv, past_len = caches
        pc = cbk.shape[1]
        assert pc == BAND_PAST and t <= CHUNK and past_len % CHUNK == 0
        nk = -(-(pc + t) // LANES) * LANES
        bias = band_bias(table, t, nk, pc + t)
        o = band_attention(q, k, v, bias, n_batch, t, t, past_len, t, cache=(cbk, cbv))
    h = proj_residual([o], [w_out], h, tm=tm)
    return h, (k, v)


def kernel(x_prompt, x_sample, cache_fox_k, cache_fox_v, cache_fox_logf, cache_diff_k, cache_diff_v, cache_band_k, cache_band_v, norm_mix_even, w_in_even, b_forget, lam_q1, lam_k1, lam_q2, lam_k2, subln_g, w_out_even, norm_ffn_even, w_gate, w_up, w_down, norm_mix_odd, w_in_odd, rel_bias, w_out_odd, norm_ffn_odd, w_router, w_gate_e, w_up_e, w_down_e, norm_final):
    bp, tp, d = x_prompt.shape
    bs, ts, _ = x_sample.shape
    past = cache_fox_k.shape[2]
    hw = H_FOX * HEAD_DIM
    fox_w = 3 * hw + H_FOX

    w_in0 = w_in_even[0]
    w_main = jnp.concatenate([w_in0[:, :3 * hw], w_in0[:, fox_w:]], axis=1).astype(BF16)
    w_f = jnp.pad(w_in0[:, 3 * hw:fox_w], ((0, 0), (0, LANES - H_FOX))).astype(BF16)
    b_f = jnp.pad(b_forget[0], (0, LANES - H_FOX)).reshape(1, LANES)
    lam_vecs = (lam_q1[0], lam_k1[0], lam_q2[0], lam_k2[0])
    w_out0 = w_out_even[0].astype(BF16)
    even_w = (norm_mix_even[0], w_main, w_f, b_f, lam_vecs, subln_g[0], w_out0[:hw], w_out0[hw:],
              norm_ffn_even[0], w_gate[0].astype(BF16), w_up[0].astype(BF16), w_down[0].astype(BF16))
    table = jnp.pad(rel_bias[0], ((0, 0), (0, REL_PAD - rel_bias.shape[2])))
    odd_w = (norm_mix_odd[0], w_in_odd[0].astype(BF16), table, w_out_odd[0].astype(BF16))
    wr = jnp.pad(w_router[0], ((0, 0), (0, LANES - N_EXPERTS)))

    mp, ms = bp * tp, bs * ts
    h_p = x_prompt.reshape(mp, d)
    h_s = x_sample.reshape(ms, d)

    h_p, (fk_p, fv_p, fl_p, dk_p, dv_p) = _even_layer(h_p, bp, tp, None, even_w, False)
    caches_even = (cache_fox_k[0], cache_fox_v[0], cache_fox_logf[0], cache_diff_k[0], cache_diff_v[0])
    h_s, (fk_s, fv_s, fl_s, dk_s, dv_s) = _even_layer(h_s, bs, ts, caches_even, even_w, True)

    h_p, (bk_p, bv_p) = _odd_attention(h_p, bp, tp, None, odd_w, False)
    h_s, (bk_s, bv_s) = _odd_attention(h_s, bs, ts, (cache_band_k[0], cache_band_v[0], past), odd_w, True)

    m_all = mp + ms
    assert mp % 512 == 0 and mp % ms == 0
    zero_counts = jnp.zeros((1, LANES), F32)
    idx_p, gate_p, cnt_p, u_all = router(h_p, norm_ffn_odd[0], wr, zero_counts, 512, m_all, 0)
    idx_s, gate_s, cnt, u_all = router(h_s, norm_ffn_odd[0], wr, cnt_p, ms, m_all, mp, u_dst=u_all)
    idx = jnp.concatenate([idx_p, idx_s], axis=0)
    counts = cnt[0, :N_EXPERTS].astype(I32)
    n_tiles_e = (counts + MOE_TILE - 1) // MOE_TILE
    tile_end = jnp.cumsum(n_tiles_e)
    tile_start = tile_end - n_tiles_e
    row_off = tile_start * MOE_TILE
    n_used = tile_end[-1:]
    nt = (2 * m_all) // MOE_TILE + N_EXPERTS
    tile_ids = jnp.arange(nt, dtype=I32)
    last = jnp.maximum(n_used[0] - 1, 0)
    tile_expert = jnp.sum((jnp.minimum(tile_ids, last)[:, None] >= tile_end[None, :]).astype(I32), axis=1)
    tile_expert = jnp.minimum(tile_expert, N_EXPERTS - 1)
    tile_rows = jnp.clip(counts[tile_expert] - (tile_ids - tile_start[tile_expert]) * MOE_TILE, 0, MOE_TILE)
    pos1 = row_off[idx[:, 0]] + idx[:, 2]
    pos2 = row_off[idx[:, 1]] + idx[:, 3]
    tokens = jnp.arange(m_all, dtype=I32)
    row_token = jnp.zeros((nt * MOE_TILE,), I32).at[pos1].set(tokens).at[pos2].set(tokens)
    xs = jnp.take(u_all, row_token, axis=0, mode="clip")
    y = ffn_moe(tile_expert, n_used.astype(I32), tile_rows.astype(I32), xs, w_gate_e[0], w_up_e[0], w_down_e[0])
    y1 = jnp.take(y, pos1, axis=0, mode="clip")
    y2 = jnp.take(y, pos2, axis=0, mode="clip")
    y_p = combine_norm(h_p, y1, y2, gate_p, norm_final, 512, 0)
    y_s = combine_norm(h_s, y1, y2, gate_s, norm_final, ms, mp)

    keep = min(BAND_PAST, tp)
    bk_p = bk_p.reshape(bp, tp, H_BAND, HEAD_DIM)[:, tp - keep:][None]
    bv_p = bv_p.reshape(bp, tp, H_BAND, HEAD_DIM)[:, tp - keep:][None]
    return (
        y_p.reshape(bp, tp, d), y_s.reshape(bs, ts, d),
        fk_p.reshape(1, bp, tp, H_FOX, HEAD_DIM), fk_s.reshape(1, bs, ts, H_FOX, HEAD_DIM),
        fv_p.reshape(1, bp, tp, H_FOX, HEAD_DIM), fv_s.reshape(1, bs, ts, H_FOX, HEAD_DIM),
        fl_p.reshape(1, bp, tp, H_FOX), fl_s.reshape(1, bs, ts, H_FOX),
        dk_p.reshape(1, bp, tp, H_DIFF, 2, HEAD_DIM), dk_s.reshape(1, bs, ts, H_DIFF, 2, HEAD_DIM),
        dv_p.reshape(1, bp, tp, H_DIFF, 2 * HEAD_DIM), dv_s.reshape(1, bs, ts, H_DIFF, 2 * HEAD_DIM),
        bk_p, bk_s.reshape(1, bs, ts, H_BAND, HEAD_DIM),
        bv_p, bv_s.reshape(1, bs, ts, H_BAND, HEAD_DIM),
    )
```

```python
import functools
import math

import jax
import jax.numpy as jnp
from jax import lax
from jax.experimental import pallas as pl
from jax.experimental.pallas import tpu as pltpu

F32 = jnp.float32
BF16 = jnp.bfloat16
I32 = jnp.int32

D_MODEL = 2048
HEAD_DIM = 128
CHUNK = 64
H_FOX = 8
H_DIFF = 4
H_BAND = 16
N_PREV_CHUNKS = 8
BAND_PAST = N_PREV_CHUNKS * CHUNK
REL_CLIP = 128
D_FF = 5632
N_EXPERTS = 8
EPS = 1e-6
SCALE = HEAD_DIM ** -0.5
LAM_INIT_L0 = 0.8 - 0.6 * math.exp(-0.3 * 0)
NEG_INF = float("-inf")
LOG2E = 1.4426950408889634
Q_SCALE = SCALE * LOG2E

LANES = 128
VMEM_LIMIT = 56 * 1024 * 1024
MOE_TILE = 1024


def _params(sem):
    return pltpu.CompilerParams(dimension_semantics=sem, vmem_limit_bytes=VMEM_LIMIT)


def _rms(x, g):
    return (x * lax.rsqrt(jnp.mean(x * x, axis=-1, keepdims=True) + EPS)) * g


def _log_sigmoid(x):
    return jnp.minimum(x, 0.0) - jnp.log1p(jnp.exp(-jnp.abs(x)))


def _norm_proj_kernel(*refs, n_groups, nb, has_forget, scales):
    if has_forget:
        x_ref, g_ref, w_ref, wf_ref, bf_ref = refs[:5]
        outs = refs[5:5 + n_groups]
        logf_ref = refs[5 + n_groups]
        xn_ref = refs[6 + n_groups]
    else:
        x_ref, g_ref, w_ref = refs[:3]
        outs = refs[3:3 + n_groups]
        xn_ref = refs[3 + n_groups]
    j = pl.program_id(1)

    @pl.when(j == 0)
    def _():
        xn = _rms(x_ref[...], g_ref[...]).astype(BF16)
        xn_ref[...] = xn
        if has_forget:
            fa = jnp.dot(xn, wf_ref[...], preferred_element_type=F32)
            logf_ref[...] = _log_sigmoid(fa + bf_ref[...])[:, :H_FOX]

    z = jnp.dot(xn_ref[...], w_ref[...], preferred_element_type=F32)
    for g in range(n_groups):
        @pl.when(j // nb == g)
        def _(g=g):
            zg = z if scales[g] == 1.0 else z * scales[g]
            outs[g][...] = zg.astype(outs[g].dtype)


def norm_proj(x, g, w, group_dtypes, group_width, scales, wf=None, bf=None, tm=512, tn=512):
    m, d = x.shape
    n = w.shape[1]
    n_groups = len(group_dtypes)
    assert n == n_groups * group_width and group_width % tn == 0 and m % tm == 0
    nb = group_width // tn
    has_forget = wf is not None
    in_specs = [
        pl.BlockSpec((tm, d), lambda i, j: (i, 0)),
        pl.BlockSpec((1, d), lambda i, j: (0, 0)),
        pl.BlockSpec((d, tn), lambda i, j: (0, j)),
    ]
    args = [x, g.reshape(1, d), w]
    if has_forget:
        in_specs += [pl.BlockSpec((d, LANES), lambda i, j: (0, 0)),
                     pl.BlockSpec((1, LANES), lambda i, j: (0, 0))]
        args += [wf, bf]
    out_specs = [pl.BlockSpec((tm, tn), lambda i, j, g=g: (i, jnp.clip(j - g * nb, 0, nb - 1)))
                 for g in range(n_groups)]
    out_shape = [jax.ShapeDtypeStruct((m, group_width), dt) for dt in group_dtypes]
    if has_forget:
        out_specs.append(pl.BlockSpec((tm, H_FOX), lambda i, j: (i, 0)))
        out_shape.append(jax.ShapeDtypeStruct((m, H_FOX), F32))
    return pl.pallas_call(
        functools.partial(_norm_proj_kernel, n_groups=n_groups, nb=nb, has_forget=has_forget,
                          scales=tuple(float(s) for s in scales)),
        grid=(m // tm, n // tn),
        in_specs=in_specs,
        out_specs=out_specs,
        out_shape=out_shape,
        scratch_shapes=[pltpu.VMEM((tm, d), BF16)],
        compiler_params=_params(("parallel", "arbitrary")),
        name="norm_proj",
    )(*args)


def _proj_res_kernel(*refs, n_in):
    xs = refs[:n_in]
    ws = refs[n_in:2 * n_in]
    h_ref = refs[2 * n_in]
    o_ref = refs[2 * n_in + 1]
    acc = h_ref[...]
    for x_ref, w_ref in zip(xs, ws):
        acc = acc + jnp.dot(x_ref[...], w_ref[...], preferred_element_type=F32)
    o_ref[...] = acc


def proj_residual(xs, ws, h, tm=512, tn=1024):
    m, n = h.shape
    assert m % tm == 0 and n % tn == 0
    n_in = len(xs)
    in_specs = [pl.BlockSpec((tm, x.shape[1]), lambda i, j: (i, 0)) for x in xs]
    in_specs += [pl.BlockSpec((w.shape[0], tn), lambda i, j: (0, j)) for w in ws]
    in_specs.append(pl.BlockSpec((tm, tn), lambda i, j: (i, j)))
    return pl.pallas_call(
        functools.partial(_proj_res_kernel, n_in=n_in),
        grid=(m // tm, n // tn),
        in_specs=in_specs,
        out_specs=pl.BlockSpec((tm, tn), lambda i, j: (i, j)),
        out_shape=jax.ShapeDtypeStruct((m, n), F32),
        compiler_params=_params(("parallel", "arbitrary")),
        name="proj_residual",
    )(*xs, *ws, h)


def _swiglu_step(x, wg_ref, wu_ref, wd_ref):
    a = jnp.dot(x, wg_ref[...], preferred_element_type=F32)
    b = jnp.dot(x, wu_ref[...], preferred_element_type=F32)
    mid = (a * jax.nn.sigmoid(a)) * b
    return jnp.dot(mid.astype(BF16), wd_ref[...], preferred_element_type=F32)


def _ffn_dense_kernel(h_ref, g_ref, wg_ref, wu_ref, wd_ref, o_ref, xn_ref, acc_ref):
    j = pl.program_id(1)

    @pl.when(j == 0)
    def _():
        h = h_ref[...]
        xn_ref[...] = _rms(h, g_ref[...]).astype(BF16)
        acc_ref[...] = h

    acc_ref[...] += _swiglu_step(xn_ref[...], wg_ref, wu_ref, wd_ref)

    @pl.when(j == pl.num_programs(1) - 1)
    def _():
        o_ref[...] = acc_ref[...]


def ffn_dense(h, g, wg, wu, wd, tm=512, tf=512):
    m, d = h.shape
    f = wg.shape[1]
    assert m % tm == 0 and f % tf == 0
    return pl.pallas_call(
        _ffn_dense_kernel,
        grid=(m // tm, f // tf),
        in_specs=[
            pl.BlockSpec((tm, d), lambda i, j: (i, 0)),
            pl.BlockSpec((1, d), lambda i, j: (0, 0)),
            pl.BlockSpec((d, tf), lambda i, j: (0, j)),
            pl.BlockSpec((d, tf), lambda i, j: (0, j)),
            pl.BlockSpec((tf, d), lambda i, j: (j, 0)),
        ],
        out_specs=pl.BlockSpec((tm, d), lambda i, j: (i, 0)),
        out_shape=jax.ShapeDtypeStruct((m, d), F32),
        scratch_shapes=[pltpu.VMEM((tm, d), BF16), pltpu.VMEM((tm, d), F32)],
        compiler_params=_params(("parallel", "arbitrary")),
        name="ffn_dense",
    )(h, g.reshape(1, d), wg, wu, wd)


def _ffn_moe_kernel(te_ref, nu_ref, tr_ref, x_ref, wg_ref, wu_ref, wd_ref, o_ref, mid_ref, *, nj_a):
    t = pl.program_id(0)
    j = pl.program_id(1)
    half = MOE_TILE // 2

    @pl.when(t < nu_ref[0])
    def _():
        @pl.when(j < nj_a)
        def _():
            wg = wg_ref[...].astype(BF16)
            wu = wu_ref[...].astype(BF16)
            for s in range(2):
                @pl.when(s * half < tr_ref[t])
                def _(s=s):
                    x = x_ref[s * half:(s + 1) * half, :]
                    a = jnp.dot(x, wg, preferred_element_type=F32)
                    b = jnp.dot(x, wu, preferred_element_type=F32)
                    mid_ref[s, j] = ((a * jax.nn.sigmoid(a)) * b).astype(BF16)

        @pl.when(j >= nj_a)
        def _():
            wd = wd_ref[...].astype(BF16)
            for s in range(2):
                rows = slice(s * half, (s + 1) * half)

                @pl.when(s * half < tr_ref[t])
                def _(s=s, rows=rows):
                    mid = jnp.concatenate([mid_ref[s, jm] for jm in range(nj_a)], axis=1)
                    o_ref[rows, :] = jnp.dot(mid, wd, preferred_element_type=F32)

                @pl.when(s * half >= tr_ref[t])
                def _(rows=rows):
                    o_ref[rows, :] = jnp.zeros((half, o_ref.shape[1]), F32)


def ffn_moe(tile_expert, n_used, tile_rows, xs, wg, wu, wd, tf=256, tn=256):
    p, d = xs.shape
    f = wg.shape[2]
    tm = MOE_TILE
    nt, nj_a, nj_b = p // tm, f // tf, d // tn

    def used(t, nu):
        return t < nu[0]

    def row(t, nu):
        return jnp.maximum(jnp.minimum(t, nu[0] - 1), 0)

    def col_a(t, j, nu):
        return jnp.where(used(t, nu), jnp.minimum(j, nj_a - 1), nj_a - 1)

    def col_b(t, j, nu):
        return jnp.where(used(t, nu), jnp.clip(j - nj_a, 0, nj_b - 1), nj_b - 1)

    grid_spec = pltpu.PrefetchScalarGridSpec(
        num_scalar_prefetch=3,
        grid=(nt, nj_a + nj_b),
        in_specs=[
            pl.BlockSpec((tm, d), lambda t, j, te, nu, tr: (row(t, nu), 0)),
            pl.BlockSpec((None, d, tf), lambda t, j, te, nu, tr: (te[t], 0, col_a(t, j, nu))),
            pl.BlockSpec((None, d, tf), lambda t, j, te, nu, tr: (te[t], 0, col_a(t, j, nu))),
            pl.BlockSpec((None, f, tn), lambda t, j, te, nu, tr: (te[t], 0, col_b(t, j, nu))),
        ],
        out_specs=pl.BlockSpec((tm, tn), lambda t, j, te, nu, tr: (row(t, nu), col_b(t, j, nu))),
        scratch_shapes=[pltpu.VMEM((2, nj_a, tm // 2, tf), BF16)],
    )
    return pl.pallas_call(
        functools.partial(_ffn_moe_kernel, nj_a=nj_a),
        grid_spec=grid_spec,
        out_shape=jax.ShapeDtypeStruct((p, d), F32),
        compiler_params=_params(("arbitrary", "arbitrary")),
        name="ffn_moe",
    )(tile_expert, n_used, tile_rows, xs, wg, wu, wd)


def _router_kernel(h_ref, g_ref, wr_ref, c0_ref, idx_ref, gate_ref, cnt_ref, u_ref, base_ref, *, tm):
    i = pl.program_id(0)

    @pl.when(i == 0)
    def _():
        base_ref[...] = c0_ref[...]

    u = _rms(h_ref[...], g_ref[...])
    u_ref[...] = u.astype(BF16)
    logits = jnp.dot(u, wr_ref[...], preferred_element_type=F32, precision=lax.Precision.HIGHEST)
    lane = lax.broadcasted_iota(I32, (tm, LANES), 1)
    logits = jnp.where(lane < N_EXPERTS, logits, NEG_INF)
    m1 = jnp.max(logits, axis=1, keepdims=True)
    i1 = jnp.min(jnp.where(logits == m1, lane, LANES), axis=1, keepdims=True)
    rest = jnp.where(lane == i1, NEG_INF, logits)
    m2 = jnp.max(rest, axis=1, keepdims=True)
    i2 = jnp.min(jnp.where(rest == m2, lane, LANES), axis=1, keepdims=True)
    e2 = jnp.exp(m2 - m1)
    g1 = 1.0 / (1.0 + e2)
    g2 = e2 / (1.0 + e2)

    sel1 = lane == i1
    sel2 = lane == i2
    onehot = jnp.where(sel1 | sel2, 1.0, 0.0)
    r = lax.broadcasted_iota(I32, (tm, tm), 0)
    c = lax.broadcasted_iota(I32, (tm, tm), 1)
    strict_lower = jnp.where(c < r, 1.0, 0.0).astype(BF16)
    before = jnp.dot(strict_lower, onehot.astype(BF16), preferred_element_type=F32) + base_ref[...]
    r1 = jnp.sum(jnp.where(sel1, before, 0.0), axis=1, keepdims=True)
    r2 = jnp.sum(jnp.where(sel2, before, 0.0), axis=1, keepdims=True)
    base_ref[...] += jnp.sum(onehot, axis=0, keepdims=True)

    idx = jnp.where(lane == 0, i1, jnp.where(lane == 1, i2,
          jnp.where(lane == 2, r1.astype(I32), r2.astype(I32))))
    idx_ref[...] = idx[:, :8]
    gate_ref[...] = jnp.where(lane == 0, g1, g2)[:, :8]

    @pl.when(i == pl.num_programs(0) - 1)
    def _():
        cnt_ref[...] = base_ref[...]


def _router_kernel_dst(h_ref, g_ref, wr_ref, c0_ref, dst_ref, *rest, tm):
    del dst_ref
    _router_kernel(h_ref, g_ref, wr_ref, c0_ref, *rest, tm=tm)


def router(h, g, wr, count0, tm, u_rows, u_row0, u_dst=None):
    m, d = h.shape
    assert m % tm == 0 and u_row0 % tm == 0
    blk0 = u_row0 // tm
    in_specs = [
        pl.BlockSpec((tm, d), lambda i: (i, 0)),
        pl.BlockSpec((1, d), lambda i: (0, 0)),
        pl.BlockSpec((d, LANES), lambda i: (0, 0)),
        pl.BlockSpec((1, LANES), lambda i: (0, 0)),
    ]
    args = [h, g.reshape(1, d), wr, count0]
    aliases = {}
    if u_dst is not None:
        in_specs.append(pl.BlockSpec(memory_space=pl.ANY))
        args.append(u_dst)
        aliases = {4: 3}
    return pl.pallas_call(
        functools.partial(_router_kernel if u_dst is None else _router_kernel_dst, tm=tm),
        grid=(m // tm,),
        in_specs=in_specs,
        out_specs=[
            pl.BlockSpec((tm, 8), lambda i: (i, 0)),
            pl.BlockSpec((tm, 8), lambda i: (i, 0)),
            pl.BlockSpec((1, LANES), lambda i: (0, 0)),
            pl.BlockSpec((tm, d), lambda i: (i + blk0, 0)),
        ],
        out_shape=[
            jax.ShapeDtypeStruct((m, 8), I32),
            jax.ShapeDtypeStruct((m, 8), F32),
            jax.ShapeDtypeStruct((1, LANES), F32),
            jax.ShapeDtypeStruct((u_rows, d), BF16),
        ],
        scratch_shapes=[pltpu.VMEM((1, LANES), F32)],
        input_output_aliases=aliases,
        compiler_params=_params(("arbitrary",)),
        name="router",
    )(*args)


def _combine_kernel(h_ref, y1_ref, y2_ref, gate_ref, g_ref, o_ref):
    gates = gate_ref[...]
    g1 = gates[:, 0:1]
    g2 = gates[:, 1:2]
    h = h_ref[...] + (g1 * y1_ref[...] + g2 * y2_ref[...])
    o_ref[...] = _rms(h, g_ref[...])


def combine_norm(h, y1, y2, gates, g, tm, y_row0):
    m, d = h.shape
    assert m % tm == 0 and y_row0 % tm == 0
    blk0 = y_row0 // tm
    row = pl.BlockSpec((tm, d), lambda i: (i, 0))
    yrow = pl.BlockSpec((tm, d), lambda i: (i + blk0, 0))
    return pl.pallas_call(
        _combine_kernel,
        grid=(m // tm,),
        in_specs=[row, yrow, yrow, pl.BlockSpec((tm, 8), lambda i: (i, 0)),
                  pl.BlockSpec((1, d), lambda i: (0, 0))],
        out_specs=row,
        out_shape=jax.ShapeDtypeStruct((m, d), F32),
        compiler_params=_params(("parallel",)),
        name="combine_norm",
    )(h, y1, y2, gates, g.reshape(1, d))


def _cumsum_kernel(x_ref, o_ref, *, tb, n_blk):
    r = lax.broadcasted_iota(I32, (tb, tb), 0)
    c = lax.broadcasted_iota(I32, (tb, tb), 1)
    upper = jnp.where(r <= c, 1.0, 0.0)
    carry = jnp.zeros((H_FOX, 1), F32)
    for b in range(n_blk):
        blk = x_ref[0, :, b * tb:(b + 1) * tb]
        cs = jnp.dot(blk, upper, preferred_element_type=F32, precision=lax.Precision.HIGHEST) + carry
        o_ref[0, :, b * tb:(b + 1) * tb] = cs
        carry = cs[:, tb - 1:tb]


def cumsum_time(x, tb):
    b, h, t = x.shape
    assert t % tb == 0
    spec = pl.BlockSpec((1, h, t), lambda i: (i, 0, 0))
    return pl.pallas_call(
        functools.partial(_cumsum_kernel, tb=tb, n_blk=t // tb),
        grid=(b,),
        in_specs=[spec],
        out_specs=spec,
        out_shape=jax.ShapeDtypeStruct((b, h, t), F32),
        compiler_params=_params(("parallel",)),
        name="cumsum_time",
    )(x)


def _rows_of(parts, r0, n):
    pieces, start = [], 0
    for part in parts:
        lo, hi = max(r0, start), min(r0 + n, start + part.shape[0])
        if lo < hi:
            pieces.append(part[lo - start:hi - start, :].astype(F32))
        start += part.shape[0]
    got = sum(p.shape[0] for p in pieces)
    if got < n:
        pieces.append(jnp.zeros((n - got, parts[0].shape[1]), F32))
    return pieces[0] if len(pieces) == 1 else jnp.concatenate(pieces, axis=0)


def _split3(x):
    hi = x.astype(BF16).astype(F32)
    r = x - hi
    mid = r.astype(BF16).astype(F32)
    return hi, mid, (r - mid).astype(BF16).astype(F32)


def _bias_cols(col):
    hi, mid, lo = _split3(col)
    lane = lax.broadcasted_iota(I32, (col.shape[0], LANES), 1)
    e = jnp.where(lane == 0, hi, jnp.where(lane == 1, mid, jnp.where(lane == 2, lo,
        jnp.where(lane < 6, 1.0, 0.0))))
    return e.astype(BF16)


def _bias_rows(row):
    hi, mid, lo = _split3(row)
    sub = lax.broadcasted_iota(I32, (LANES, row.shape[1]), 0)
    e = jnp.where(sub < 3, 1.0, jnp.where(sub == 3, hi, jnp.where(sub == 4, mid,
        jnp.where(sub == 5, lo, 0.0))))
    return e.astype(BF16)


def _ones_col(n):
    return jnp.where(lax.broadcasted_iota(I32, (n, LANES), 1) == 0, 1.0, 0.0).astype(BF16)


def _fox_kernel(*refs, tq, tk, q_off, n_q, has_cache):
    if has_cache:
        q_ref, k_ref, v_ref, kc_ref, vc_ref, cq_ref, ck_ref, o_ref, kt_ref, vb_ref, m_ref, acc_ref = refs
        k_parts, v_parts = (kc_ref, k_ref), (vc_ref, v_ref)
    else:
        q_ref, k_ref, v_ref, cq_ref, ck_ref, o_ref, kt_ref, vb_ref, m_ref, acc_ref = refs
        k_parts, v_parts = (k_ref,), (v_ref,)
    h = pl.program_id(1)
    qi = pl.program_id(2)

    @pl.when(qi == 0)
    def _():
        for blk in range(kt_ref.shape[0]):
            r0 = blk * tk
            kt_ref[blk, 0:HEAD_DIM, :] = _rows_of(k_parts, r0, tk).T.astype(BF16)
            kt_ref[blk, HEAD_DIM:, :] = _bias_rows(-LOG2E * ck_ref[0, h, blk])
            vb_ref[r0:r0 + tk, 0:HEAD_DIM] = _rows_of(v_parts, r0, tk).astype(BF16)
            vb_ref[r0:r0 + tk, HEAD_DIM:] = _ones_col(tk)

    qpos0 = q_off if n_q == 1 else q_off + qi * tq
    cq_row = LOG2E * cq_ref[0, h, q_off // tq + qi]
    eye = lax.broadcasted_iota(I32, (tq, tq), 0) == lax.broadcasted_iota(I32, (tq, tq), 1)
    cq = jnp.sum(jnp.where(eye, cq_row, 0.0), axis=1, keepdims=True)
    qx = jnp.concatenate([q_ref[...], _bias_cols(cq)], axis=1)
    row = lax.broadcasted_iota(I32, (tq, tk), 0)
    col = lax.broadcasted_iota(I32, (tq, tk), 1)

    def scores(kv):
        return jnp.dot(qx, kt_ref[kv], preferred_element_type=F32)

    def consume(s, kv, masked):
        if masked:
            s = jnp.where(kv * tk + col <= qpos0 + row, s, NEG_INF)
        m = m_ref[...]
        m_new = jnp.maximum(m, jnp.max(s, axis=1, keepdims=True))
        p = jnp.exp2(s - m_new).astype(BF16)
        k0 = pl.multiple_of(kv * tk, tk)
        pv = jnp.dot(p, vb_ref[pl.ds(k0, tk), :], preferred_element_type=F32)
        acc_ref[...] = jnp.exp2(m - m_new) * acc_ref[...] + pv
        m_ref[...] = m_new

    m_ref[...] = jnp.full(m_ref.shape, NEG_INF, F32)
    acc_ref[...] = jnp.zeros(acc_ref.shape, F32)
    n_full = qpos0 // tk

    def step(kv, s):
        s_next = scores(kv + 1)
        consume(s, kv, False)
        return s_next

    s_last = lax.fori_loop(0, n_full, step, scores(0))
    consume(s_last, n_full, True)
    acc = acc_ref[...]
    o_ref[...] = (acc[:, :HEAD_DIM] / acc[:, HEAD_DIM:HEAD_DIM + 1]).astype(o_ref.dtype)


def fox_attention(q, k, v, c, n_batch, t_q, t_new, q_off, tq, tk, cache=None):
    n_q = t_q // tq
    tc = c.shape[2]
    assert t_q % tq == 0 and q_off % tq == 0 and tc % tq == 0 and tc % tk == 0
    assert (q_off % tk) + tq <= tk if n_q == 1 else (tq == tk and q_off % tk == 0)
    c_q = c.reshape(n_batch, H_FOX, tc // tq, 1, tq)
    c_k = c.reshape(n_batch, H_FOX, tc // tk, 1, tk)
    new = pl.BlockSpec((t_new, HEAD_DIM), lambda b, h, i: (b, h))
    in_specs = [pl.BlockSpec((tq, HEAD_DIM), lambda b, h, i: (b * n_q + i, h)), new, new]
    args = [q, k, v]
    if cache is not None:
        past = cache[0].shape[1]
        assert past + t_new <= tc
        old = pl.BlockSpec((past, HEAD_DIM), lambda b, h, i: (b, h))
        in_specs += [old, old]
        args += [x.reshape(n_batch * past, H_FOX * HEAD_DIM) for x in cache]
    else:
        assert t_new == tc
    in_specs += [pl.BlockSpec((1, H_FOX, tc // tq, 1, tq), lambda b, h, i: (b, 0, 0, 0, 0)),
                 pl.BlockSpec((1, H_FOX, tc // tk, 1, tk), lambda b, h, i: (b, 0, 0, 0, 0))]
    return pl.pallas_call(
        functools.partial(_fox_kernel, tq=tq, tk=tk, q_off=q_off, n_q=n_q, has_cache=cache is not None),
        grid=(n_batch, H_FOX, n_q),
        in_specs=in_specs,
        out_specs=pl.BlockSpec((tq, HEAD_DIM), lambda b, h, i: (b * n_q + i, h)),
        out_shape=jax.ShapeDtypeStruct((n_batch * t_q, H_FOX * HEAD_DIM), BF16),
        scratch_shapes=[pltpu.VMEM((tc // tk, 2 * HEAD_DIM, tk), BF16), pltpu.VMEM((tc, 2 * HEAD_DIM), BF16),
                        pltpu.VMEM((tq, 1), F32), pltpu.VMEM((tq, 2 * HEAD_DIM), F32)],
        compiler_params=_params(("parallel", "parallel", "arbitrary")),
        name="fox_attention",
    )(*args, c_q, c_k)


def _diff_kernel(slope_ref, q_ref, k_ref, v_ref, *refs, tq, tk, q_off, n_q, n_valid, has_cache):
    if has_cache:
        kc0_ref, kc1_ref, vc_ref = refs[:3]
        refs = refs[3:]
    lq1_ref, lk1_ref, lq2_ref, lk2_ref, sg_ref, o_ref, kt_ref, vb_ref, m_ref, l_ref, acc_ref = refs
    h = pl.program_id(1)
    qi = pl.program_id(2)
    dv = 2 * HEAD_DIM
    slope2 = slope_ref[h] * LOG2E

    @pl.when(qi == 0)
    def _():
        v_parts = (vc_ref, v_ref) if has_cache else (v_ref,)
        for blk in range(kt_ref.shape[1]):
            r0 = blk * tk
            kpos_row = (r0 + lax.broadcasted_iota(I32, (1, tk), 1)).astype(F32)
            key_bias = _bias_rows(slope2 * kpos_row)
            for mp in range(2):
                new_rows = k_ref.at[:, mp * HEAD_DIM:(mp + 1) * HEAD_DIM]
                k_parts = ((kc0_ref, kc1_ref)[mp], new_rows) if has_cache else (new_rows,)
                kt_ref[mp, blk, 0:HEAD_DIM, :] = _rows_of(k_parts, r0, tk).T.astype(BF16)
                kt_ref[mp, blk, HEAD_DIM:, :] = key_bias
            vb_ref[r0:r0 + tk, :] = _rows_of(v_parts, r0, tk).astype(BF16)

    lam = (jnp.exp(jnp.sum(lq1_ref[...] * lk1_ref[...], axis=1, keepdims=True))
           - jnp.exp(jnp.sum(lq2_ref[...] * lk2_ref[...], axis=1, keepdims=True)) + LAM_INIT_L0)
    qpos0 = q_off if n_q == 1 else q_off + qi * tq
    qpos_col = (qpos0 + lax.broadcasted_iota(I32, (tq, 1), 0)).astype(F32)
    query_bias = _bias_cols(-slope2 * qpos_col)
    q = q_ref[...]
    qx = [jnp.concatenate([q[:, mp * HEAD_DIM:(mp + 1) * HEAD_DIM], query_bias], axis=1) for mp in range(2)]
    row = lax.broadcasted_iota(I32, (tq, tk), 0)
    col = lax.broadcasted_iota(I32, (tq, tk), 1)

    def scores(kv):
        return tuple(jnp.dot(qx[mp], kt_ref[mp, kv], preferred_element_type=F32) for mp in range(2))

    def consume(ss, kv, masked):
        k0 = pl.multiple_of(kv * tk, tk)
        vblk = vb_ref[pl.ds(k0, tk), :]
        if masked:
            qpos = qpos0 + row
            kpos = kv * tk + col
            visible = ((kpos >> 6) <= (qpos >> 6)) & (kpos < n_valid)
            fix = (2.0 * slope2) * jnp.maximum(kpos - qpos, 0).astype(F32)
        for mp in range(2):
            s = ss[mp]
            if masked:
                s = jnp.where(visible, s - fix, NEG_INF)
            m = m_ref[mp]
            m_new = jnp.maximum(m, jnp.max(s, axis=1, keepdims=True))
            alpha = jnp.exp2(m - m_new)
            p = jnp.exp2(s - m_new)
            l_ref[mp] = alpha * l_ref[mp] + jnp.sum(p, axis=1, keepdims=True)
            acc_ref[mp] = alpha * acc_ref[mp] + jnp.dot(p.astype(BF16), vblk, preferred_element_type=F32)
            m_ref[mp] = m_new

    m_ref[...] = jnp.full(m_ref.shape, NEG_INF, F32)
    l_ref[...] = jnp.zeros(l_ref.shape, F32)
    acc_ref[...] = jnp.zeros(acc_ref.shape, F32)
    n_full = qpos0 // tk

    def step(kv, ss):
        ss_next = scores(kv + 1)
        consume(ss, kv, False)
        return ss_next

    ss_last = lax.fori_loop(0, n_full, step, scores(0))
    consume(ss_last, n_full, True)
    o = acc_ref[0] / l_ref[0] - lam * (acc_ref[1] / l_ref[1])
    o_ref[...] = (_rms(o, sg_ref[...]) * (1.0 - LAM_INIT_L0)).astype(o_ref.dtype)


def diff_attention(q, k, v, lam_vecs, subln_g, n_batch, t_q, t_new, t_s, q_off, tq, tk, cache=None):
    n_q = t_q // tq
    assert t_q % tq == 0 and t_s % tk == 0
    dv = 2 * HEAD_DIM
    slopes = 2.0 ** (-8.0 * jnp.arange(1, H_DIFF + 1, dtype=F32) / H_DIFF)
    vec = pl.BlockSpec((1, HEAD_DIM), lambda b, h, i, s: (0, 0))
    new = pl.BlockSpec((t_new, dv), lambda b, h, i, s: (b, h))
    in_specs = [pl.BlockSpec((tq, dv), lambda b, h, i, s: (b * n_q + i, h)), new, new]
    args = [slopes, q, k, v]
    past = 0
    if cache is not None:
        past = cache[0].shape[1]
        in_specs += [
            pl.BlockSpec((past, HEAD_DIM), lambda b, h, i, s: (b, 2 * h)),
            pl.BlockSpec((past, HEAD_DIM), lambda b, h, i, s: (b, 2 * h + 1)),
            pl.BlockSpec((past, dv), lambda b, h, i, s: (b, h)),
        ]
        ck2d = cache[0].reshape(n_batch * past, H_DIFF * dv)
        args += [ck2d, ck2d, cache[1].reshape(n_batch * past, H_DIFF * dv)]
    n_valid = past + t_new
    assert n_valid <= t_s and tq % CHUNK == 0 or n_q == 1
    if n_q == 1:
        last_visible = min(((q_off + tq - 1) // CHUNK + 1) * CHUNK, n_valid)
        assert (q_off // tk) * tk <= n_valid and last_visible <= (q_off // tk + 1) * tk
    else:
        assert tq == tk and q_off % tk == 0
    in_specs += [vec, vec, vec, vec, pl.BlockSpec((1, dv), lambda b, h, i, s: (0, 0))]
    grid_spec = pltpu.PrefetchScalarGridSpec(
        num_scalar_prefetch=1,
        grid=(n_batch, H_DIFF, n_q),
        in_specs=in_specs,
        out_specs=pl.BlockSpec((tq, dv), lambda b, h, i, s: (b * n_q + i, h)),
        scratch_shapes=[pltpu.VMEM((2, t_s // tk, dv, tk), BF16), pltpu.VMEM((t_s, dv), BF16),
                        pltpu.VMEM((2, tq, 1), F32), pltpu.VMEM((2, tq, 1), F32), pltpu.VMEM((2, tq, dv), F32)],
    )
    return pl.pallas_call(
        functools.partial(_diff_kernel, tq=tq, tk=tk, q_off=q_off, n_q=n_q, n_valid=n_valid,
                          has_cache=cache is not None),
        grid_spec=grid_spec,
        out_shape=jax.ShapeDtypeStruct((n_batch * t_q, H_DIFF * dv), BF16),
        compiler_params=_params(("parallel", "parallel", "arbitrary")),
        name="diff_attention",
    )(*args, *[x.reshape(1, HEAD_DIM) for x in lam_vecs], subln_g.reshape(1, dv))


REL_PAD = 384


def _band_bias_kernel(t_ref, o_ref, *, nk, n_valid):
    blk = pl.program_id(0)
    row = lax.broadcasted_iota(I32, (REL_PAD, nk), 0)
    j = lax.broadcasted_iota(I32, (REL_PAD, nk), 1)
    jc = lax.broadcasted_iota(I32, (1, nk), 1) >> 6
    jv = lax.broadcasted_iota(I32, (1, nk), 1) < n_valid
    for r in range(8):
        i = blk * 8 + r
        idx = jnp.clip(i - j + BAND_PAST, -REL_CLIP, REL_CLIP) + REL_CLIP
        onehot = jnp.where(row == idx, 1.0, 0.0)
        bias = jnp.dot(t_ref[...], onehot, preferred_element_type=F32, precision=lax.Precision.HIGHEST)
        ic = i >> 6
        visible = (jc - N_PREV_CHUNKS <= ic) & (jc >= ic) & jv
        o_ref[:, r, :] = jnp.where(visible, bias * LOG2E, NEG_INF)


def band_bias(table_padded, tq, nk, n_valid):
    assert CHUNK == 64
    return pl.pallas_call(
        functools.partial(_band_bias_kernel, nk=nk, n_valid=n_valid),
        grid=(tq // 8,),
        in_specs=[pl.BlockSpec((H_BAND, REL_PAD), lambda i: (0, 0))],
        out_specs=pl.BlockSpec((H_BAND, 8, nk), lambda i: (0, i, 0)),
        out_shape=jax.ShapeDtypeStruct((H_BAND, tq, nk), F32),
        compiler_params=_params(("parallel",)),
        name="band_bias",
    )(table_padded)


def _band_kernel(*refs, tq, nk, n_g, q_off, has_cache):
    if has_cache:
        q_ref, k_ref, v_ref, kc_ref, vc_ref, b_ref, o_ref, kb_ref, vb_ref = refs
        k_parts, v_parts = (kc_ref, k_ref), (vc_ref, v_ref)
        front = 0
    else:
        q_ref, k_ref, v_ref, b_ref, o_ref, kb_ref, vb_ref = refs
        k_parts, v_parts = (k_ref,), (v_ref,)
        front = kb_ref.shape[0] - k_ref.shape[0]
        kb_ref[0:front, :] = jnp.zeros((front, HEAD_DIM), BF16)
        vb_ref[0:front, 0:HEAD_DIM] = jnp.zeros((front, HEAD_DIM), BF16)
    t_s = kb_ref.shape[0]
    kb_ref[front:, :] = _rows_of(k_parts, 0, t_s - front).astype(BF16)
    vb_ref[front:, 0:HEAD_DIM] = _rows_of(v_parts, 0, t_s - front).astype(BF16)
    vb_ref[:, HEAD_DIM:] = _ones_col(t_s)
    jrow = lax.broadcasted_iota(I32, (1, nk), 1)

    def row0(g):
        return g * tq if isinstance(g, int) else pl.multiple_of(g * tq, tq)

    def scores(g):
        r0 = row0(g)
        return lax.dot_general(q_ref[pl.ds(r0, tq), :], kb_ref[pl.ds(r0, nk), :], (((1,), (1,)), ((), ())),
                               preferred_element_type=F32)

    def consume(s, g, clip_start):
        r0 = row0(g)
        s = s + b_ref[0]
        if clip_start:
            s = s + jnp.where(jrow >= BAND_PAST - (q_off + g * tq), 0.0, NEG_INF)
        m = jnp.max(s, axis=1, keepdims=True)
        p = jnp.exp2(s - m).astype(BF16)
        o = jnp.dot(p, vb_ref[pl.ds(r0, nk), :], preferred_element_type=F32)
        o_ref[pl.ds(r0, tq), :] = (o[:, :HEAD_DIM] / o[:, HEAD_DIM:HEAD_DIM + 1]).astype(o_ref.dtype)

    n_clip = min(n_g, max(0, -(-(BAND_PAST - q_off) // tq)))
    s = scores(0)
    for g in range(n_clip):
        s_next = scores(g + 1) if g + 1 < n_g else None
        consume(s, g, True)
        s = s_next
    if n_g > n_clip:
        def step(g, s):
            s_next = scores(jnp.minimum(g + 1, n_g - 1))
            consume(s, g, False)
            return s_next

        lax.fori_loop(n_clip, n_g, step, s)


def band_attention(q, k, v, bias, n_batch, t_q, t_new, q_off, tq, cache=None):
    n_g = t_q // tq
    nk = bias.shape[2]
    t_s = (n_g - 1) * tq + nk
    rows = pl.BlockSpec((t_q, HEAD_DIM), lambda b, h: (b, h))
    new = pl.BlockSpec((t_new, HEAD_DIM), lambda b, h: (b, h))
    in_specs = [rows, new, new]
    args = [q, k, v]
    if cache is not None:
        assert cache[0].shape[1] == BAND_PAST and BAND_PAST + t_new <= t_s and n_g == 1
        old = pl.BlockSpec((BAND_PAST, HEAD_DIM), lambda b, h: (b, h))
        in_specs += [old, old]
        args += [x.reshape(n_batch * BAND_PAST, H_BAND * HEAD_DIM) for x in cache]
    else:
        assert BAND_PAST + t_new == t_s
    in_specs.append(pl.BlockSpec((1, tq, nk), lambda b, h: (h, 0, 0)))
    return pl.pallas_call(
        functools.partial(_band_kernel, tq=tq, nk=nk, n_g=n_g, q_off=q_off, has_cache=cache is not None),
        grid=(n_batch, H_BAND),
        in_specs=in_specs,
        out_specs=rows,
        out_shape=jax.ShapeDtypeStruct((n_batch * t_q, H_BAND * HEAD_DIM), BF16),
        scratch_shapes=[pltpu.VMEM((t_s, HEAD_DIM), BF16), pltpu.VMEM((t_s, 2 * HEAD_DIM), BF16)],
        compiler_params=_params(("parallel", "parallel")),
        name="band_attention",
    )(*args, bias)


def _even_layer(h, n_batch, t, caches, w, sample):
    (g_mix, w_main, w_f, b_f, lam_vecs, subln_g, w_out_a, w_out_b, g_ffn, wg, wu, wd) = w
    hw = H_FOX * HEAD_DIM
    tm = 512 if not sample else h.shape[0]
    qa, ka, va, qb, kb, vb, logf = norm_proj(
        h, g_mix, w_main, [BF16, F32, F32, BF16, F32, F32], hw, [Q_SCALE, 1.0, 1.0, Q_SCALE, 1.0, 1.0],
        wf=w_f, bf=b_f, tm=tm)
    logf_t = jnp.swapaxes(logf.reshape(n_batch, t, H_FOX), 1, 2)
    if not sample:
        c = cumsum_time(logf_t, 512)
        oa = fox_attention(qa, ka, va, c, n_batch, t, t, 0, 512, 512)
        ob = diff_attention(qb, kb, vb, lam_vecs, subln_g, n_batch, t, t, t, 0, 512, 512)
    else:
        cfk, cfv, cfl, cdk, cdv = caches
        past = cfk.shape[1]
        t_k = past + t
        t_pad = -(-t_k // 384) * 384
        lf = jnp.concatenate([jnp.swapaxes(cfl, 1, 2), logf_t], axis=2)
        c = cumsum_time(jnp.pad(lf, ((0, 0), (0, 0), (0, t_pad - t_k))), 384)
        oa = fox_attention(qa, ka, va, c, n_batch, t, t, past, t, 384, cache=(cfk, cfv))
        ob = diff_attention(qb, kb, vb, lam_vecs, subln_g, n_batch, t, t, t_pad, past, t, 384, cache=(cdk, cdv))
    h = proj_residual([oa, ob], [w_out_a, w_out_b], h, tm=tm)
    h = ffn_dense(h, g_ffn, wg, wu, wd, tm=tm)
    return h, (ka, va, logf, kb, vb)


def _odd_attention(h, n_batch, t, caches, w, sample):
    (g_mix, w_in, table, w_out) = w
    d = D_MODEL
    tm = 512 if not sample else h.shape[0]
    q, k, v = norm_proj(h, g_mix, w_in, [BF16, F32, F32], d, [Q_SCALE, 1.0, 1.0], tm=tm)
    if not sample:
        tq = 2 * CHUNK
        bias = band_bias(table, tq, BAND_PAST + tq, BAND_PAST + tq)
        o = band_attention(q, k, v, bias, n_batch, t, t, 0, tq)
    else:
        cbk, cbv, past_len = caches
        pc = cbk.shape[1]
        assert pc == BAND_PAST and t <= CHUNK and past_len % CHUNK == 0
        nk = -(-(pc + t) // LANES) * LANES
        bias = band_bias(table, t, nk, pc + t)
        o = band_attention(q, k, v, bias, n_batch, t, t, past_len, t, cache=(cbk, cbv))
    h = proj_residual([o], [w_out], h, tm=tm)
    return h, (k, v)


def kernel(x_prompt, x_sample, cache_fox_k, cache_fox_v, cache_fox_logf, cache_diff_k, cache_diff_v, cache_band_k, cache_band_v, norm_mix_even, w_in_even, b_forget, lam_q1, lam_k1, lam_q2, lam_k2, subln_g, w_out_even, norm_ffn_even, w_gate, w_up, w_down, norm_mix_odd, w_in_odd, rel_bias, w_out_odd, norm_ffn_odd, w_router, w_gate_e, w_up_e, w_down_e, norm_final):
    bp, tp, d = x_prompt.shape
    bs, ts, _ = x_sample.shape
    past = cache_fox_k.shape[2]
    hw = H_FOX * HEAD_DIM
    fox_w = 3 * hw + H_FOX

    w_in0 = w_in_even[0]
    w_main = jnp.concatenate([w_in0[:, :3 * hw], w_in0[:, fox_w:]], axis=1).astype(BF16)
    w_f = jnp.pad(w_in0[:, 3 * hw:fox_w], ((0, 0), (0, LANES - H_FOX))).astype(BF16)
    b_f = jnp.pad(b_forget[0], (0, LANES - H_FOX)).reshape(1, LANES)
    lam_vecs = (lam_q1[0], lam_k1[0], lam_q2[0], lam_k2[0])
    w_out0 = w_out_even[0].astype(BF16)
    even_w = (norm_mix_even[0], w_main, w_f, b_f, lam_vecs, subln_g[0], w_out0[:hw], w_out0[hw:],
              norm_ffn_even[0], w_gate[0].astype(BF16), w_up[0].astype(BF16), w_down[0].astype(BF16))
    table = jnp.pad(rel_bias[0], ((0, 0), (0, REL_PAD - rel_bias.shape[2])))
    odd_w = (norm_mix_odd[0], w_in_odd[0].astype(BF16), table, w_out_odd[0].astype(BF16))
    wr = jnp.pad(w_router[0], ((0, 0), (0, LANES - N_EXPERTS)))

    mp, ms = bp * tp, bs * ts
    h_p = x_prompt.reshape(mp, d)
    h_s = x_sample.reshape(ms, d)

    h_p, (fk_p, fv_p, fl_p, dk_p, dv_p) = _even_layer(h_p, bp, tp, None, even_w, False)
    caches_even = (cache_fox_k[0], cache_fox_v[0], cache_fox_logf[0], cache_diff_k[0], cache_diff_v[0])
    h_s, (fk_s, fv_s, fl_s, dk_s, dv_s) = _even_layer(h_s, bs, ts, caches_even, even_w, True)

    h_p, (bk_p, bv_p) = _odd_attention(h_p, bp, tp, None, odd_w, False)
    h_s, (bk_s, bv_s) = _odd_attention(h_s, bs, ts, (cache_band_k[0], cache_band_v[0], past), odd_w, True)

    m_all = mp + ms
    assert mp % 512 == 0 and mp % ms == 0
    zero_counts = jnp.zeros((1, LANES), F32)
    idx_p, gate_p, cnt_p, u_all = router(h_p, norm_ffn_odd[0], wr, zero_counts, 512, m_all, 0)
    idx_s, gate_s, cnt, u_all = router(h_s, norm_ffn_odd[0], wr, cnt_p, ms, m_all, mp, u_dst=u_all)
    idx = jnp.concatenate([idx_p, idx_s], axis=0)
    counts = cnt[0, :N_EXPERTS].astype(I32)
    n_tiles_e = (counts + MOE_TILE - 1) // MOE_TILE
    tile_end = jnp.cumsum(n_tiles_e)
    tile_start = tile_end - n_tiles_e
    row_off = tile_start * MOE_TILE
    n_used = tile_end[-1:]
    nt = (2 * m_all) // MOE_TILE + N_EXPERTS
    tile_ids = jnp.arange(nt, dtype=I32)
    last = jnp.maximum(n_used[0] - 1, 0)
    tile_expert = jnp.sum((jnp.minimum(tile_ids, last)[:, None] >= tile_end[None, :]).astype(I32), axis=1)
    tile_expert = jnp.minimum(tile_expert, N_EXPERTS - 1)
    tile_rows = jnp.clip(counts[tile_expert] - (tile_ids - tile_start[tile_expert]) * MOE_TILE, 0, MOE_TILE)
    pos1 = row_off[idx[:, 0]] + idx[:, 2]
    pos2 = row_off[idx[:, 1]] + idx[:, 3]
    tokens = jnp.arange(m_all, dtype=I32)
    row_token = jnp.zeros((nt * MOE_TILE,), I32).at[pos1].set(tokens).at[pos2].set(tokens)
    xs = jnp.take(u_all, row_token, axis=0, mode="clip")
    y = ffn_moe(tile_expert, n_used.astype(I32), tile_rows.astype(I32), xs, w_gate_e[0], w_up_e[0], w_down_e[0])
    y1 = jnp.take(y, pos1, axis=0, mode="clip")
    y2 = jnp.take(y, pos2, axis=0, mode="clip")
    y_p = combine_norm(h_p, y1, y2, gate_p, norm_final, 512, 0)
    y_s = combine_norm(h_s, y1, y2, gate_s, norm_final, ms, mp)

    keep = min(BAND_PAST, tp)
    bk_p = bk_p.reshape(bp, tp, H_BAND, HEAD_DIM)[:, tp - keep:][None]
    bv_p = bv_p.reshape(bp, tp, H_BAND, HEAD_DIM)[:, tp - keep:][None]
    return (
        y_p.reshape(bp, tp, d), y_s.reshape(bs, ts, d),
        fk_p.reshape(1, bp, tp, H_FOX, HEAD_DIM), fk_s.reshape(1, bs, ts, H_FOX, HEAD_DIM),
        fv_p.reshape(1, bp, tp, H_FOX, HEAD_DIM), fv_s.reshape(1, bs, ts, H_FOX, HEAD_DIM),
        fl_p.reshape(1, bp, tp, H_FOX), fl_s.reshape(1, bs, ts, H_FOX),
        dk_p.reshape(1, bp, tp, H_DIFF, 2, HEAD_DIM), dk_s.reshape(1, bs, ts, H_DIFF, 2, HEAD_DIM),
        dv_p.reshape(1, bp, tp, H_DIFF, 2 * HEAD_DIM), dv_s.reshape(1, bs, ts, H_DIFF, 2 * HEAD_DIM),
        bk_p, bk_s.reshape(1, bs, ts, H_BAND, HEAD_DIM),
        bv_p, bv_s.reshape(1, bs, ts, H_BAND, HEAD_DIM),
    )
```

```python
import functools
import math

import jax
import jax.numpy as jnp
from jax import lax
from jax.experimental import pallas as pl
from jax.experimental.pallas import tpu as pltpu

F32 = jnp.float32
BF16 = jnp.bfloat16
I32 = jnp.int32

D_MODEL = 2048
HEAD_DIM = 128
CHUNK = 64
H_FOX = 8
H_DIFF = 4
H_BAND = 16
N_PREV_CHUNKS = 8
BAND_PAST = N_PREV_CHUNKS * CHUNK
REL_CLIP = 128
D_FF = 5632
N_EXPERTS = 8
EPS = 1e-6
SCALE = HEAD_DIM ** -0.5
LAM_INIT_L0 = 0.8 - 0.6 * math.exp(-0.3 * 0)
NEG_INF = float("-inf")
LOG2E = 1.4426950408889634
Q_SCALE = SCALE * LOG2E

LANES = 128
VMEM_LIMIT = 56 * 1024 * 1024
MOE_TILE = 1024


def _params(sem):
    return pltpu.CompilerParams(dimension_semantics=sem, vmem_limit_bytes=VMEM_LIMIT)


def _rms(x, g):
    return (x * lax.rsqrt(jnp.mean(x * x, axis=-1, keepdims=True) + EPS)) * g


def _log_sigmoid(x):
    return jnp.minimum(x, 0.0) - jnp.log1p(jnp.exp(-jnp.abs(x)))


def _norm_proj_kernel(*refs, has_forget, scale):
    if has_forget:
        x_ref, g_ref, w_ref, wf_ref, bf_ref, o_ref, xn_ref, logf_ref = refs
    else:
        x_ref, g_ref, w_ref, o_ref, xn_ref = refs

    @pl.when(pl.program_id(1) == 0)
    def _():
        xn = _rms(x_ref[...], g_ref[...]).astype(BF16)
        xn_ref[...] = xn
        if has_forget:
            fa = jnp.dot(xn, wf_ref[...], preferred_element_type=F32)
            logf_ref[...] = _log_sigmoid(fa + bf_ref[...])[:, :H_FOX]

    z = jnp.dot(xn_ref[...], w_ref[...], preferred_element_type=F32)
    o_ref[...] = (z * scale).astype(o_ref.dtype)


def norm_proj(x, g, w, group_width, scale, wf=None, bf=None, tm=512, tn=512):
    m, d = x.shape
    n = w.shape[1]
    assert n % group_width == 0 and group_width % tn == 0 and m % tm == 0
    nb = group_width // tn
    has_forget = wf is not None
    in_specs = [
        pl.BlockSpec((tm, d), lambda i, j: (i, 0)),
        pl.BlockSpec((1, d), lambda i, j: (0, 0)),
        pl.BlockSpec((d, tn), lambda i, j: (0, j)),
    ]
    args = [x, g.reshape(1, d), w]
    out_specs = [pl.BlockSpec((None, tm, tn), lambda i, j: (j // nb, i, j % nb)),
                 pl.BlockSpec((tm, d), lambda i, j: (i, 0))]
    out_shape = [jax.ShapeDtypeStruct((n // group_width, m, group_width), BF16),
                 jax.ShapeDtypeStruct((m, d), BF16)]
    if has_forget:
        in_specs += [pl.BlockSpec((d, LANES), lambda i, j: (0, 0)),
                     pl.BlockSpec((1, LANES), lambda i, j: (0, 0))]
        args += [wf, bf]
        out_specs.append(pl.BlockSpec((tm, H_FOX), lambda i, j: (i, 0)))
        out_shape.append(jax.ShapeDtypeStruct((m, H_FOX), F32))
    return pl.pallas_call(
        functools.partial(_norm_proj_kernel, has_forget=has_forget, scale=float(scale)),
        grid=(m // tm, n // tn),
        in_specs=in_specs,
        out_specs=out_specs,
        out_shape=out_shape,
        compiler_params=_params(("parallel", "arbitrary")),
        name="norm_proj",
    )(*args)


def _proj_kernel(x_ref, w_ref, o_ref):
    o_ref[...] = jnp.dot(x_ref[...], w_ref[...], preferred_element_type=F32)


def proj(x, w, group_width, tm=512, tn=512):
    m, d = x.shape
    n = w.shape[1]
    assert n % group_width == 0 and group_width % tn == 0 and m % tm == 0
    nb = group_width // tn
    return pl.pallas_call(
        _proj_kernel,
        grid=(m // tm, n // tn),
        in_specs=[pl.BlockSpec((tm, d), lambda i, j: (i, 0)), pl.BlockSpec((d, tn), lambda i, j: (0, j))],
        out_specs=pl.BlockSpec((None, tm, tn), lambda i, j: (j // nb, i, j % nb)),
        out_shape=jax.ShapeDtypeStruct((n // group_width, m, group_width), F32),
        compiler_params=_params(("parallel", "arbitrary")),
        name="proj",
    )(x, w)


def _proj_res_kernel(*refs, n_in):
    xs = refs[:n_in]
    ws = refs[n_in:2 * n_in]
    h_ref = refs[2 * n_in]
    o_ref = refs[2 * n_in + 1]
    acc = h_ref[...]
    for x_ref, w_ref in zip(xs, ws):
        acc = acc + jnp.dot(x_ref[...], w_ref[...], preferred_element_type=F32)
    o_ref[...] = acc


def proj_residual(xs, ws, h, tm=512, tn=2048):
    m, n = h.shape
    assert m % tm == 0 and n % tn == 0
    n_in = len(xs)
    in_specs = [pl.BlockSpec((tm, x.shape[1]), lambda i, j: (i, 0)) for x in xs]
    in_specs += [pl.BlockSpec((w.shape[0], tn), lambda i, j: (0, j)) for w in ws]
    in_specs.append(pl.BlockSpec((tm, tn), lambda i, j: (i, j)))
    return pl.pallas_call(
        functools.partial(_proj_res_kernel, n_in=n_in),
        grid=(m // tm, n // tn),
        in_specs=in_specs,
        out_specs=pl.BlockSpec((tm, tn), lambda i, j: (i, j)),
        out_shape=jax.ShapeDtypeStruct((m, n), F32),
        compiler_params=_params(("parallel", "arbitrary")),
        name="proj_residual",
    )(*xs, *ws, h)


def _swiglu_step(x, wg_ref, wu_ref, wd_ref):
    a = jnp.dot(x, wg_ref[...], preferred_element_type=F32)
    b = jnp.dot(x, wu_ref[...], preferred_element_type=F32)
    mid = (a * jax.nn.sigmoid(a)) * b
    return jnp.dot(mid.astype(BF16), wd_ref[...], preferred_element_type=F32)


def _ffn_dense_kernel(h_ref, g_ref, wg_ref, wu_ref, wd_ref, o_ref, xn_ref, acc_ref):
    j = pl.program_id(1)

    @pl.when(j == 0)
    def _():
        h = h_ref[...]
        xn_ref[...] = _rms(h, g_ref[...]).astype(BF16)
        acc_ref[...] = h

    acc_ref[...] += _swiglu_step(xn_ref[...], wg_ref, wu_ref, wd_ref)

    @pl.when(j == pl.num_programs(1) - 1)
    def _():
        o_ref[...] = acc_ref[...]


def ffn_dense(h, g, wg, wu, wd, tm=512, tf=512):
    m, d = h.shape
    f = wg.shape[1]
    assert m % tm == 0 and f % tf == 0
    return pl.pallas_call(
        _ffn_dense_kernel,
        grid=(m // tm, f // tf),
        in_specs=[
            pl.BlockSpec((tm, d), lambda i, j: (i, 0)),
            pl.BlockSpec((1, d), lambda i, j: (0, 0)),
            pl.BlockSpec((d, tf), lambda i, j: (0, j)),
            pl.BlockSpec((d, tf), lambda i, j: (0, j)),
            pl.BlockSpec((tf, d), lambda i, j: (j, 0)),
        ],
        out_specs=pl.BlockSpec((tm, d), lambda i, j: (i, 0)),
        out_shape=jax.ShapeDtypeStruct((m, d), F32),
        scratch_shapes=[pltpu.VMEM((tm, d), BF16), pltpu.VMEM((tm, d), F32)],
        compiler_params=_params(("parallel", "arbitrary")),
        name="ffn_dense",
    )(h, g.reshape(1, d), wg, wu, wd)


def _ffn_moe_kernel(te_ref, nu_ref, tr_ref, x_ref, wg_ref, wu_ref, wd_ref, o_ref, mid_ref, *, nj_a):
    t = pl.program_id(0)
    j = pl.program_id(1)
    half = MOE_TILE // 2

    @pl.when(t < nu_ref[0])
    def _():
        @pl.when(j < nj_a)
        def _():
            wg = wg_ref[...].astype(BF16)
            wu = wu_ref[...].astype(BF16)
            for s in range(2):
                @pl.when(s * half < tr_ref[t])
                def _(s=s):
                    x = x_ref[s * half:(s + 1) * half, :]
                    a = jnp.dot(x, wg, preferred_element_type=F32)
                    b = jnp.dot(x, wu, preferred_element_type=F32)
                    mid_ref[s, j] = ((a * jax.nn.sigmoid(a)) * b).astype(BF16)

        @pl.when(j >= nj_a)
        def _():
            wd = wd_ref[...].astype(BF16)
            for s in range(2):
                rows = slice(s * half, (s + 1) * half)

                @pl.when(s * half < tr_ref[t])
                def _(s=s, rows=rows):
                    mid = jnp.concatenate([mid_ref[s, jm] for jm in range(nj_a)], axis=1)
                    o_ref[rows, :] = jnp.dot(mid, wd, preferred_element_type=F32)

                @pl.when(s * half >= tr_ref[t])
                def _(rows=rows):
                    o_ref[rows, :] = jnp.zeros((half, o_ref.shape[1]), F32)


def ffn_moe(tile_expert, n_used, tile_rows, xs, wg, wu, wd, tf=256, tn=256):
    p, d = xs.shape
    f = wg.shape[2]
    tm = MOE_TILE
    nt, nj_a, nj_b = p // tm, f // tf, d // tn

    def used(t, nu):
        return t < nu[0]

    def row(t, nu):
        return jnp.maximum(jnp.minimum(t, nu[0] - 1), 0)

    def col_a(t, j, nu):
        return jnp.where(used(t, nu), jnp.minimum(j, nj_a - 1), nj_a - 1)

    def col_b(t, j, nu):
        return jnp.where(used(t, nu), jnp.clip(j - nj_a, 0, nj_b - 1), nj_b - 1)

    grid_spec = pltpu.PrefetchScalarGridSpec(
        num_scalar_prefetch=3,
        grid=(nt, nj_a + nj_b),
        in_specs=[
            pl.BlockSpec((tm, d), lambda t, j, te, nu, tr: (row(t, nu), 0)),
            pl.BlockSpec((None, d, tf), lambda t, j, te, nu, tr: (te[t], 0, col_a(t, j, nu))),
            pl.BlockSpec((None, d, tf), lambda t, j, te, nu, tr: (te[t], 0, col_a(t, j, nu))),
            pl.BlockSpec((None, f, tn), lambda t, j, te, nu, tr: (te[t], 0, col_b(t, j, nu))),
        ],
        out_specs=pl.BlockSpec((tm, tn), lambda t, j, te, nu, tr: (row(t, nu), col_b(t, j, nu))),
        scratch_shapes=[pltpu.VMEM((2, nj_a, tm // 2, tf), BF16)],
    )
    return pl.pallas_call(
        functools.partial(_ffn_moe_kernel, nj_a=nj_a),
        grid_spec=grid_spec,
        out_shape=jax.ShapeDtypeStruct((p, d), F32),
        compiler_params=_params(("arbitrary", "arbitrary")),
        name="ffn_moe",
    )(tile_expert, n_used, tile_rows, xs, wg, wu, wd)


def _router_kernel(h_ref, g_ref, wr_ref, c0_ref, idx_ref, gate_ref, cnt_ref, u_ref, base_ref, *, tm):
    i = pl.program_id(0)

    @pl.when(i == 0)
    def _():
        base_ref[...] = c0_ref[...]

    u = _rms(h_ref[...], g_ref[...])
    u_ref[...] = u.astype(BF16)
    logits = jnp.dot(u, wr_ref[...], preferred_element_type=F32, precision=lax.Precision.HIGHEST)
    lane = lax.broadcasted_iota(I32, (tm, LANES), 1)
    logits = jnp.where(lane < N_EXPERTS, logits, NEG_INF)
    m1 = jnp.max(logits, axis=1, keepdims=True)
    i1 = jnp.min(jnp.where(logits == m1, lane, LANES), axis=1, keepdims=True)
    rest = jnp.where(lane == i1, NEG_INF, logits)
    m2 = jnp.max(rest, axis=1, keepdims=True)
    i2 = jnp.min(jnp.where(rest == m2, lane, LANES), axis=1, keepdims=True)
    e2 = jnp.exp(m2 - m1)
    g1 = 1.0 / (1.0 + e2)
    g2 = e2 / (1.0 + e2)

    sel1 = lane == i1
    sel2 = lane == i2
    onehot = jnp.where(sel1 | sel2, 1.0, 0.0)
    r = lax.broadcasted_iota(I32, (tm, tm), 0)
    c = lax.broadcasted_iota(I32, (tm, tm), 1)
    strict_lower = jnp.where(c < r, 1.0, 0.0).astype(BF16)
    before = jnp.dot(strict_lower, onehot.astype(BF16), preferred_element_type=F32) + base_ref[...]
    r1 = jnp.sum(jnp.where(sel1, before, 0.0), axis=1, keepdims=True)
    r2 = jnp.sum(jnp.where(sel2, before, 0.0), axis=1, keepdims=True)
    base_ref[...] += jnp.sum(onehot, axis=0, keepdims=True)

    idx = jnp.where(lane == 0, i1, jnp.where(lane == 1, i2,
          jnp.where(lane == 2, r1.astype(I32), r2.astype(I32))))
    idx_ref[...] = idx[:, :8]
    gate_ref[...] = jnp.where(lane == 0, g1, g2)[:, :8]

    @pl.when(i == pl.num_programs(0) - 1)
    def _():
        cnt_ref[...] = base_ref[...]


def _router_kernel_dst(h_ref, g_ref, wr_ref, c0_ref, dst_ref, *rest, tm):
    del dst_ref
    _router_kernel(h_ref, g_ref, wr_ref, c0_ref, *rest, tm=tm)


def router(h, g, wr, count0, tm, u_rows, u_row0, u_dst=None):
    m, d = h.shape
    assert m % tm == 0 and u_row0 % tm == 0
    blk0 = u_row0 // tm
    in_specs = [
        pl.BlockSpec((tm, d), lambda i: (i, 0)),
        pl.BlockSpec((1, d), lambda i: (0, 0)),
        pl.BlockSpec((d, LANES), lambda i: (0, 0)),
        pl.BlockSpec((1, LANES), lambda i: (0, 0)),
    ]
    args = [h, g.reshape(1, d), wr, count0]
    aliases = {}
    if u_dst is not None:
        in_specs.append(pl.BlockSpec(memory_space=pl.ANY))
        args.append(u_dst)
        aliases = {4: 3}
    return pl.pallas_call(
        functools.partial(_router_kernel if u_dst is None else _router_kernel_dst, tm=tm),
        grid=(m // tm,),
        in_specs=in_specs,
        out_specs=[
            pl.BlockSpec((tm, 8), lambda i: (i, 0)),
            pl.BlockSpec((tm, 8), lambda i: (i, 0)),
            pl.BlockSpec((1, LANES), lambda i: (0, 0)),
            pl.BlockSpec((tm, d), lambda i: (i + blk0, 0)),
        ],
        out_shape=[
            jax.ShapeDtypeStruct((m, 8), I32),
            jax.ShapeDtypeStruct((m, 8), F32),
            jax.ShapeDtypeStruct((1, LANES), F32),
            jax.ShapeDtypeStruct((u_rows, d), BF16),
        ],
        scratch_shapes=[pltpu.VMEM((1, LANES), F32)],
        input_output_aliases=aliases,
        compiler_params=_params(("arbitrary",)),
        name="router",
    )(*args)


def _combine_kernel(h_ref, y1_ref, y2_ref, gate_ref, g_ref, o_ref):
    gates = gate_ref[...]
    g1 = gates[:, 0:1]
    g2 = gates[:, 1:2]
    h = h_ref[...] + (g1 * y1_ref[...] + g2 * y2_ref[...])
    o_ref[...] = _rms(h, g_ref[...])


def combine_norm(h, y1, y2, gates, g, tm, y_row0):
    m, d = h.shape
    assert m % tm == 0 and y_row0 % tm == 0
    blk0 = y_row0 // tm
    row = pl.BlockSpec((tm, d), lambda i: (i, 0))
    yrow = pl.BlockSpec((tm, d), lambda i: (i + blk0, 0))
    return pl.pallas_call(
        _combine_kernel,
        grid=(m // tm,),
        in_specs=[row, yrow, yrow, pl.BlockSpec((tm, 8), lambda i: (i, 0)),
                  pl.BlockSpec((1, d), lambda i: (0, 0))],
        out_specs=row,
        out_shape=jax.ShapeDtypeStruct((m, d), F32),
        compiler_params=_params(("parallel",)),
        name="combine_norm",
    )(h, y1, y2, gates, g.reshape(1, d))


def _cumsum_kernel(x_ref, o_ref, *, tb, n_blk):
    r = lax.broadcasted_iota(I32, (tb, tb), 0)
    c = lax.broadcasted_iota(I32, (tb, tb), 1)
    upper = jnp.where(r <= c, 1.0, 0.0)
    carry = jnp.zeros((H_FOX, 1), F32)
    for b in range(n_blk):
        blk = x_ref[0, :, b * tb:(b + 1) * tb]
        cs = jnp.dot(blk, upper, preferred_element_type=F32, precision=lax.Precision.HIGHEST) + carry
        o_ref[0, :, b * tb:(b + 1) * tb] = cs
        carry = cs[:, tb - 1:tb]


def cumsum_time(x, tb):
    b, h, t = x.shape
    assert t % tb == 0
    spec = pl.BlockSpec((1, h, t), lambda i: (i, 0, 0))
    return pl.pallas_call(
        functools.partial(_cumsum_kernel, tb=tb, n_blk=t // tb),
        grid=(b,),
        in_specs=[spec],
        out_specs=spec,
        out_shape=jax.ShapeDtypeStruct((b, h, t), F32),
        compiler_params=_params(("parallel",)),
        name="cumsum_time",
    )(x)


def _rows_of(parts, r0, n):
    pieces, start = [], 0
    for part in parts:
        lo, hi = max(r0, start), min(r0 + n, start + part.shape[0])
        if lo < hi:
            pieces.append(part[lo - start:hi - start, :].astype(F32))
        start += part.shape[0]
    got = sum(p.shape[0] for p in pieces)
    if got < n:
        pieces.append(jnp.zeros((n - got, parts[0].shape[1]), F32))
    return pieces[0] if len(pieces) == 1 else jnp.concatenate(pieces, axis=0)


def _split3(x):
    hi = x.astype(BF16).astype(F32)
    r = x - hi
    mid = r.astype(BF16).astype(F32)
    return hi, mid, (r - mid).astype(BF16).astype(F32)


def _bias_cols(col):
    hi, mid, lo = _split3(col)
    lane = lax.broadcasted_iota(I32, (col.shape[0], LANES), 1)
    e = jnp.where(lane == 0, hi, jnp.where(lane == 1, mid, jnp.where(lane == 2, lo,
        jnp.where(lane < 6, 1.0, 0.0))))
    return e.astype(BF16)


def _bias_rows(row):
    hi, mid, lo = _split3(row)
    sub = lax.broadcasted_iota(I32, (LANES, row.shape[1]), 0)
    e = jnp.where(sub < 3, 1.0, jnp.where(sub == 3, hi, jnp.where(sub == 4, mid,
        jnp.where(sub == 5, lo, 0.0))))
    return e.astype(BF16)


def _ones_col(n):
    return jnp.where(lax.broadcasted_iota(I32, (n, LANES), 1) == 0, 1.0, 0.0).astype(BF16)


def _fox_kernel(*refs, tq, tk, q_off, n_q, has_cache):
    if has_cache:
        q_ref, k_ref, v_ref, kc_ref, vc_ref, cq_ref, ck_ref, o_ref, kt_ref, vb_ref, m_ref, acc_ref = refs
        k_parts, v_parts = (kc_ref, k_ref), (vc_ref, v_ref)
    else:
        q_ref, k_ref, v_ref, cq_ref, ck_ref, o_ref, kt_ref, vb_ref, m_ref, acc_ref = refs
        k_parts, v_parts = (k_ref,), (v_ref,)
    h = pl.program_id(1)
    qi = pl.program_id(2)

    @pl.when(qi == 0)
    def _():
        for blk in range(kt_ref.shape[0]):
            r0 = blk * tk
            kt_ref[blk, 0:HEAD_DIM, :] = _rows_of(k_parts, r0, tk).T.astype(BF16)
            kt_ref[blk, HEAD_DIM:, :] = _bias_rows(-LOG2E * ck_ref[0, h, blk])
            vb_ref[r0:r0 + tk, 0:HEAD_DIM] = _rows_of(v_parts, r0, tk).astype(BF16)
            vb_ref[r0:r0 + tk, HEAD_DIM:] = _ones_col(tk)

    qpos0 = q_off if n_q == 1 else q_off + qi * tq
    cq_row = LOG2E * cq_ref[0, h, q_off // tq + qi]
    eye = lax.broadcasted_iota(I32, (tq, tq), 0) == lax.broadcasted_iota(I32, (tq, tq), 1)
    cq = jnp.sum(jnp.where(eye, cq_row, 0.0), axis=1, keepdims=True)
    qx = jnp.concatenate([q_ref[...], _bias_cols(cq)], axis=1)
    row = lax.broadcasted_iota(I32, (tq, tk), 0)
    col = lax.broadcasted_iota(I32, (tq, tk), 1)

    def scores(kv):
        return jnp.dot(qx, kt_ref[kv], preferred_element_type=F32)

    def consume(s, kv, masked):
        if masked:
            s = jnp.where(kv * tk + col <= qpos0 + row, s, NEG_INF)
        m = m_ref[...]
        m_new = jnp.maximum(m, jnp.max(s, axis=1, keepdims=True))
        p = jnp.exp2(s - m_new).astype(BF16)
        k0 = pl.multiple_of(kv * tk, tk)
        pv = jnp.dot(p, vb_ref[pl.ds(k0, tk), :], preferred_element_type=F32)
        acc_ref[...] = jnp.exp2(m - m_new) * acc_ref[...] + pv
        m_ref[...] = m_new

    m_ref[...] = jnp.full(m_ref.shape, NEG_INF, F32)
    acc_ref[...] = jnp.zeros(acc_ref.shape, F32)
    n_full = qpos0 // tk

    def step(kv, s):
        s_next = scores(kv + 1)
        consume(s, kv, False)
        return s_next

    s_last = lax.fori_loop(0, n_full, step, scores(0))
    consume(s_last, n_full, True)
    acc = acc_ref[...]
    o_ref[...] = (acc[:, :HEAD_DIM] / acc[:, HEAD_DIM:HEAD_DIM + 1]).astype(o_ref.dtype)


def fox_attention(q, k, v, c, n_batch, t_q, t_new, q_off, tq, tk, cache=None):
    n_q = t_q // tq
    tc = c.shape[2]
    assert t_q % tq == 0 and q_off % tq == 0 and tc % tq == 0 and tc % tk == 0
    assert (q_off % tk) + tq <= tk if n_q == 1 else (tq == tk and q_off % tk == 0)
    c_q = c.reshape(n_batch, H_FOX, tc // tq, 1, tq)
    c_k = c.reshape(n_batch, H_FOX, tc // tk, 1, tk)
    (q, qg), (k, kg), (v, vg) = q, k, v
    in_specs = [pl.BlockSpec((None, tq, HEAD_DIM), lambda b, h, i: (qg, b * n_q + i, h)),
                pl.BlockSpec((None, t_new, HEAD_DIM), lambda b, h, i: (kg, b, h)),
                pl.BlockSpec((None, t_new, HEAD_DIM), lambda b, h, i: (vg, b, h))]
    args = [q, k, v]
    if cache is not None:
        past = cache[0].shape[1]
        assert past + t_new <= tc
        old = pl.BlockSpec((past, HEAD_DIM), lambda b, h, i: (b, h))
        in_specs += [old, old]
        args += [x.reshape(n_batch * past, H_FOX * HEAD_DIM) for x in cache]
    else:
        assert t_new == tc
    in_specs += [pl.BlockSpec((1, H_FOX, tc // tq, 1, tq), lambda b, h, i: (b, 0, 0, 0, 0)),
                 pl.BlockSpec((1, H_FOX, tc // tk, 1, tk), lambda b, h, i: (b, 0, 0, 0, 0))]
    return pl.pallas_call(
        functools.partial(_fox_kernel, tq=tq, tk=tk, q_off=q_off, n_q=n_q, has_cache=cache is not None),
        grid=(n_batch, H_FOX, n_q),
        in_specs=in_specs,
        out_specs=pl.BlockSpec((tq, HEAD_DIM), lambda b, h, i: (b * n_q + i, h)),
        out_shape=jax.ShapeDtypeStruct((n_batch * t_q, H_FOX * HEAD_DIM), BF16),
        scratch_shapes=[pltpu.VMEM((tc // tk, 2 * HEAD_DIM, tk), BF16), pltpu.VMEM((tc, 2 * HEAD_DIM), BF16),
                        pltpu.VMEM((tq, 1), F32), pltpu.VMEM((tq, 2 * HEAD_DIM), F32)],
        compiler_params=_params(("parallel", "parallel", "arbitrary")),
        name="fox_attention",
    )(*args, c_q, c_k)


def _diff_kernel(slope_ref, q_ref, k_ref, v_ref, *refs, tq, tk, q_off, n_q, n_valid, has_cache):
    if has_cache:
        kc0_ref, kc1_ref, vc_ref = refs[:3]
        refs = refs[3:]
    lq1_ref, lk1_ref, lq2_ref, lk2_ref, sg_ref, o_ref, kt_ref, vb_ref, m_ref, l_ref, acc_ref = refs
    h = pl.program_id(1)
    qi = pl.program_id(2)
    dv = 2 * HEAD_DIM
    slope2 = slope_ref[h] * LOG2E

    @pl.when(qi == 0)
    def _():
        v_parts = (vc_ref, v_ref) if has_cache else (v_ref,)
        for blk in range(kt_ref.shape[1]):
            r0 = blk * tk
            kpos_row = (r0 + lax.broadcasted_iota(I32, (1, tk), 1)).astype(F32)
            key_bias = _bias_rows(slope2 * kpos_row)
            for mp in range(2):
                new_rows = k_ref.at[:, mp * HEAD_DIM:(mp + 1) * HEAD_DIM]
                k_parts = ((kc0_ref, kc1_ref)[mp], new_rows) if has_cache else (new_rows,)
                kt_ref[mp, blk, 0:HEAD_DIM, :] = _rows_of(k_parts, r0, tk).T.astype(BF16)
                kt_ref[mp, blk, HEAD_DIM:, :] = key_bias
            vb_ref[r0:r0 + tk, :] = _rows_of(v_parts, r0, tk).astype(BF16)

    lam = (jnp.exp(jnp.sum(lq1_ref[...] * lk1_ref[...], axis=1, keepdims=True))
           - jnp.exp(jnp.sum(lq2_ref[...] * lk2_ref[...], axis=1, keepdims=True)) + LAM_INIT_L0)
    qpos0 = q_off if n_q == 1 else q_off + qi * tq
    qpos_col = (qpos0 + lax.broadcasted_iota(I32, (tq, 1), 0)).astype(F32)
    query_bias = _bias_cols(-slope2 * qpos_col)
    q = q_ref[...]
    qx = [jnp.concatenate([q[:, mp * HEAD_DIM:(mp + 1) * HEAD_DIM], query_bias], axis=1) for mp in range(2)]
    row = lax.broadcasted_iota(I32, (tq, tk), 0)
    col = lax.broadcasted_iota(I32, (tq, tk), 1)

    def scores(kv):
        return tuple(jnp.dot(qx[mp], kt_ref[mp, kv], preferred_element_type=F32) for mp in range(2))

    def consume(ss, kv, masked):
        k0 = pl.multiple_of(kv * tk, tk)
        vblk = vb_ref[pl.ds(k0, tk), :]
        if masked:
            qpos = qpos0 + row
            kpos = kv * tk + col
            visible = ((kpos >> 6) <= (qpos >> 6)) & (kpos < n_valid)
            fix = (2.0 * slope2) * jnp.maximum(kpos - qpos, 0).astype(F32)
        for mp in range(2):
            s = ss[mp]
            if masked:
                s = jnp.where(visible, s - fix, NEG_INF)
            m = m_ref[mp]
            m_new = jnp.maximum(m, jnp.max(s, axis=1, keepdims=True))
            alpha = jnp.exp2(m - m_new)
            p = jnp.exp2(s - m_new)
            l_ref[mp] = alpha * l_ref[mp] + jnp.sum(p, axis=1, keepdims=True)
            acc_ref[mp] = alpha * acc_ref[mp] + jnp.dot(p.astype(BF16), vblk, preferred_element_type=F32)
            m_ref[mp] = m_new

    m_ref[...] = jnp.full(m_ref.shape, NEG_INF, F32)
    l_ref[...] = jnp.zeros(l_ref.shape, F32)
    acc_ref[...] = jnp.zeros(acc_ref.shape, F32)
    n_full = qpos0 // tk

    def step(kv, ss):
        ss_next = scores(kv + 1)
        consume(ss, kv, False)
        return ss_next

    ss_last = lax.fori_loop(0, n_full, step, scores(0))
    consume(ss_last, n_full, True)
    o = acc_ref[0] / l_ref[0] - lam * (acc_ref[1] / l_ref[1])
    o_ref[...] = (_rms(o, sg_ref[...]) * (1.0 - LAM_INIT_L0)).astype(o_ref.dtype)


def diff_attention(q, k, v, lam_vecs, subln_g, n_batch, t_q, t_new, t_s, q_off, tq, tk, cache=None):
    n_q = t_q // tq
    assert t_q % tq == 0 and t_s % tk == 0
    dv = 2 * HEAD_DIM
    slopes = 2.0 ** (-8.0 * jnp.arange(1, H_DIFF + 1, dtype=F32) / H_DIFF)
    vec = pl.BlockSpec((1, HEAD_DIM), lambda b, h, i, s: (0, 0))
    (q, qg), (k, kg), (v, vg) = q, k, v
    in_specs = [pl.BlockSpec((None, tq, dv), lambda b, h, i, s: (qg, b * n_q + i, h)),
                pl.BlockSpec((None, t_new, dv), lambda b, h, i, s: (kg, b, h)),
                pl.BlockSpec((None, t_new, dv), lambda b, h, i, s: (vg, b, h))]
    args = [slopes, q, k, v]
    past = 0
    if cache is not None:
        past = cache[0].shape[1]
        in_specs += [
            pl.BlockSpec((past, HEAD_DIM), lambda b, h, i, s: (b, 2 * h)),
            pl.BlockSpec((past, HEAD_DIM), lambda b, h, i, s: (b, 2 * h + 1)),
            pl.BlockSpec((past, dv), lambda b, h, i, s: (b, h)),
        ]
        ck2d = cache[0].reshape(n_batch * past, H_DIFF * dv)
        args += [ck2d, ck2d, cache[1].reshape(n_batch * past, H_DIFF * dv)]
    n_valid = past + t_new
    assert n_valid <= t_s and tq % CHUNK == 0 or n_q == 1
    if n_q == 1:
        last_visible = min(((q_off + tq - 1) // CHUNK + 1) * CHUNK, n_valid)
        assert (q_off // tk) * tk <= n_valid and last_visible <= (q_off // tk + 1) * tk
    else:
        assert tq == tk and q_off % tk == 0
    in_specs += [vec, vec, vec, vec, pl.BlockSpec((1, dv), lambda b, h, i, s: (0, 0))]
    grid_spec = pltpu.PrefetchScalarGridSpec(
        num_scalar_prefetch=1,
        grid=(n_batch, H_DIFF, n_q),
        in_specs=in_specs,
        out_specs=pl.BlockSpec((tq, dv), lambda b, h, i, s: (b * n_q + i, h)),
        scratch_shapes=[pltpu.VMEM((2, t_s // tk, dv, tk), BF16), pltpu.VMEM((t_s, dv), BF16),
                        pltpu.VMEM((2, tq, 1), F32), pltpu.VMEM((2, tq, 1), F32), pltpu.VMEM((2, tq, dv), F32)],
    )
    return pl.pallas_call(
        functools.partial(_diff_kernel, tq=tq, tk=tk, q_off=q_off, n_q=n_q, n_valid=n_valid,
                          has_cache=cache is not None),
        grid_spec=grid_spec,
        out_shape=jax.ShapeDtypeStruct((n_batch * t_q, H_DIFF * dv), BF16),
        compiler_params=_params(("parallel", "parallel", "arbitrary")),
        name="diff_attention",
    )(*args, *[x.reshape(1, HEAD_DIM) for x in lam_vecs], subln_g.reshape(1, dv))


REL_PAD = 384


def _band_bias_kernel(t_ref, o_ref, *, nk, n_valid):
    blk = pl.program_id(0)
    row = lax.broadcasted_iota(I32, (REL_PAD, nk), 0)
    j = lax.broadcasted_iota(I32, (REL_PAD, nk), 1)
    jc = lax.broadcasted_iota(I32, (1, nk), 1) >> 6
    jv = lax.broadcasted_iota(I32, (1, nk), 1) < n_valid
    for r in range(8):
        i = blk * 8 + r
        idx = jnp.clip(i - j + BAND_PAST, -REL_CLIP, REL_CLIP) + REL_CLIP
        onehot = jnp.where(row == idx, 1.0, 0.0)
        bias = jnp.dot(t_ref[...], onehot, preferred_element_type=F32, precision=lax.Precision.HIGHEST)
        ic = i >> 6
        visible = (jc - N_PREV_CHUNKS <= ic) & (jc >= ic) & jv
        o_ref[:, r, :] = jnp.where(visible, bias * LOG2E, NEG_INF)


def band_bias(table_padded, tq, nk, n_valid):
    assert CHUNK == 64
    return pl.pallas_call(
        functools.partial(_band_bias_kernel, nk=nk, n_valid=n_valid),
        grid=(tq // 8,),
        in_specs=[pl.BlockSpec((H_BAND, REL_PAD), lambda i: (0, 0))],
        out_specs=pl.BlockSpec((H_BAND, 8, nk), lambda i: (0, i, 0)),
        out_shape=jax.ShapeDtypeStruct((H_BAND, tq, nk), F32),
        compiler_params=_params(("parallel",)),
        name="band_bias",
    )(table_padded)


def _band_kernel(*refs, tq, nk, n_g, q_off, has_cache):
    if has_cache:
        q_ref, k_ref, v_ref, kc_ref, vc_ref, b_ref, o_ref, kb_ref, vb_ref = refs
        k_parts, v_parts = (kc_ref, k_ref), (vc_ref, v_ref)
        front = 0
    else:
        q_ref, k_ref, v_ref, b_ref, o_ref, kb_ref, vb_ref = refs
        k_parts, v_parts = (k_ref,), (v_ref,)
        front = kb_ref.shape[0] - k_ref.shape[0]
        kb_ref[0:front, :] = jnp.zeros((front, HEAD_DIM), BF16)
        vb_ref[0:front, 0:HEAD_DIM] = jnp.zeros((front, HEAD_DIM), BF16)
    t_s = kb_ref.shape[0]
    kb_ref[front:, :] = _rows_of(k_parts, 0, t_s - front).astype(BF16)
    vb_ref[front:, 0:HEAD_DIM] = _rows_of(v_parts, 0, t_s - front).astype(BF16)
    vb_ref[:, HEAD_DIM:] = _ones_col(t_s)
    jrow = lax.broadcasted_iota(I32, (1, nk), 1)

    def row0(g):
        return g * tq if isinstance(g, int) else pl.multiple_of(g * tq, tq)

    def scores(g):
        r0 = row0(g)
        return lax.dot_general(q_ref[pl.ds(r0, tq), :], kb_ref[pl.ds(r0, nk), :], (((1,), (1,)), ((), ())),
                               preferred_element_type=F32)

    def consume(s, g, clip_start):
        r0 = row0(g)
        s = s + b_ref[0]
        if clip_start:
            s = s + jnp.where(jrow >= BAND_PAST - (q_off + g * tq), 0.0, NEG_INF)
        m = jnp.max(s, axis=1, keepdims=True)
        p = jnp.exp2(s - m).astype(BF16)
        o = jnp.dot(p, vb_ref[pl.ds(r0, nk), :], preferred_element_type=F32)
        o_ref[pl.ds(r0, tq), :] = (o[:, :HEAD_DIM] / o[:, HEAD_DIM:HEAD_DIM + 1]).astype(o_ref.dtype)

    n_clip = min(n_g, max(0, -(-(BAND_PAST - q_off) // tq)))
    s = scores(0)
    for g in range(n_clip):
        s_next = scores(g + 1) if g + 1 < n_g else None
        consume(s, g, True)
        s = s_next
    if n_g > n_clip:
        def step(g, s):
            s_next = scores(jnp.minimum(g + 1, n_g - 1))
            consume(s, g, False)
            return s_next

        lax.fori_loop(n_clip, n_g, step, s)


def band_attention(q, k, v, bias, n_batch, t_q, t_new, q_off, tq, cache=None):
    n_g = t_q // tq
    nk = bias.shape[2]
    t_s = (n_g - 1) * tq + nk
    rows = pl.BlockSpec((t_q, HEAD_DIM), lambda b, h: (b, h))
    (q, qg), (k, kg), (v, vg) = q, k, v
    in_specs = [pl.BlockSpec((None, t_q, HEAD_DIM), lambda b, h: (qg, b, h)),
                pl.BlockSpec((None, t_new, HEAD_DIM), lambda b, h: (kg, b, h)),
                pl.BlockSpec((None, t_new, HEAD_DIM), lambda b, h: (vg, b, h))]
    args = [q, k, v]
    if cache is not None:
        assert cache[0].shape[1] == BAND_PAST and BAND_PAST + t_new <= t_s and n_g == 1
        old = pl.BlockSpec((BAND_PAST, HEAD_DIM), lambda b, h: (b, h))
        in_specs += [old, old]
        args += [x.reshape(n_batch * BAND_PAST, H_BAND * HEAD_DIM) for x in cache]
    else:
        assert BAND_PAST + t_new == t_s
    in_specs.append(pl.BlockSpec((1, tq, nk), lambda b, h: (h, 0, 0)))
    return pl.pallas_call(
        functools.partial(_band_kernel, tq=tq, nk=nk, n_g=n_g, q_off=q_off, has_cache=cache is not None),
        grid=(n_batch, H_BAND),
        in_specs=in_specs,
        out_specs=rows,
        out_shape=jax.ShapeDtypeStruct((n_batch * t_q, H_BAND * HEAD_DIM), BF16),
        scratch_shapes=[pltpu.VMEM((t_s, HEAD_DIM), BF16), pltpu.VMEM((t_s, 2 * HEAD_DIM), BF16)],
        compiler_params=_params(("parallel", "parallel")),
        name="band_attention",
    )(*args, bias)


def _even_layer(h, n_batch, t, caches, w, sample):
    (g_mix, w_q, w_kv, w_f, b_f, lam_vecs, subln_g, w_out_a, w_out_b, g_ffn, wg, wu, wd) = w
    hw = H_FOX * HEAD_DIM
    tm = 512 if not sample else h.shape[0]
    tm_in = 1024 if not sample else tm
    q, xn, logf = norm_proj(h, g_mix, w_q, hw, Q_SCALE, wf=w_f, bf=b_f, tm=tm_in)
    kv = proj(xn, w_kv, hw, tm=tm_in)
    logf_t = jnp.swapaxes(logf.reshape(n_batch, t, H_FOX), 1, 2)
    if not sample:
        c = cumsum_time(logf_t, 512)
        oa = fox_attention((q, 0), (kv, 0), (kv, 1), c, n_batch, t, t, 0, 512, 512)
        ob = diff_attention((q, 1), (kv, 2), (kv, 3), lam_vecs, subln_g, n_batch, t, t, t, 0, 512, 512)
    else:
        cfk, cfv, cfl, cdk, cdv = caches
        past = cfk.shape[1]
        t_k = past + t
        t_pad = -(-t_k // 384) * 384
        lf = jnp.concatenate([jnp.swapaxes(cfl, 1, 2), logf_t], axis=2)
        c = cumsum_time(jnp.pad(lf, ((0, 0), (0, 0), (0, t_pad - t_k))), 384)
        oa = fox_attention((q, 0), (kv, 0), (kv, 1), c, n_batch, t, t, past, t, 384, cache=(cfk, cfv))
        ob = diff_attention((q, 1), (kv, 2), (kv, 3), lam_vecs, subln_g, n_batch, t, t, t_pad, past, t, 384,
                            cache=(cdk, cdv))
    h = proj_residual([oa, ob], [w_out_a, w_out_b], h, tm=tm)
    h = ffn_dense(h, g_ffn, wg, wu, wd, tm=tm)
    return h, (kv[0], kv[1], logf, kv[2], kv[3])


def _odd_attention(h, n_batch, t, caches, w, sample):
    (g_mix, w_q, w_kv, table, w_out) = w
    d = D_MODEL
    tm = 512 if not sample else h.shape[0]
    tm_in = 1024 if not sample else tm
    q, xn = norm_proj(h, g_mix, w_q, d, Q_SCALE, tm=tm_in)
    kv = proj(xn, w_kv, d, tm=tm_in)
    if not sample:
        tq = 2 * CHUNK
        bias = band_bias(table, tq, BAND_PAST + tq, BAND_PAST + tq)
        o = band_attention((q, 0), (kv, 0), (kv, 1), bias, n_batch, t, t, 0, tq)
    else:
        cbk, cbv, past_len = caches
        pc = cbk.shape[1]
        assert pc == BAND_PAST and t <= CHUNK and past_len % CHUNK == 0
        nk = -(-(pc + t) // LANES) * LANES
        bias = band_bias(table, t, nk, pc + t)
        o = band_attention((q, 0), (kv, 0), (kv, 1), bias, n_batch, t, t, past_len, t, cache=(cbk, cbv))
    h = proj_residual([o], [w_out], h, tm=tm)
    return h, (kv[0], kv[1])


def kernel(x_prompt, x_sample, cache_fox_k, cache_fox_v, cache_fox_logf, cache_diff_k, cache_diff_v, cache_band_k, cache_band_v, norm_mix_even, w_in_even, b_forget, lam_q1, lam_k1, lam_q2, lam_k2, subln_g, w_out_even, norm_ffn_even, w_gate, w_up, w_down, norm_mix_odd, w_in_odd, rel_bias, w_out_odd, norm_ffn_odd, w_router, w_gate_e, w_up_e, w_down_e, norm_final):
    bp, tp, d = x_prompt.shape
    bs, ts, _ = x_sample.shape
    past = cache_fox_k.shape[2]
    hw = H_FOX * HEAD_DIM
    fox_w = 3 * hw + H_FOX

    w_in0 = w_in_even[0]
    w_q0 = jnp.concatenate([w_in0[:, :hw], w_in0[:, fox_w:fox_w + hw]], axis=1).astype(BF16)
    w_kv0 = jnp.concatenate([w_in0[:, hw:3 * hw], w_in0[:, fox_w + hw:]], axis=1).astype(BF16)
    w_f = jnp.pad(w_in0[:, 3 * hw:fox_w], ((0, 0), (0, LANES - H_FOX))).astype(BF16)
    b_f = jnp.pad(b_forget[0], (0, LANES - H_FOX)).reshape(1, LANES)
    lam_vecs = (lam_q1[0], lam_k1[0], lam_q2[0], lam_k2[0])
    w_out0 = w_out_even[0].astype(BF16)
    even_w = (norm_mix_even[0], w_q0, w_kv0, w_f, b_f, lam_vecs, subln_g[0], w_out0[:hw], w_out0[hw:],
              norm_ffn_even[0], w_gate[0].astype(BF16), w_up[0].astype(BF16), w_down[0].astype(BF16))
    table = jnp.pad(rel_bias[0], ((0, 0), (0, REL_PAD - rel_bias.shape[2])))
    w_in1 = w_in_odd[0].astype(BF16)
    odd_w = (norm_mix_odd[0], w_in1[:, :d], w_in1[:, d:], table, w_out_odd[0].astype(BF16))
    wr = jnp.pad(w_router[0], ((0, 0), (0, LANES - N_EXPERTS)))

    mp, ms = bp * tp, bs * ts
    h_p = x_prompt.reshape(mp, d)
    h_s = x_sample.reshape(ms, d)

    h_p, (fk_p, fv_p, fl_p, dk_p, dv_p) = _even_layer(h_p, bp, tp, None, even_w, False)
    caches_even = (cache_fox_k[0], cache_fox_v[0], cache_fox_logf[0], cache_diff_k[0], cache_diff_v[0])
    h_s, (fk_s, fv_s, fl_s, dk_s, dv_s) = _even_layer(h_s, bs, ts, caches_even, even_w, True)

    h_p, (bk_p, bv_p) = _odd_attention(h_p, bp, tp, None, odd_w, False)
    h_s, (bk_s, bv_s) = _odd_attention(h_s, bs, ts, (cache_band_k[0], cache_band_v[0], past), odd_w, True)

    m_all = mp + ms
    assert mp % 512 == 0 and mp % ms == 0
    zero_counts = jnp.zeros((1, LANES), F32)
    idx_p, gate_p, cnt_p, u_all = router(h_p, norm_ffn_odd[0], wr, zero_counts, 512, m_all, 0)
    idx_s, gate_s, cnt, u_all = router(h_s, norm_ffn_odd[0], wr, cnt_p, ms, m_all, mp, u_dst=u_all)
    idx = jnp.concatenate([idx_p, idx_s], axis=0)
    counts = cnt[0, :N_EXPERTS].astype(I32)
    n_tiles_e = (counts + MOE_TILE - 1) // MOE_TILE
    tile_end = jnp.cumsum(n_tiles_e)
    tile_start = tile_end - n_tiles_e
    row_off = tile_start * MOE_TILE
    n_used = tile_end[-1:]
    nt = (2 * m_all) // MOE_TILE + N_EXPERTS
    tile_ids = jnp.arange(nt, dtype=I32)
    last = jnp.maximum(n_used[0] - 1, 0)
    tile_expert = jnp.sum((jnp.minimum(tile_ids, last)[:, None] >= tile_end[None, :]).astype(I32), axis=1)
    tile_expert = jnp.minimum(tile_expert, N_EXPERTS - 1)
    tile_rows = jnp.clip(counts[tile_expert] - (tile_ids - tile_start[tile_expert]) * MOE_TILE, 0, MOE_TILE)
    pos1 = row_off[idx[:, 0]] + idx[:, 2]
    pos2 = row_off[idx[:, 1]] + idx[:, 3]
    tokens = jnp.arange(m_all, dtype=I32)
    row_token = jnp.zeros((nt * MOE_TILE,), I32).at[pos1].set(tokens).at[pos2].set(tokens)
    xs = jnp.take(u_all, row_token, axis=0, mode="clip")
    y = ffn_moe(tile_expert, n_used.astype(I32), tile_rows.astype(I32), xs, w_gate_e[0], w_up_e[0], w_down_e[0])
    y1 = jnp.take(y, pos1, axis=0, mode="clip")
    y2 = jnp.take(y, pos2, axis=0, mode="clip")
    y_p = combine_norm(h_p, y1, y2, gate_p, norm_final, 512, 0)
    y_s = combine_norm(h_s, y1, y2, gate_s, norm_final, ms, mp)

    keep = min(BAND_PAST, tp)
    bk_p = bk_p.reshape(bp, tp, H_BAND, HEAD_DIM)[:, tp - keep:][None]
    bv_p = bv_p.reshape(bp, tp, H_BAND, HEAD_DIM)[:, tp - keep:][None]
    return (
        y_p.reshape(bp, tp, d), y_s.reshape(bs, ts, d),
        fk_p.reshape(1, bp, tp, H_FOX, HEAD_DIM), fk_s.reshape(1, bs, ts, H_FOX, HEAD_DIM),
        fv_p.reshape(1, bp, tp, H_FOX, HEAD_DIM), fv_s.reshape(1, bs, ts, H_FOX, HEAD_DIM),
        fl_p.reshape(1, bp, tp, H_FOX), fl_s.reshape(1, bs, ts, H_FOX),
        dk_p.reshape(1, bp, tp, H_DIFF, 2, HEAD_DIM), dk_s.reshape(1, bs, ts, H_DIFF, 2, HEAD_DIM),
        dv_p.reshape(1, bp, tp, H_DIFF, 2 * HEAD_DIM), dv_s.reshape(1, bs, ts, H_DIFF, 2 * HEAD_DIM),
        bk_p, bk_s.reshape(1, bs, ts, H_BAND, HEAD_DIM),
        bv_p, bv_s.reshape(1, bs, ts, H_BAND, HEAD_DIM),
    )
```

```python
import functools
import math

import jax
import jax.numpy as jnp
from jax import lax
from jax.experimental import pallas as pl
from jax.experimental.pallas import tpu as pltpu

F32 = jnp.float32
BF16 = jnp.bfloat16
I32 = jnp.int32

D_MODEL = 2048
HEAD_DIM = 128
CHUNK = 64
H_FOX = 8
H_DIFF = 4
H_BAND = 16
N_PREV_CHUNKS = 8
BAND_PAST = N_PREV_CHUNKS * CHUNK
REL_CLIP = 128
D_FF = 5632
N_EXPERTS = 8
EPS = 1e-6
SCALE = HEAD_DIM ** -0.5
LAM_INIT_L0 = 0.8 - 0.6 * math.exp(-0.3 * 0)
NEG_INF = float("-inf")
LOG2E = 1.4426950408889634
Q_SCALE = SCALE * LOG2E

LANES = 128
VMEM_LIMIT = 56 * 1024 * 1024
MOE_TILE = 1024


def _params(sem):
    return pltpu.CompilerParams(dimension_semantics=sem, vmem_limit_bytes=VMEM_LIMIT)


def _rms(x, g):
    return (x * lax.rsqrt(jnp.mean(x * x, axis=-1, keepdims=True) + EPS)) * g


def _log_sigmoid(x):
    return jnp.minimum(x, 0.0) - jnp.log1p(jnp.exp(-jnp.abs(x)))


def _norm_proj_kernel(*refs, has_forget, scale):
    if has_forget:
        x_ref, g_ref, w_ref, wf_ref, bf_ref, o_ref, xn_ref, logf_ref = refs
    else:
        x_ref, g_ref, w_ref, o_ref, xn_ref = refs

    @pl.when(pl.program_id(1) == 0)
    def _():
        xn = _rms(x_ref[...], g_ref[...]).astype(BF16)
        xn_ref[...] = xn
        if has_forget:
            fa = jnp.dot(xn, wf_ref[...], preferred_element_type=F32)
            logf_ref[...] = _log_sigmoid(fa + bf_ref[...])[:, :H_FOX]

    z = jnp.dot(xn_ref[...], w_ref[...], preferred_element_type=F32)
    o_ref[...] = (z * scale).astype(o_ref.dtype)


def norm_proj(x, g, w, group_width, scale, wf=None, bf=None, tm=512, tn=512):
    m, d = x.shape
    n = w.shape[1]
    assert n % group_width == 0 and group_width % tn == 0 and m % tm == 0
    nb = group_width // tn
    has_forget = wf is not None
    in_specs = [
        pl.BlockSpec((tm, d), lambda i, j: (i, 0)),
        pl.BlockSpec((1, d), lambda i, j: (0, 0)),
        pl.BlockSpec((d, tn), lambda i, j: (0, j)),
    ]
    args = [x, g.reshape(1, d), w]
    out_specs = [pl.BlockSpec((None, tm, tn), lambda i, j: (j // nb, i, j % nb)),
                 pl.BlockSpec((tm, d), lambda i, j: (i, 0))]
    out_shape = [jax.ShapeDtypeStruct((n // group_width, m, group_width), BF16),
                 jax.ShapeDtypeStruct((m, d), BF16)]
    if has_forget:
        in_specs += [pl.BlockSpec((d, LANES), lambda i, j: (0, 0)),
                     pl.BlockSpec((1, LANES), lambda i, j: (0, 0))]
        args += [wf, bf]
        out_specs.append(pl.BlockSpec((tm, H_FOX), lambda i, j: (i, 0)))
        out_shape.append(jax.ShapeDtypeStruct((m, H_FOX), F32))
    return pl.pallas_call(
        functools.partial(_norm_proj_kernel, has_forget=has_forget, scale=float(scale)),
        grid=(m // tm, n // tn),
        in_specs=in_specs,
        out_specs=out_specs,
        out_shape=out_shape,
        compiler_params=_params(("parallel", "arbitrary")),
        name="norm_proj",
    )(*args)


def _proj_kernel(x_ref, w_ref, *o_refs):
    x = x_ref[...]
    for g, o_ref in enumerate(o_refs):
        o_ref[...] = jnp.dot(x, w_ref[g], preferred_element_type=F32)


def proj(x, w, tm=512, tn=256):
    m, d = x.shape
    n_groups, _, gw = w.shape
    assert gw % tn == 0 and m % tm == 0
    return pl.pallas_call(
        _proj_kernel,
        grid=(m // tm, gw // tn),
        in_specs=[pl.BlockSpec((tm, d), lambda i, j: (i, 0)),
                  pl.BlockSpec((n_groups, d, tn), lambda i, j: (0, 0, j))],
        out_specs=[pl.BlockSpec((tm, tn), lambda i, j: (i, j))] * n_groups,
        out_shape=[jax.ShapeDtypeStruct((m, gw), F32)] * n_groups,
        compiler_params=_params(("parallel", "arbitrary")),
        name="proj",
    )(x, w)


def _proj_res_kernel(*refs, n_in):
    xs = refs[:n_in]
    ws = refs[n_in:2 * n_in]
    h_ref = refs[2 * n_in]
    o_ref = refs[2 * n_in + 1]
    acc = h_ref[...]
    for x_ref, w_ref in zip(xs, ws):
        acc = acc + jnp.dot(x_ref[...], w_ref[...], preferred_element_type=F32)
    o_ref[...] = acc


def proj_residual(xs, ws, h, tm=512, tn=2048):
    m, n = h.shape
    assert m % tm == 0 and n % tn == 0
    n_in = len(xs)
    in_specs = [pl.BlockSpec((tm, x.shape[1]), lambda i, j: (i, 0)) for x in xs]
    in_specs += [pl.BlockSpec((w.shape[0], tn), lambda i, j: (0, j)) for w in ws]
    in_specs.append(pl.BlockSpec((tm, tn), lambda i, j: (i, j)))
    return pl.pallas_call(
        functools.partial(_proj_res_kernel, n_in=n_in),
        grid=(m // tm, n // tn),
        in_specs=in_specs,
        out_specs=pl.BlockSpec((tm, tn), lambda i, j: (i, j)),
        out_shape=jax.ShapeDtypeStruct((m, n), F32),
        compiler_params=_params(("parallel", "arbitrary")),
        name="proj_residual",
    )(*xs, *ws, h)


def _swiglu_step(x, wg_ref, wu_ref, wd_ref):
    a = jnp.dot(x, wg_ref[...], preferred_element_type=F32)
    b = jnp.dot(x, wu_ref[...], preferred_element_type=F32)
    mid = (a * jax.nn.sigmoid(a)) * b
    return jnp.dot(mid.astype(BF16), wd_ref[...], preferred_element_type=F32)


def _ffn_dense_kernel(h_ref, g_ref, wg_ref, wu_ref, wd_ref, o_ref, xn_ref, acc_ref):
    j = pl.program_id(1)

    @pl.when(j == 0)
    def _():
        h = h_ref[...]
        xn_ref[...] = _rms(h, g_ref[...]).astype(BF16)
        acc_ref[...] = h

    acc_ref[...] += _swiglu_step(xn_ref[...], wg_ref, wu_ref, wd_ref)

    @pl.when(j == pl.num_programs(1) - 1)
    def _():
        o_ref[...] = acc_ref[...]


def ffn_dense(h, g, wg, wu, wd, tm=512, tf=512):
    m, d = h.shape
    f = wg.shape[1]
    assert m % tm == 0 and f % tf == 0
    return pl.pallas_call(
        _ffn_dense_kernel,
        grid=(m // tm, f // tf),
        in_specs=[
            pl.BlockSpec((tm, d), lambda i, j: (i, 0)),
            pl.BlockSpec((1, d), lambda i, j: (0, 0)),
            pl.BlockSpec((d, tf), lambda i, j: (0, j)),
            pl.BlockSpec((d, tf), lambda i, j: (0, j)),
            pl.BlockSpec((tf, d), lambda i, j: (j, 0)),
        ],
        out_specs=pl.BlockSpec((tm, d), lambda i, j: (i, 0)),
        out_shape=jax.ShapeDtypeStruct((m, d), F32),
        scratch_shapes=[pltpu.VMEM((tm, d), BF16), pltpu.VMEM((tm, d), F32)],
        compiler_params=_params(("parallel", "arbitrary")),
        name="ffn_dense",
    )(h, g.reshape(1, d), wg, wu, wd)


GATHER_STEPS = 16


def _row_copy(u_hbm, xbuf_ref, sem_ref, slot, src_row, dst_row):
    return pltpu.make_async_copy(u_hbm.at[pl.ds(src_row, 1)], xbuf_ref.at[slot, pl.ds(dst_row, 1)],
                                 sem_ref.at[slot])


def _ffn_moe_kernel(te_ref, nu_ref, tr_ref, tok_ref, u_hbm, wg_ref, wu_ref, wd_ref, o_ref,
                    mid_ref, xbuf_ref, x_ref, sem_ref, *, nj_a):
    t = pl.program_id(0)
    j = pl.program_id(1)
    half = MOE_TILE // 2
    per_step = MOE_TILE // GATHER_STEPS
    slot = t % 2

    def issue(tile, dst_slot, first_row, n_rows):
        for r in range(n_rows):
            row = first_row + r
            _row_copy(u_hbm, xbuf_ref, sem_ref, dst_slot, tok_ref[tile * MOE_TILE + row], row).start()

    @pl.when((t == 0) & (j == 0))
    def _():
        def body(c, carry):
            issue(0, 0, c * per_step, per_step)
            return carry
        lax.fori_loop(0, GATHER_STEPS, body, 0)

    @pl.when((t < nu_ref[0]) & (j == 0))
    def _():
        def body(r, carry):
            _row_copy(u_hbm, xbuf_ref, sem_ref, slot, 0, r).wait()
            return carry
        lax.fori_loop(0, MOE_TILE, body, 0)
        words = xbuf_ref[slot]
        lo = lax.bitcast_convert_type(words << 16, F32)
        hi = lax.bitcast_convert_type(words & jnp.uint32(0xFFFF0000), F32)
        x_ref[:, 0:words.shape[1]] = lo.astype(BF16)
        x_ref[:, words.shape[1]:] = hi.astype(BF16)

    @pl.when((t + 1 < nu_ref[0]) & (j < GATHER_STEPS))
    def _():
        issue(t + 1, 1 - slot, j * per_step, per_step)

    @pl.when(j < nj_a)
    def _():
        wg = wg_ref[...].astype(BF16)
        wu = wu_ref[...].astype(BF16)
        for s in range(2):
            @pl.when(s * half < tr_ref[t])
            def _(s=s):
                x = x_ref[s * half:(s + 1) * half, :]
                a = jnp.dot(x, wg, preferred_element_type=F32)
                b = jnp.dot(x, wu, preferred_element_type=F32)
                mid_ref[s, j] = ((a * jax.nn.sigmoid(a)) * b).astype(BF16)

    @pl.when(j >= nj_a)
    def _():
        wd = wd_ref[...].astype(BF16)
        for s in range(2):
            rows = slice(s * half, (s + 1) * half)

            @pl.when(s * half < tr_ref[t])
            def _(s=s, rows=rows):
                mid = jnp.concatenate([mid_ref[s, jm] for jm in range(nj_a)], axis=1)
                o_ref[rows, :] = jnp.dot(mid, wd, preferred_element_type=F32)

            @pl.when(s * half >= tr_ref[t])
            def _(rows=rows):
                o_ref[rows, :] = jnp.zeros((half, o_ref.shape[1]), F32)


def ffn_moe(tile_expert, n_used, tile_rows, row_token, u_packed, wg, wu, wd, tf=256, tn=256):
    p = row_token.shape[0]
    d = 2 * u_packed.shape[1]
    f = wg.shape[2]
    tm = MOE_TILE
    nt, nj_a, nj_b = p // tm, f // tf, d // tn
    assert nj_a + nj_b >= GATHER_STEPS and tm % GATHER_STEPS == 0

    def used(t, nu):
        return t < nu[0]

    def col_a(t, j, nu):
        return jnp.where(used(t, nu), jnp.minimum(j, nj_a - 1), nj_a - 1)

    def col_b(t, j, nu):
        return jnp.where(used(t, nu), jnp.clip(j - nj_a, 0, nj_b - 1), nj_b - 1)

    grid_spec = pltpu.PrefetchScalarGridSpec(
        num_scalar_prefetch=4,
        grid=(nt, nj_a + nj_b),
        in_specs=[
            pl.BlockSpec(memory_space=pl.ANY),
            pl.BlockSpec((None, d, tf), lambda t, j, te, nu, tr, tok: (te[t], 0, col_a(t, j, nu))),
            pl.BlockSpec((None, d, tf), lambda t, j, te, nu, tr, tok: (te[t], 0, col_a(t, j, nu))),
            pl.BlockSpec((None, f, tn), lambda t, j, te, nu, tr, tok: (te[t], 0, col_b(t, j, nu))),
        ],
        out_specs=pl.BlockSpec((tm, tn), lambda t, j, te, nu, tr, tok: (t, jnp.clip(j - nj_a, 0, nj_b - 1))),
        scratch_shapes=[pltpu.VMEM((2, nj_a, tm // 2, tf), BF16), pltpu.VMEM((2, tm, d // 2), jnp.uint32),
                        pltpu.VMEM((tm, d), BF16), pltpu.SemaphoreType.DMA((2,))],
    )
    return pl.pallas_call(
        functools.partial(_ffn_moe_kernel, nj_a=nj_a),
        grid_spec=grid_spec,
        out_shape=jax.ShapeDtypeStruct((p, d), F32),
        compiler_params=_params(("arbitrary", "arbitrary")),
        name="ffn_moe",
    )(tile_expert, n_used, tile_rows, row_token, u_packed, wg, wu, wd)


def _router_kernel(h_ref, g_ref, wr_ref, c0_ref, idx_ref, gate_ref, cnt_ref, u_ref, base_ref, *, tm):
    i = pl.program_id(0)

    @pl.when(i == 0)
    def _():
        base_ref[...] = c0_ref[...]

    u = _rms(h_ref[...], g_ref[...])
    bits = lax.bitcast_convert_type(u.astype(BF16).astype(F32), jnp.uint32)
    half_d = u.shape[1] // 2
    u_ref[...] = (bits[:, :half_d] >> 16) | (bits[:, half_d:] & jnp.uint32(0xFFFF0000))
    logits = jnp.dot(u, wr_ref[...], preferred_element_type=F32, precision=lax.Precision.HIGHEST)
    lane = lax.broadcasted_iota(I32, (tm, LANES), 1)
    logits = jnp.where(lane < N_EXPERTS, logits, NEG_INF)
    m1 = jnp.max(logits, axis=1, keepdims=True)
    i1 = jnp.min(jnp.where(logits == m1, lane, LANES), axis=1, keepdims=True)
    rest = jnp.where(lane == i1, NEG_INF, logits)
    m2 = jnp.max(rest, axis=1, keepdims=True)
    i2 = jnp.min(jnp.where(rest == m2, lane, LANES), axis=1, keepdims=True)
    e2 = jnp.exp(m2 - m1)
    g1 = 1.0 / (1.0 + e2)
    g2 = e2 / (1.0 + e2)

    sel1 = lane == i1
    sel2 = lane == i2
    onehot = jnp.where(sel1 | sel2, 1.0, 0.0)
    r = lax.broadcasted_iota(I32, (tm, tm), 0)
    c = lax.broadcasted_iota(I32, (tm, tm), 1)
    strict_lower = jnp.where(c < r, 1.0, 0.0).astype(BF16)
    before = jnp.dot(strict_lower, onehot.astype(BF16), preferred_element_type=F32) + base_ref[...]
    r1 = jnp.sum(jnp.where(sel1, before, 0.0), axis=1, keepdims=True)
    r2 = jnp.sum(jnp.where(sel2, before, 0.0), axis=1, keepdims=True)
    base_ref[...] += jnp.sum(onehot, axis=0, keepdims=True)

    idx = jnp.where(lane == 0, i1, jnp.where(lane == 1, i2,
          jnp.where(lane == 2, r1.astype(I32), r2.astype(I32))))
    idx_ref[...] = idx[:, :8]
    gate_ref[...] = jnp.where(lane == 0, g1, g2)[:, :8]

    @pl.when(i == pl.num_programs(0) - 1)
    def _():
        cnt_ref[...] = base_ref[...]


def router(h, g, wr, count0, tm):
    m, d = h.shape
    assert m % tm == 0
    return pl.pallas_call(
        functools.partial(_router_kernel, tm=tm),
        grid=(m // tm,),
        in_specs=[
            pl.BlockSpec((tm, d), lambda i: (i, 0)),
            pl.BlockSpec((1, d), lambda i: (0, 0)),
            pl.BlockSpec((d, LANES), lambda i: (0, 0)),
            pl.BlockSpec((1, LANES), lambda i: (0, 0)),
        ],
        out_specs=[
            pl.BlockSpec((tm, 8), lambda i: (i, 0)),
            pl.BlockSpec((tm, 8), lambda i: (i, 0)),
            pl.BlockSpec((1, LANES), lambda i: (0, 0)),
            pl.BlockSpec((tm, d // 2), lambda i: (i, 0)),
        ],
        out_shape=[
            jax.ShapeDtypeStruct((m, 8), I32),
            jax.ShapeDtypeStruct((m, 8), F32),
            jax.ShapeDtypeStruct((1, LANES), F32),
            jax.ShapeDtypeStruct((m, d // 2), jnp.uint32),
        ],
        scratch_shapes=[pltpu.VMEM((1, LANES), F32)],
        compiler_params=_params(("arbitrary",)),
        name="router",
    )(h, g.reshape(1, d), wr, count0)


def _combine_kernel(h_ref, y1_ref, y2_ref, gate_ref, g_ref, o_ref):
    gates = gate_ref[...]
    g1 = gates[:, 0:1]
    g2 = gates[:, 1:2]
    h = h_ref[...] + (g1 * y1_ref[...] + g2 * y2_ref[...])
    o_ref[...] = _rms(h, g_ref[...])


def combine_norm(h, y1, y2, gates, g, tm, y_row0):
    m, d = h.shape
    assert m % tm == 0 and y_row0 % tm == 0
    blk0 = y_row0 // tm
    row = pl.BlockSpec((tm, d), lambda i: (i, 0))
    yrow = pl.BlockSpec((tm, d), lambda i: (i + blk0, 0))
    return pl.pallas_call(
        _combine_kernel,
        grid=(m // tm,),
        in_specs=[row, yrow, yrow, pl.BlockSpec((tm, 8), lambda i: (i, 0)),
                  pl.BlockSpec((1, d), lambda i: (0, 0))],
        out_specs=row,
        out_shape=jax.ShapeDtypeStruct((m, d), F32),
        compiler_params=_params(("parallel",)),
        name="combine_norm",
    )(h, y1, y2, gates, g.reshape(1, d))


def _cumsum_kernel(x_ref, o_ref, *, tb, n_blk):
    r = lax.broadcasted_iota(I32, (tb, tb), 0)
    c = lax.broadcasted_iota(I32, (tb, tb), 1)
    upper = jnp.where(r <= c, 1.0, 0.0)
    carry = jnp.zeros((H_FOX, 1), F32)
    for b in range(n_blk):
        blk = x_ref[0, :, b * tb:(b + 1) * tb]
        cs = jnp.dot(blk, upper, preferred_element_type=F32, precision=lax.Precision.HIGHEST) + carry
        o_ref[0, :, b * tb:(b + 1) * tb] = cs
        carry = cs[:, tb - 1:tb]


def cumsum_time(x, tb):
    b, h, t = x.shape
    assert t % tb == 0
    spec = pl.BlockSpec((1, h, t), lambda i: (i, 0, 0))
    return pl.pallas_call(
        functools.partial(_cumsum_kernel, tb=tb, n_blk=t // tb),
        grid=(b,),
        in_specs=[spec],
        out_specs=spec,
        out_shape=jax.ShapeDtypeStruct((b, h, t), F32),
        compiler_params=_params(("parallel",)),
        name="cumsum_time",
    )(x)


def _rows_of(parts, r0, n):
    pieces, start = [], 0
    for part in parts:
        lo, hi = max(r0, start), min(r0 + n, start + part.shape[0])
        if lo < hi:
            pieces.append(part[lo - start:hi - start, :].astype(F32))
        start += part.shape[0]
    got = sum(p.shape[0] for p in pieces)
    if got < n:
        pieces.append(jnp.zeros((n - got, parts[0].shape[1]), F32))
    return pieces[0] if len(pieces) == 1 else jnp.concatenate(pieces, axis=0)


def _split3(x):
    hi = x.astype(BF16).astype(F32)
    r = x - hi
    mid = r.astype(BF16).astype(F32)
    return hi, mid, (r - mid).astype(BF16).astype(F32)


def _bias_cols(col):
    hi, mid, lo = _split3(col)
    lane = lax.broadcasted_iota(I32, (col.shape[0], LANES), 1)
    e = jnp.where(lane == 0, hi, jnp.where(lane == 1, mid, jnp.where(lane == 2, lo,
        jnp.where(lane < 6, 1.0, 0.0))))
    return e.astype(BF16)


def _bias_rows(row):
    hi, mid, lo = _split3(row)
    sub = lax.broadcasted_iota(I32, (LANES, row.shape[1]), 0)
    e = jnp.where(sub < 3, 1.0, jnp.where(sub == 3, hi, jnp.where(sub == 4, mid,
        jnp.where(sub == 5, lo, 0.0))))
    return e.astype(BF16)


def _ones_col(n):
    return jnp.where(lax.broadcasted_iota(I32, (n, LANES), 1) == 0, 1.0, 0.0).astype(BF16)


def _fox_kernel(*refs, tq, tk, q_off, n_q, has_cache):
    if has_cache:
        q_ref, k_ref, v_ref, kc_ref, vc_ref, cq_ref, ck_ref, o_ref, kt_ref, vb_ref, m_ref, acc_ref = refs
        k_parts, v_parts = (kc_ref, k_ref), (vc_ref, v_ref)
    else:
        q_ref, k_ref, v_ref, cq_ref, ck_ref, o_ref, kt_ref, vb_ref, m_ref, acc_ref = refs
        k_parts, v_parts = (k_ref,), (v_ref,)
    h = pl.program_id(1)
    qi = pl.program_id(2)

    @pl.when(qi == 0)
    def _():
        for blk in range(kt_ref.shape[0]):
            r0 = blk * tk
            kt_ref[blk, 0:HEAD_DIM, :] = _rows_of(k_parts, r0, tk).T.astype(BF16)
            kt_ref[blk, HEAD_DIM:, :] = _bias_rows(-LOG2E * ck_ref[0, h, blk])
            vb_ref[r0:r0 + tk, 0:HEAD_DIM] = _rows_of(v_parts, r0, tk).astype(BF16)
            vb_ref[r0:r0 + tk, HEAD_DIM:] = _ones_col(tk)

    qpos0 = q_off if n_q == 1 else q_off + qi * tq
    cq_row = LOG2E * cq_ref[0, h, q_off // tq + qi]
    eye = lax.broadcasted_iota(I32, (tq, tq), 0) == lax.broadcasted_iota(I32, (tq, tq), 1)
    cq = jnp.sum(jnp.where(eye, cq_row, 0.0), axis=1, keepdims=True)
    qx = jnp.concatenate([q_ref[...], _bias_cols(cq)], axis=1)
    row = lax.broadcasted_iota(I32, (tq, tk), 0)
    col = lax.broadcasted_iota(I32, (tq, tk), 1)

    def scores(kv):
        return jnp.dot(qx, kt_ref[kv], preferred_element_type=F32)

    def consume(s, kv, masked):
        if masked:
            s = jnp.where(kv * tk + col <= qpos0 + row, s, NEG_INF)
        m = m_ref[...]
        m_new = jnp.maximum(m, jnp.max(s, axis=1, keepdims=True))
        p = jnp.exp2(s - m_new).astype(BF16)
        k0 = pl.multiple_of(kv * tk, tk)
        pv = jnp.dot(p, vb_ref[pl.ds(k0, tk), :], preferred_element_type=F32)
        acc_ref[...] = jnp.exp2(m - m_new) * acc_ref[...] + pv
        m_ref[...] = m_new

    m_ref[...] = jnp.full(m_ref.shape, NEG_INF, F32)
    acc_ref[...] = jnp.zeros(acc_ref.shape, F32)
    n_full = qpos0 // tk

    def step(kv, s):
        s_next = scores(kv + 1)
        consume(s, kv, False)
        return s_next

    s_last = lax.fori_loop(0, n_full, step, scores(0))
    consume(s_last, n_full, True)
    acc = acc_ref[...]
    o_ref[...] = (acc[:, :HEAD_DIM] / acc[:, HEAD_DIM:HEAD_DIM + 1]).astype(o_ref.dtype)


def fox_attention(q, k, v, c, n_batch, t_q, t_new, q_off, tq, tk, cache=None):
    n_q = t_q // tq
    tc = c.shape[2]
    assert t_q % tq == 0 and q_off % tq == 0 and tc % tq == 0 and tc % tk == 0
    assert (q_off % tk) + tq <= tk if n_q == 1 else (tq == tk and q_off % tk == 0)
    c_q = c.reshape(n_batch, H_FOX, tc // tq, 1, tq)
    c_k = c.reshape(n_batch, H_FOX, tc // tk, 1, tk)
    (q, qg), (k, kg), (v, vg) = q, k, v
    in_specs = [pl.BlockSpec((None, tq, HEAD_DIM), lambda b, h, i: (qg, b * n_q + i, h)),
                pl.BlockSpec((None, t_new, HEAD_DIM), lambda b, h, i: (kg, b, h)),
                pl.BlockSpec((None, t_new, HEAD_DIM), lambda b, h, i: (vg, b, h))]
    args = [q, k, v]
    if cache is not None:
        past = cache[0].shape[1]
        assert past + t_new <= tc
        old = pl.BlockSpec((past, HEAD_DIM), lambda b, h, i: (b, h))
        in_specs += [old, old]
        args += [x.reshape(n_batch * past, H_FOX * HEAD_DIM) for x in cache]
    else:
        assert t_new == tc
    in_specs += [pl.BlockSpec((1, H_FOX, tc // tq, 1, tq), lambda b, h, i: (b, 0, 0, 0, 0)),
                 pl.BlockSpec((1, H_FOX, tc // tk, 1, tk), lambda b, h, i: (b, 0, 0, 0, 0))]
    return pl.pallas_call(
        functools.partial(_fox_kernel, tq=tq, tk=tk, q_off=q_off, n_q=n_q, has_cache=cache is not None),
        grid=(n_batch, H_FOX, n_q),
        in_specs=in_specs,
        out_specs=pl.BlockSpec((tq, HEAD_DIM), lambda b, h, i: (b * n_q + i, h)),
        out_shape=jax.ShapeDtypeStruct((n_batch * t_q, H_FOX * HEAD_DIM), BF16),
        scratch_shapes=[pltpu.VMEM((tc // tk, 2 * HEAD_DIM, tk), BF16), pltpu.VMEM((tc, 2 * HEAD_DIM), BF16),
                        pltpu.VMEM((tq, 1), F32), pltpu.VMEM((tq, 2 * HEAD_DIM), F32)],
        compiler_params=_params(("parallel", "parallel", "arbitrary")),
        name="fox_attention",
    )(*args, c_q, c_k)


def _diff_kernel(slope_ref, q_ref, k_ref, v_ref, *refs, tq, tk, q_off, n_q, n_valid, has_cache):
    if has_cache:
        kc0_ref, kc1_ref, vc_ref = refs[:3]
        refs = refs[3:]
    lq1_ref, lk1_ref, lq2_ref, lk2_ref, sg_ref, o_ref, kt_ref, vb_ref, m_ref, l_ref, acc_ref = refs
    h = pl.program_id(1)
    qi = pl.program_id(2)
    dv = 2 * HEAD_DIM
    slope2 = slope_ref[h] * LOG2E

    @pl.when(qi == 0)
    def _():
        v_parts = (vc_ref, v_ref) if has_cache else (v_ref,)
        for blk in range(kt_ref.shape[1]):
            r0 = blk * tk
            kpos_row = (r0 + lax.broadcasted_iota(I32, (1, tk), 1)).astype(F32)
            key_bias = _bias_rows(slope2 * kpos_row)
            for mp in range(2):
                new_rows = k_ref.at[:, mp * HEAD_DIM:(mp + 1) * HEAD_DIM]
                k_parts = ((kc0_ref, kc1_ref)[mp], new_rows) if has_cache else (new_rows,)
                kt_ref[mp, blk, 0:HEAD_DIM, :] = _rows_of(k_parts, r0, tk).T.astype(BF16)
                kt_ref[mp, blk, HEAD_DIM:, :] = key_bias
            vb_ref[r0:r0 + tk, :] = _rows_of(v_parts, r0, tk).astype(BF16)

    lam = (jnp.exp(jnp.sum(lq1_ref[...] * lk1_ref[...], axis=1, keepdims=True))
           - jnp.exp(jnp.sum(lq2_ref[...] * lk2_ref[...], axis=1, keepdims=True)) + LAM_INIT_L0)
    qpos0 = q_off if n_q == 1 else q_off + qi * tq
    qpos_col = (qpos0 + lax.broadcasted_iota(I32, (tq, 1), 0)).astype(F32)
    query_bias = _bias_cols(-slope2 * qpos_col)
    q = q_ref[...]
    qx = [jnp.concatenate([q[:, mp * HEAD_DIM:(mp + 1) * HEAD_DIM], query_bias], axis=1) for mp in range(2)]
    row = lax.broadcasted_iota(I32, (tq, tk), 0)
    col = lax.broadcasted_iota(I32, (tq, tk), 1)

    def scores(kv):
        return tuple(jnp.dot(qx[mp], kt_ref[mp, kv], preferred_element_type=F32) for mp in range(2))

    def consume(ss, kv, masked):
        k0 = pl.multiple_of(kv * tk, tk)
        vblk = vb_ref[pl.ds(k0, tk), :]
        if masked:
            qpos = qpos0 + row
            kpos = kv * tk + col
            visible = ((kpos >> 6) <= (qpos >> 6)) & (kpos < n_valid)
            fix = (2.0 * slope2) * jnp.maximum(kpos - qpos, 0).astype(F32)
        for mp in range(2):
            s = ss[mp]
            if masked:
                s = jnp.where(visible, s - fix, NEG_INF)
            m = m_ref[mp]
            m_new = jnp.maximum(m, jnp.max(s, axis=1, keepdims=True))
            alpha = jnp.exp2(m - m_new)
            p = jnp.exp2(s - m_new)
            l_ref[mp] = alpha * l_ref[mp] + jnp.sum(p, axis=1, keepdims=True)
            acc_ref[mp] = alpha * acc_ref[mp] + jnp.dot(p.astype(BF16), vblk, preferred_element_type=F32)
            m_ref[mp] = m_new

    m_ref[...] = jnp.full(m_ref.shape, NEG_INF, F32)
    l_ref[...] = jnp.zeros(l_ref.shape, F32)
    acc_ref[...] = jnp.zeros(acc_ref.shape, F32)
    n_full = qpos0 // tk

    def step(kv, ss):
        ss_next = scores(kv + 1)
        consume(ss, kv, False)
        return ss_next

    ss_last = lax.fori_loop(0, n_full, step, scores(0))
    consume(ss_last, n_full, True)
    o = acc_ref[0] / l_ref[0] - lam * (acc_ref[1] / l_ref[1])
    o_ref[...] = (_rms(o, sg_ref[...]) * (1.0 - LAM_INIT_L0)).astype(o_ref.dtype)


def diff_attention(q, k, v, lam_vecs, subln_g, n_batch, t_q, t_new, t_s, q_off, tq, tk, cache=None):
    n_q = t_q // tq
    assert t_q % tq == 0 and t_s % tk == 0
    dv = 2 * HEAD_DIM
    slopes = 2.0 ** (-8.0 * jnp.arange(1, H_DIFF + 1, dtype=F32) / H_DIFF)
    vec = pl.BlockSpec((1, HEAD_DIM), lambda b, h, i, s: (0, 0))
    (q, qg), (k, kg), (v, vg) = q, k, v
    in_specs = [pl.BlockSpec((None, tq, dv), lambda b, h, i, s: (qg, b * n_q + i, h)),
                pl.BlockSpec((None, t_new, dv), lambda b, h, i, s: (kg, b, h)),
                pl.BlockSpec((None, t_new, dv), lambda b, h, i, s: (vg, b, h))]
    args = [slopes, q, k, v]
    past = 0
    if cache is not None:
        past = cache[0].shape[1]
        in_specs += [
            pl.BlockSpec((past, HEAD_DIM), lambda b, h, i, s: (b, 2 * h)),
            pl.BlockSpec((past, HEAD_DIM), lambda b, h, i, s: (b, 2 * h + 1)),
            pl.BlockSpec((past, dv), lambda b, h, i, s: (b, h)),
        ]
        ck2d = cache[0].reshape(n_batch * past, H_DIFF * dv)
        args += [ck2d, ck2d, cache[1].reshape(n_batch * past, H_DIFF * dv)]
    n_valid = past + t_new
    assert n_valid <= t_s and tq % CHUNK == 0 or n_q == 1
    if n_q == 1:
        last_visible = min(((q_off + tq - 1) // CHUNK + 1) * CHUNK, n_valid)
        assert (q_off // tk) * tk <= n_valid and last_visible <= (q_off // tk + 1) * tk
    else:
        assert tq == tk and q_off % tk == 0
    in_specs += [vec, vec, vec, vec, pl.BlockSpec((1, dv), lambda b, h, i, s: (0, 0))]
    grid_spec = pltpu.PrefetchScalarGridSpec(
        num_scalar_prefetch=1,
        grid=(n_batch, H_DIFF, n_q),
        in_specs=in_specs,
        out_specs=pl.BlockSpec((tq, dv), lambda b, h, i, s: (b * n_q + i, h)),
        scratch_shapes=[pltpu.VMEM((2, t_s // tk, dv, tk), BF16), pltpu.VMEM((t_s, dv), BF16),
                        pltpu.VMEM((2, tq, 1), F32), pltpu.VMEM((2, tq, 1), F32), pltpu.VMEM((2, tq, dv), F32)],
    )
    return pl.pallas_call(
        functools.partial(_diff_kernel, tq=tq, tk=tk, q_off=q_off, n_q=n_q, n_valid=n_valid,
                          has_cache=cache is not None),
        grid_spec=grid_spec,
        out_shape=jax.ShapeDtypeStruct((n_batch * t_q, H_DIFF * dv), BF16),
        compiler_params=_params(("parallel", "parallel", "arbitrary")),
        name="diff_attention",
    )(*args, *[x.reshape(1, HEAD_DIM) for x in lam_vecs], subln_g.reshape(1, dv))


REL_PAD = 384


def _band_bias_kernel(t_ref, o_ref, *, nk, n_valid):
    blk = pl.program_id(0)
    row = lax.broadcasted_iota(I32, (REL_PAD, nk), 0)
    j = lax.broadcasted_iota(I32, (REL_PAD, nk), 1)
    jc = lax.broadcasted_iota(I32, (1, nk), 1) >> 6
    jv = lax.broadcasted_iota(I32, (1, nk), 1) < n_valid
    for r in range(8):
        i = blk * 8 + r
        idx = jnp.clip(i - j + BAND_PAST, -REL_CLIP, REL_CLIP) + REL_CLIP
        onehot = jnp.where(row == idx, 1.0, 0.0)
        bias = jnp.dot(t_ref[...], onehot, preferred_element_type=F32, precision=lax.Precision.HIGHEST)
        ic = i >> 6
        visible = (jc - N_PREV_CHUNKS <= ic) & (jc >= ic) & jv
        o_ref[:, r, :] = jnp.where(visible, bias * LOG2E, NEG_INF)


def band_bias(table_padded, tq, nk, n_valid):
    assert CHUNK == 64
    return pl.pallas_call(
        functools.partial(_band_bias_kernel, nk=nk, n_valid=n_valid),
        grid=(tq // 8,),
        in_specs=[pl.BlockSpec((H_BAND, REL_PAD), lambda i: (0, 0))],
        out_specs=pl.BlockSpec((H_BAND, 8, nk), lambda i: (0, i, 0)),
        out_shape=jax.ShapeDtypeStruct((H_BAND, tq, nk), F32),
        compiler_params=_params(("parallel",)),
        name="band_bias",
    )(table_padded)


def _band_kernel(*refs, tq, nk, n_g, q_off, has_cache):
    if has_cache:
        q_ref, k_ref, v_ref, kc_ref, vc_ref, b_ref, o_ref, kb_ref, vb_ref = refs
        k_parts, v_parts = (kc_ref, k_ref), (vc_ref, v_ref)
        front = 0
    else:
        q_ref, k_ref, v_ref, b_ref, o_ref, kb_ref, vb_ref = refs
        k_parts, v_parts = (k_ref,), (v_ref,)
        front = kb_ref.shape[0] - k_ref.shape[0]
        kb_ref[0:front, :] = jnp.zeros((front, HEAD_DIM), BF16)
        vb_ref[0:front, 0:HEAD_DIM] = jnp.zeros((front, HEAD_DIM), BF16)
    t_s = kb_ref.shape[0]
    kb_ref[front:, :] = _rows_of(k_parts, 0, t_s - front).astype(BF16)
    vb_ref[front:, 0:HEAD_DIM] = _rows_of(v_parts, 0, t_s - front).astype(BF16)
    vb_ref[:, HEAD_DIM:] = _ones_col(t_s)
    jrow = lax.broadcasted_iota(I32, (1, nk), 1)

    def row0(g):
        return g * tq if isinstance(g, int) else pl.multiple_of(g * tq, tq)

    def scores(g):
        r0 = row0(g)
        return lax.dot_general(q_ref[pl.ds(r0, tq), :], kb_ref[pl.ds(r0, nk), :], (((1,), (1,)), ((), ())),
                               preferred_element_type=F32)

    def consume(s, g, clip_start):
        r0 = row0(g)
        s = s + b_ref[0]
        if clip_start:
            s = s + jnp.where(jrow >= BAND_PAST - (q_off + g * tq), 0.0, NEG_INF)
        m = jnp.max(s, axis=1, keepdims=True)
        p = jnp.exp2(s - m).astype(BF16)
        o = jnp.dot(p, vb_ref[pl.ds(r0, nk), :], preferred_element_type=F32)
        o_ref[pl.ds(r0, tq), :] = (o[:, :HEAD_DIM] / o[:, HEAD_DIM:HEAD_DIM + 1]).astype(o_ref.dtype)

    n_clip = min(n_g, max(0, -(-(BAND_PAST - q_off) // tq)))
    s = scores(0)
    for g in range(n_clip):
        s_next = scores(g + 1) if g + 1 < n_g else None
        consume(s, g, True)
        s = s_next
    if n_g > n_clip:
        def step(g, s):
            s_next = scores(jnp.minimum(g + 1, n_g - 1))
            consume(s, g, False)
            return s_next

        lax.fori_loop(n_clip, n_g, step, s)


def band_attention(q, k, v, bias, n_batch, t_q, t_new, q_off, tq, cache=None):
    n_g = t_q // tq
    nk = bias.shape[2]
    t_s = (n_g - 1) * tq + nk
    rows = pl.BlockSpec((t_q, HEAD_DIM), lambda b, h: (b, h))
    (q, qg), (k, kg), (v, vg) = q, k, v
    in_specs = [pl.BlockSpec((None, t_q, HEAD_DIM), lambda b, h: (qg, b, h)),
                pl.BlockSpec((None, t_new, HEAD_DIM), lambda b, h: (kg, b, h)),
                pl.BlockSpec((None, t_new, HEAD_DIM), lambda b, h: (vg, b, h))]
    args = [q, k, v]
    if cache is not None:
        assert cache[0].shape[1] == BAND_PAST and BAND_PAST + t_new <= t_s and n_g == 1
        old = pl.BlockSpec((BAND_PAST, HEAD_DIM), lambda b, h: (b, h))
        in_specs += [old, old]
        args += [x.reshape(n_batch * BAND_PAST, H_BAND * HEAD_DIM) for x in cache]
    else:
        assert BAND_PAST + t_new == t_s
    in_specs.append(pl.BlockSpec((1, tq, nk), lambda b, h: (h, 0, 0)))
    return pl.pallas_call(
        functools.partial(_band_kernel, tq=tq, nk=nk, n_g=n_g, q_off=q_off, has_cache=cache is not None),
        grid=(n_batch, H_BAND),
        in_specs=in_specs,
        out_specs=rows,
        out_shape=jax.ShapeDtypeStruct((n_batch * t_q, H_BAND * HEAD_DIM), BF16),
        scratch_shapes=[pltpu.VMEM((t_s, HEAD_DIM), BF16), pltpu.VMEM((t_s, 2 * HEAD_DIM), BF16)],
        compiler_params=_params(("parallel", "parallel")),
        name="band_attention",
    )(*args, bias)


def _even_layer(h, n_batch, t, caches, w, sample):
    (g_mix, w_q, w_kv, w_f, b_f, lam_vecs, subln_g, w_out_a, w_out_b, g_ffn, wg, wu, wd) = w
    hw = H_FOX * HEAD_DIM
    tm = 512 if not sample else h.shape[0]
    tm_in = 1024 if not sample else tm
    q, xn, logf = norm_proj(h, g_mix, w_q, hw, Q_SCALE, wf=w_f, bf=b_f, tm=tm_in)
    kv = proj(xn, w_kv, tm=tm_in)
    fk, fv, dk, dv = [(x[None], 0) for x in kv]
    logf_t = jnp.swapaxes(logf.reshape(n_batch, t, H_FOX), 1, 2)
    if not sample:
        c = cumsum_time(logf_t, 512)
        oa = fox_attention((q, 0), fk, fv, c, n_batch, t, t, 0, 512, 512)
        ob = diff_attention((q, 1), dk, dv, lam_vecs, subln_g, n_batch, t, t, t, 0, 512, 512)
    else:
        cfk, cfv, cfl, cdk, cdv = caches
        past = cfk.shape[1]
        t_k = past + t
        t_pad = -(-t_k // 384) * 384
        lf = jnp.concatenate([jnp.swapaxes(cfl, 1, 2), logf_t], axis=2)
        c = cumsum_time(jnp.pad(lf, ((0, 0), (0, 0), (0, t_pad - t_k))), 384)
        oa = fox_attention((q, 0), fk, fv, c, n_batch, t, t, past, t, 384, cache=(cfk, cfv))
        ob = diff_attention((q, 1), dk, dv, lam_vecs, subln_g, n_batch, t, t, t_pad, past, t, 384,
                            cache=(cdk, cdv))
    h = proj_residual([oa, ob], [w_out_a, w_out_b], h, tm=tm)
    h = ffn_dense(h, g_ffn, wg, wu, wd, tm=tm)
    return h, (kv[0], kv[1], logf, kv[2], kv[3])


def _odd_attention(h, n_batch, t, caches, w, sample):
    (g_mix, w_q, w_kv, table, w_out) = w
    d = D_MODEL
    tm = 512 if not sample else h.shape[0]
    tm_in = 1024 if not sample else tm
    q, xn = norm_proj(h, g_mix, w_q, d, Q_SCALE, tm=tm_in)
    kv = proj(xn, w_kv, tm=tm_in)
    bk, bv = [(x[None], 0) for x in kv]
    if not sample:
        tq = 2 * CHUNK
        bias = band_bias(table, tq, BAND_PAST + tq, BAND_PAST + tq)
        o = band_attention((q, 0), bk, bv, bias, n_batch, t, t, 0, tq)
    else:
        cbk, cbv, past_len = caches
        pc = cbk.shape[1]
        assert pc == BAND_PAST and t <= CHUNK and past_len % CHUNK == 0
        nk = -(-(pc + t) // LANES) * LANES
        bias = band_bias(table, t, nk, pc + t)
        o = band_attention((q, 0), bk, bv, bias, n_batch, t, t, past_len, t, cache=(cbk, cbv))
    h = proj_residual([o], [w_out], h, tm=tm)
    return h, (kv[0], kv[1])


def kernel(x_prompt, x_sample, cache_fox_k, cache_fox_v, cache_fox_logf, cache_diff_k, cache_diff_v, cache_band_k, cache_band_v, norm_mix_even, w_in_even, b_forget, lam_q1, lam_k1, lam_q2, lam_k2, subln_g, w_out_even, norm_ffn_even, w_gate, w_up, w_down, norm_mix_odd, w_in_odd, rel_bias, w_out_odd, norm_ffn_odd, w_router, w_gate_e, w_up_e, w_down_e, norm_final):
    bp, tp, d = x_prompt.shape
    bs, ts, _ = x_sample.shape
    past = cache_fox_k.shape[2]
    hw = H_FOX * HEAD_DIM
    fox_w = 3 * hw + H_FOX

    w_in0 = w_in_even[0]
    w_q0 = jnp.concatenate([w_in0[:, :hw], w_in0[:, fox_w:fox_w + hw]], axis=1).astype(BF16)
    w_kv0 = jnp.stack([w_in0[:, hw:2 * hw], w_in0[:, 2 * hw:3 * hw], w_in0[:, fox_w + hw:fox_w + 2 * hw],
                       w_in0[:, fox_w + 2 * hw:]]).astype(BF16)
    w_f = jnp.pad(w_in0[:, 3 * hw:fox_w], ((0, 0), (0, LANES - H_FOX))).astype(BF16)
    b_f = jnp.pad(b_forget[0], (0, LANES - H_FOX)).reshape(1, LANES)
    lam_vecs = (lam_q1[0], lam_k1[0], lam_q2[0], lam_k2[0])
    w_out0 = w_out_even[0].astype(BF16)
    even_w = (norm_mix_even[0], w_q0, w_kv0, w_f, b_f, lam_vecs, subln_g[0], w_out0[:hw], w_out0[hw:],
              norm_ffn_even[0], w_gate[0].astype(BF16), w_up[0].astype(BF16), w_down[0].astype(BF16))
    table = jnp.pad(rel_bias[0], ((0, 0), (0, REL_PAD - rel_bias.shape[2])))
    w_in1 = w_in_odd[0].astype(BF16)
    odd_w = (norm_mix_odd[0], w_in1[:, :d], jnp.stack([w_in1[:, d:2 * d], w_in1[:, 2 * d:]]), table,
             w_out_odd[0].astype(BF16))
    wr = jnp.pad(w_router[0], ((0, 0), (0, LANES - N_EXPERTS)))

    mp, ms = bp * tp, bs * ts
    h_p = x_prompt.reshape(mp, d)
    h_s = x_sample.reshape(ms, d)

    h_p, (fk_p, fv_p, fl_p, dk_p, dv_p) = _even_layer(h_p, bp, tp, None, even_w, False)
    caches_even = (cache_fox_k[0], cache_fox_v[0], cache_fox_logf[0], cache_diff_k[0], cache_diff_v[0])
    h_s, (fk_s, fv_s, fl_s, dk_s, dv_s) = _even_layer(h_s, bs, ts, caches_even, even_w, True)

    h_p, (bk_p, bv_p) = _odd_attention(h_p, bp, tp, None, odd_w, False)
    h_s, (bk_s, bv_s) = _odd_attention(h_s, bs, ts, (cache_band_k[0], cache_band_v[0], past), odd_w, True)

    m_all = mp + ms
    assert mp % 512 == 0 and mp % ms == 0
    zero_counts = jnp.zeros((1, LANES), F32)
    idx_p, gate_p, cnt_p, u_p = router(h_p, norm_ffn_odd[0], wr, zero_counts, 512)
    idx_s, gate_s, cnt, u_s = router(h_s, norm_ffn_odd[0], wr, cnt_p, ms)
    idx = jnp.concatenate([idx_p, idx_s], axis=0)
    u_all = jnp.concatenate([u_p, u_s], axis=0)
    counts = cnt[0, :N_EXPERTS].astype(I32)
    n_tiles_e = (counts + MOE_TILE - 1) // MOE_TILE
    tile_end = jnp.cumsum(n_tiles_e)
    tile_start = tile_end - n_tiles_e
    row_off = tile_start * MOE_TILE
    n_used = tile_end[-1:]
    nt = (2 * m_all) // MOE_TILE + N_EXPERTS
    tile_ids = jnp.arange(nt, dtype=I32)
    last = jnp.maximum(n_used[0] - 1, 0)
    tile_expert = jnp.sum((jnp.minimum(tile_ids, last)[:, None] >= tile_end[None, :]).astype(I32), axis=1)
    tile_expert = jnp.minimum(tile_expert, N_EXPERTS - 1)
    tile_rows = jnp.clip(counts[tile_expert] - (tile_ids - tile_start[tile_expert]) * MOE_TILE, 0, MOE_TILE)
    pos1 = row_off[idx[:, 0]] + idx[:, 2]
    pos2 = row_off[idx[:, 1]] + idx[:, 3]
    tokens = jnp.arange(m_all, dtype=I32)
    row_token = jnp.zeros((nt * MOE_TILE,), I32).at[pos1].set(tokens).at[pos2].set(tokens)
    y = ffn_moe(tile_expert, n_used.astype(I32), tile_rows.astype(I32), row_token, u_all,
                w_gate_e[0], w_up_e[0], w_down_e[0])
    y1 = jnp.take(y, pos1, axis=0, mode="clip")
    y2 = jnp.take(y, pos2, axis=0, mode="clip")
    y_p = combine_norm(h_p, y1, y2, gate_p, norm_final, 512, 0)
    y_s = combine_norm(h_s, y1, y2, gate_s, norm_final, ms, mp)

    keep = min(BAND_PAST, tp)
    bk_p = bk_p.reshape(bp, tp, H_BAND, HEAD_DIM)[:, tp - keep:][None]
    bv_p = bv_p.reshape(bp, tp, H_BAND, HEAD_DIM)[:, tp - keep:][None]
    return (
        y_p.reshape(bp, tp, d), y_s.reshape(bs, ts, d),
        fk_p.reshape(1, bp, tp, H_FOX, HEAD_DIM), fk_s.reshape(1, bs, ts, H_FOX, HEAD_DIM),
        fv_p.reshape(1, bp, tp, H_FOX, HEAD_DIM), fv_s.reshape(1, bs, ts, H_FOX, HEAD_DIM),
        fl_p.reshape(1, bp, tp, H_FOX), fl_s.reshape(1, bs, ts, H_FOX),
        dk_p.reshape(1, bp, tp, H_DIFF, 2, HEAD_DIM), dk_s.reshape(1, bs, ts, H_DIFF, 2, HEAD_DIM),
        dv_p.reshape(1, bp, tp, H_DIFF, 2 * HEAD_DIM), dv_s.reshape(1, bs, ts, H_DIFF, 2 * HEAD_DIM),
        bk_p, bk_s.reshape(1, bs, ts, H_BAND, HEAD_DIM),
        bv_p, bv_s.reshape(1, bs, ts, H_BAND, HEAD_DIM),
    )
```

```python
import functools
import math

import jax
import jax.numpy as jnp
from jax import lax
from jax.experimental import pallas as pl
from jax.experimental.pallas import tpu as pltpu

F32 = jnp.float32
BF16 = jnp.bfloat16
I32 = jnp.int32

D_MODEL = 2048
HEAD_DIM = 128
CHUNK = 64
H_FOX = 8
H_DIFF = 4
H_BAND = 16
N_PREV_CHUNKS = 8
BAND_PAST = N_PREV_CHUNKS * CHUNK
REL_CLIP = 128
D_FF = 5632
N_EXPERTS = 8
EPS = 1e-6
SCALE = HEAD_DIM ** -0.5
LAM_INIT_L0 = 0.8 - 0.6 * math.exp(-0.3 * 0)
NEG_INF = float("-inf")
LOG2E = 1.4426950408889634
Q_SCALE = SCALE * LOG2E

LANES = 128
VMEM_LIMIT = 56 * 1024 * 1024
MOE_TILE = 1024


def _params(sem):
    return pltpu.CompilerParams(dimension_semantics=sem, vmem_limit_bytes=VMEM_LIMIT)


def _rms(x, g):
    return (x * lax.rsqrt(jnp.mean(x * x, axis=-1, keepdims=True) + EPS)) * g


def _log_sigmoid(x):
    return jnp.minimum(x, 0.0) - jnp.log1p(jnp.exp(-jnp.abs(x)))


def _norm_proj_kernel(*refs, has_forget, scale):
    if has_forget:
        x_ref, g_ref, w_ref, wf_ref, bf_ref, o_ref, xn_ref, logf_ref = refs
    else:
        x_ref, g_ref, w_ref, o_ref, xn_ref = refs

    @pl.when(pl.program_id(1) == 0)
    def _():
        xn = _rms(x_ref[...], g_ref[...]).astype(BF16)
        xn_ref[...] = xn
        if has_forget:
            fa = jnp.dot(xn, wf_ref[...], preferred_element_type=F32)
            logf_ref[...] = _log_sigmoid(fa + bf_ref[...])[:, :H_FOX]

    z = jnp.dot(xn_ref[...], w_ref[...], preferred_element_type=F32)
    o_ref[...] = (z * scale).astype(o_ref.dtype)


def norm_proj(x, g, w, group_width, scale, wf=None, bf=None, tm=512, tn=512):
    m, d = x.shape
    n = w.shape[1]
    assert n % group_width == 0 and group_width % tn == 0 and m % tm == 0
    nb = group_width // tn
    has_forget = wf is not None
    in_specs = [
        pl.BlockSpec((tm, d), lambda i, j: (i, 0)),
        pl.BlockSpec((1, d), lambda i, j: (0, 0)),
        pl.BlockSpec((d, tn), lambda i, j: (0, j)),
    ]
    args = [x, g.reshape(1, d), w]
    out_specs = [pl.BlockSpec((None, tm, tn), lambda i, j: (j // nb, i, j % nb)),
                 pl.BlockSpec((tm, d), lambda i, j: (i, 0))]
    out_shape = [jax.ShapeDtypeStruct((n // group_width, m, group_width), BF16),
                 jax.ShapeDtypeStruct((m, d), BF16)]
    if has_forget:
        in_specs += [pl.BlockSpec((d, LANES), lambda i, j: (0, 0)),
                     pl.BlockSpec((1, LANES), lambda i, j: (0, 0))]
        args += [wf, bf]
        out_specs.append(pl.BlockSpec((tm, H_FOX), lambda i, j: (i, 0)))
        out_shape.append(jax.ShapeDtypeStruct((m, H_FOX), F32))
    return pl.pallas_call(
        functools.partial(_norm_proj_kernel, has_forget=has_forget, scale=float(scale)),
        grid=(m // tm, n // tn),
        in_specs=in_specs,
        out_specs=out_specs,
        out_shape=out_shape,
        compiler_params=_params(("parallel", "arbitrary")),
        name="norm_proj",
    )(*args)


def _proj_kernel(x_ref, w_ref, *o_refs):
    x = x_ref[...]
    for g, o_ref in enumerate(o_refs):
        o_ref[...] = jnp.dot(x, w_ref[g], preferred_element_type=F32)


def proj(x, w, tm=512, tn=256):
    m, d = x.shape
    n_groups, _, gw = w.shape
    assert gw % tn == 0 and m % tm == 0
    return pl.pallas_call(
        _proj_kernel,
        grid=(m // tm, gw // tn),
        in_specs=[pl.BlockSpec((tm, d), lambda i, j: (i, 0)),
                  pl.BlockSpec((n_groups, d, tn), lambda i, j: (0, 0, j))],
        out_specs=[pl.BlockSpec((tm, tn), lambda i, j: (i, j))] * n_groups,
        out_shape=[jax.ShapeDtypeStruct((m, gw), F32)] * n_groups,
        compiler_params=_params(("parallel", "arbitrary")),
        name="proj",
    )(x, w)


def _proj_res_kernel(*refs, n_in):
    xs = refs[:n_in]
    ws = refs[n_in:2 * n_in]
    h_ref = refs[2 * n_in]
    o_ref = refs[2 * n_in + 1]
    acc = h_ref[...]
    for x_ref, w_ref in zip(xs, ws):
        acc = acc + jnp.dot(x_ref[...], w_ref[...], preferred_element_type=F32)
    o_ref[...] = acc


def proj_residual(xs, ws, h, tm=512, tn=2048):
    m, n = h.shape
    assert m % tm == 0 and n % tn == 0
    n_in = len(xs)
    in_specs = [pl.BlockSpec((tm, x.shape[1]), lambda i, j: (i, 0)) for x in xs]
    in_specs += [pl.BlockSpec((w.shape[0], tn), lambda i, j: (0, j)) for w in ws]
    in_specs.append(pl.BlockSpec((tm, tn), lambda i, j: (i, j)))
    return pl.pallas_call(
        functools.partial(_proj_res_kernel, n_in=n_in),
        grid=(m // tm, n // tn),
        in_specs=in_specs,
        out_specs=pl.BlockSpec((tm, tn), lambda i, j: (i, j)),
        out_shape=jax.ShapeDtypeStruct((m, n), F32),
        compiler_params=_params(("parallel", "arbitrary")),
        name="proj_residual",
    )(*xs, *ws, h)


def _swiglu_step(x, wg_ref, wu_ref, wd_ref):
    a = jnp.dot(x, wg_ref[...], preferred_element_type=F32)
    b = jnp.dot(x, wu_ref[...], preferred_element_type=F32)
    mid = (a * jax.nn.sigmoid(a)) * b
    return jnp.dot(mid.astype(BF16), wd_ref[...], preferred_element_type=F32)


def _ffn_dense_kernel(h_ref, g_ref, wg_ref, wu_ref, wd_ref, o_ref, xn_ref, acc_ref):
    j = pl.program_id(1)

    @pl.when(j == 0)
    def _():
        h = h_ref[...]
        xn_ref[...] = _rms(h, g_ref[...]).astype(BF16)
        acc_ref[...] = h

    acc_ref[...] += _swiglu_step(xn_ref[...], wg_ref, wu_ref, wd_ref)

    @pl.when(j == pl.num_programs(1) - 1)
    def _():
        o_ref[...] = acc_ref[...]


def ffn_dense(h, g, wg, wu, wd, tm=512, tf=512):
    m, d = h.shape
    f = wg.shape[1]
    assert m % tm == 0 and f % tf == 0
    return pl.pallas_call(
        _ffn_dense_kernel,
        grid=(m // tm, f // tf),
        in_specs=[
            pl.BlockSpec((tm, d), lambda i, j: (i, 0)),
            pl.BlockSpec((1, d), lambda i, j: (0, 0)),
            pl.BlockSpec((d, tf), lambda i, j: (0, j)),
            pl.BlockSpec((d, tf), lambda i, j: (0, j)),
            pl.BlockSpec((tf, d), lambda i, j: (j, 0)),
        ],
        out_specs=pl.BlockSpec((tm, d), lambda i, j: (i, 0)),
        out_shape=jax.ShapeDtypeStruct((m, d), F32),
        scratch_shapes=[pltpu.VMEM((tm, d), BF16), pltpu.VMEM((tm, d), F32)],
        compiler_params=_params(("parallel", "arbitrary")),
        name="ffn_dense",
    )(h, g.reshape(1, d), wg, wu, wd)


GATHER_STEPS = 16


def _row_copy(u_hbm, xbuf_ref, sem_ref, slot, src_row, dst_row):
    return pltpu.make_async_copy(u_hbm.at[pl.ds(src_row, 1)], xbuf_ref.at[slot, pl.ds(dst_row, 1)],
                                 sem_ref.at[slot])


def _ffn_moe_kernel(te_ref, nu_ref, tr_ref, tok_ref, u_hbm, wg_ref, wu_ref, wd_ref, o_ref,
                    mid_ref, xbuf_ref, x_ref, sem_ref, *, nj_a):
    t = pl.program_id(0)
    j = pl.program_id(1)
    half = MOE_TILE // 2
    per_step = MOE_TILE // GATHER_STEPS
    slot = t % 2

    def issue(tile, dst_slot, first_row, n_rows):
        for r in range(n_rows):
            row = first_row + r
            _row_copy(u_hbm, xbuf_ref, sem_ref, dst_slot, tok_ref[tile * MOE_TILE + row], row).start()

    @pl.when((t == 0) & (j == 0))
    def _():
        def body(c, carry):
            issue(0, 0, c * per_step, per_step)
            return carry
        lax.fori_loop(0, GATHER_STEPS, body, 0)

    @pl.when((t < nu_ref[0]) & (j == 0))
    def _():
        def body(r, carry):
            _row_copy(u_hbm, xbuf_ref, sem_ref, slot, 0, r).wait()
            return carry
        lax.fori_loop(0, MOE_TILE, body, 0)
        words = xbuf_ref[slot]
        lo = lax.bitcast_convert_type(words << 16, F32)
        hi = lax.bitcast_convert_type(words & jnp.uint32(0xFFFF0000), F32)
        x_ref[:, 0:words.shape[1]] = lo.astype(BF16)
        x_ref[:, words.shape[1]:] = hi.astype(BF16)

    @pl.when((t + 1 < nu_ref[0]) & (j < GATHER_STEPS))
    def _():
        issue(t + 1, 1 - slot, j * per_step, per_step)

    @pl.when(j < nj_a)
    def _():
        wg = wg_ref[...].astype(BF16)
        wu = wu_ref[...].astype(BF16)
        for s in range(2):
            @pl.when(s * half < tr_ref[t])
            def _(s=s):
                x = x_ref[s * half:(s + 1) * half, :]
                a = jnp.dot(x, wg, preferred_element_type=F32)
                b = jnp.dot(x, wu, preferred_element_type=F32)
                mid_ref[s, j] = ((a * jax.nn.sigmoid(a)) * b).astype(BF16)

    @pl.when(j >= nj_a)
    def _():
        wd = wd_ref[...].astype(BF16)
        for s in range(2):
            rows = slice(s * half, (s + 1) * half)

            @pl.when(s * half < tr_ref[t])
            def _(s=s, rows=rows):
                mid = jnp.concatenate([mid_ref[s, jm] for jm in range(nj_a)], axis=1)
                o_ref[rows, :] = jnp.dot(mid, wd, preferred_element_type=F32)

            @pl.when(s * half >= tr_ref[t])
            def _(rows=rows):
                o_ref[rows, :] = jnp.zeros((half, o_ref.shape[1]), F32)


def ffn_moe(tile_expert, n_used, tile_rows, row_token, u_packed, wg, wu, wd, tf=256, tn=256):
    p = row_token.shape[0]
    d = 2 * u_packed.shape[1]
    f = wg.shape[2]
    tm = MOE_TILE
    nt, nj_a, nj_b = p // tm, f // tf, d // tn
    assert nj_a + nj_b >= GATHER_STEPS and tm % GATHER_STEPS == 0

    def used(t, nu):
        return t < nu[0]

    def col_a(t, j, nu):
        return jnp.where(used(t, nu), jnp.minimum(j, nj_a - 1), nj_a - 1)

    def col_b(t, j, nu):
        return jnp.where(used(t, nu), jnp.clip(j - nj_a, 0, nj_b - 1), nj_b - 1)

    grid_spec = pltpu.PrefetchScalarGridSpec(
        num_scalar_prefetch=4,
        grid=(nt, nj_a + nj_b),
        in_specs=[
            pl.BlockSpec(memory_space=pl.ANY),
            pl.BlockSpec((None, d, tf), lambda t, j, te, nu, tr, tok: (te[t], 0, col_a(t, j, nu))),
            pl.BlockSpec((None, d, tf), lambda t, j, te, nu, tr, tok: (te[t], 0, col_a(t, j, nu))),
            pl.BlockSpec((None, f, tn), lambda t, j, te, nu, tr, tok: (te[t], 0, col_b(t, j, nu))),
        ],
        out_specs=pl.BlockSpec((tm, tn), lambda t, j, te, nu, tr, tok: (t, jnp.clip(j - nj_a, 0, nj_b - 1))),
        scratch_shapes=[pltpu.VMEM((2, nj_a, tm // 2, tf), BF16), pltpu.VMEM((2, tm, d // 2), jnp.uint32),
                        pltpu.VMEM((tm, d), BF16), pltpu.SemaphoreType.DMA((2,))],
    )
    return pl.pallas_call(
        functools.partial(_ffn_moe_kernel, nj_a=nj_a),
        grid_spec=grid_spec,
        out_shape=jax.ShapeDtypeStruct((p, d), F32),
        compiler_params=_params(("arbitrary", "arbitrary")),
        name="ffn_moe",
    )(tile_expert, n_used, tile_rows, row_token, u_packed, wg, wu, wd)


def _router_kernel(h_ref, g_ref, wr_ref, c0_ref, idx_ref, gate_ref, cnt_ref, u_ref, base_ref, *, tm):
    i = pl.program_id(0)

    @pl.when(i == 0)
    def _():
        base_ref[...] = c0_ref[...]

    u = _rms(h_ref[...], g_ref[...])
    bits = lax.bitcast_convert_type(u.astype(BF16).astype(F32), jnp.uint32)
    half_d = u.shape[1] // 2
    u_ref[...] = (bits[:, :half_d] >> 16) | (bits[:, half_d:] & jnp.uint32(0xFFFF0000))
    logits = jnp.dot(u, wr_ref[...], preferred_element_type=F32, precision=lax.Precision.HIGHEST)
    lane = lax.broadcasted_iota(I32, (tm, LANES), 1)
    logits = jnp.where(lane < N_EXPERTS, logits, NEG_INF)
    m1 = jnp.max(logits, axis=1, keepdims=True)
    i1 = jnp.min(jnp.where(logits == m1, lane, LANES), axis=1, keepdims=True)
    rest = jnp.where(lane == i1, NEG_INF, logits)
    m2 = jnp.max(rest, axis=1, keepdims=True)
    i2 = jnp.min(jnp.where(rest == m2, lane, LANES), axis=1, keepdims=True)
    e2 = jnp.exp(m2 - m1)
    g1 = 1.0 / (1.0 + e2)
    g2 = e2 / (1.0 + e2)

    sel1 = lane == i1
    sel2 = lane == i2
    onehot = jnp.where(sel1 | sel2, 1.0, 0.0)
    r = lax.broadcasted_iota(I32, (tm, tm), 0)
    c = lax.broadcasted_iota(I32, (tm, tm), 1)
    strict_lower = jnp.where(c < r, 1.0, 0.0).astype(BF16)
    before = jnp.dot(strict_lower, onehot.astype(BF16), preferred_element_type=F32) + base_ref[...]
    r1 = jnp.sum(jnp.where(sel1, before, 0.0), axis=1, keepdims=True)
    r2 = jnp.sum(jnp.where(sel2, before, 0.0), axis=1, keepdims=True)
    base_ref[...] += jnp.sum(onehot, axis=0, keepdims=True)

    idx = jnp.where(lane == 0, i1, jnp.where(lane == 1, i2,
          jnp.where(lane == 2, r1.astype(I32), r2.astype(I32))))
    idx_ref[...] = idx[:, :8]
    gate_ref[...] = jnp.where(lane == 0, g1, g2)[:, :8]

    @pl.when(i == pl.num_programs(0) - 1)
    def _():
        cnt_ref[...] = base_ref[...]


def router(h, g, wr, count0, tm):
    m, d = h.shape
    assert m % tm == 0
    return pl.pallas_call(
        functools.partial(_router_kernel, tm=tm),
        grid=(m // tm,),
        in_specs=[
            pl.BlockSpec((tm, d), lambda i: (i, 0)),
            pl.BlockSpec((1, d), lambda i: (0, 0)),
            pl.BlockSpec((d, LANES), lambda i: (0, 0)),
            pl.BlockSpec((1, LANES), lambda i: (0, 0)),
        ],
        out_specs=[
            pl.BlockSpec((tm, 8), lambda i: (i, 0)),
            pl.BlockSpec((tm, 8), lambda i: (i, 0)),
            pl.BlockSpec((1, LANES), lambda i: (0, 0)),
            pl.BlockSpec((tm, d // 2), lambda i: (i, 0)),
        ],
        out_shape=[
            jax.ShapeDtypeStruct((m, 8), I32),
            jax.ShapeDtypeStruct((m, 8), F32),
            jax.ShapeDtypeStruct((1, LANES), F32),
            jax.ShapeDtypeStruct((m, d // 2), jnp.uint32),
        ],
        scratch_shapes=[pltpu.VMEM((1, LANES), F32)],
        compiler_params=_params(("arbitrary",)),
        name="router",
    )(h, g.reshape(1, d), wr, count0)


def _combine_kernel(h_ref, y1_ref, y2_ref, gate_ref, g_ref, o_ref):
    gates = gate_ref[...]
    g1 = gates[:, 0:1]
    g2 = gates[:, 1:2]
    h = h_ref[...] + (g1 * y1_ref[...] + g2 * y2_ref[...])
    o_ref[...] = _rms(h, g_ref[...])


def combine_norm(h, y1, y2, gates, g, tm, y_row0):
    m, d = h.shape
    assert m % tm == 0 and y_row0 % tm == 0
    blk0 = y_row0 // tm
    row = pl.BlockSpec((tm, d), lambda i: (i, 0))
    yrow = pl.BlockSpec((tm, d), lambda i: (i + blk0, 0))
    return pl.pallas_call(
        _combine_kernel,
        grid=(m // tm,),
        in_specs=[row, yrow, yrow, pl.BlockSpec((tm, 8), lambda i: (i, 0)),
                  pl.BlockSpec((1, d), lambda i: (0, 0))],
        out_specs=row,
        out_shape=jax.ShapeDtypeStruct((m, d), F32),
        compiler_params=_params(("parallel",)),
        name="combine_norm",
    )(h, y1, y2, gates, g.reshape(1, d))


def _cumsum_kernel(x_ref, o_ref, *, tb, n_blk):
    r = lax.broadcasted_iota(I32, (tb, tb), 0)
    c = lax.broadcasted_iota(I32, (tb, tb), 1)
    upper = jnp.where(r <= c, 1.0, 0.0)
    carry = jnp.zeros((H_FOX, 1), F32)
    for b in range(n_blk):
        blk = x_ref[0, :, b * tb:(b + 1) * tb]
        cs = jnp.dot(blk, upper, preferred_element_type=F32, precision=lax.Precision.HIGHEST) + carry
        o_ref[0, :, b * tb:(b + 1) * tb] = cs
        carry = cs[:, tb - 1:tb]


def cumsum_time(x, tb):
    b, h, t = x.shape
    assert t % tb == 0
    spec = pl.BlockSpec((1, h, t), lambda i: (i, 0, 0))
    return pl.pallas_call(
        functools.partial(_cumsum_kernel, tb=tb, n_blk=t // tb),
        grid=(b,),
        in_specs=[spec],
        out_specs=spec,
        out_shape=jax.ShapeDtypeStruct((b, h, t), F32),
        compiler_params=_params(("parallel",)),
        name="cumsum_time",
    )(x)


def _rows_of(parts, r0, n):
    pieces, start = [], 0
    for part in parts:
        lo, hi = max(r0, start), min(r0 + n, start + part.shape[0])
        if lo < hi:
            pieces.append(part[lo - start:hi - start, :].astype(F32))
        start += part.shape[0]
    got = sum(p.shape[0] for p in pieces)
    if got < n:
        pieces.append(jnp.zeros((n - got, parts[0].shape[1]), F32))
    return pieces[0] if len(pieces) == 1 else jnp.concatenate(pieces, axis=0)


def _split3(x):
    hi = x.astype(BF16).astype(F32)
    r = x - hi
    mid = r.astype(BF16).astype(F32)
    return hi, mid, (r - mid).astype(BF16).astype(F32)


def _bias_cols(col):
    hi, mid, lo = _split3(col)
    lane = lax.broadcasted_iota(I32, (col.shape[0], LANES), 1)
    e = jnp.where(lane == 0, hi, jnp.where(lane == 1, mid, jnp.where(lane == 2, lo,
        jnp.where(lane < 6, 1.0, 0.0))))
    return e.astype(BF16)


def _bias_rows(row):
    hi, mid, lo = _split3(row)
    sub = lax.broadcasted_iota(I32, (LANES, row.shape[1]), 0)
    e = jnp.where(sub < 3, 1.0, jnp.where(sub == 3, hi, jnp.where(sub == 4, mid,
        jnp.where(sub == 5, lo, 0.0))))
    return e.astype(BF16)


def _ones_col(n):
    return jnp.where(lax.broadcasted_iota(I32, (n, LANES), 1) == 0, 1.0, 0.0).astype(BF16)


def _run_blocks(n_full, scores, consume, sa_ref, sb_ref):
    n = sa_ref.shape[0]

    def put(ref, arrays):
        for i, a in enumerate(arrays):
            ref[i] = a

    def get(ref):
        return tuple(ref[i] for i in range(n))

    def pair(i, carry):
        kv = 2 * i
        put(sb_ref, scores(kv + 1))
        consume(get(sa_ref), kv, False)
        put(sa_ref, scores(kv + 2))
        consume(get(sb_ref), kv + 1, False)
        return carry

    def tail_odd():
        put(sb_ref, scores(n_full))
        consume(get(sa_ref), n_full - 1, False)
        consume(get(sb_ref), n_full, True)

    def tail_even():
        consume(get(sa_ref), n_full, True)

    put(sa_ref, scores(0))
    lax.fori_loop(0, n_full // 2, pair, 0)
    if isinstance(n_full, int):
        (tail_odd if n_full % 2 else tail_even)()
    else:
        pl.when(n_full % 2 == 1)(tail_odd)
        pl.when(n_full % 2 == 0)(tail_even)


def _fox_kernel(*refs, tq, tk, q_off, n_q, has_cache):
    if has_cache:
        (q_ref, k_ref, v_ref, kc_ref, vc_ref, cq_ref, ck_ref, o_ref, kt_ref, vb_ref, m_ref, acc_ref,
         sa_ref, sb_ref) = refs
        k_parts, v_parts = (kc_ref, k_ref), (vc_ref, v_ref)
    else:
        q_ref, k_ref, v_ref, cq_ref, ck_ref, o_ref, kt_ref, vb_ref, m_ref, acc_ref, sa_ref, sb_ref = refs
        k_parts, v_parts = (k_ref,), (v_ref,)
    h = pl.program_id(1)
    qi = pl.program_id(2)

    @pl.when(qi == 0)
    def _():
        for blk in range(kt_ref.shape[0]):
            r0 = blk * tk
            kt_ref[blk, 0:HEAD_DIM, :] = _rows_of(k_parts, r0, tk).T.astype(BF16)
            kt_ref[blk, HEAD_DIM:, :] = _bias_rows(-LOG2E * ck_ref[0, h, blk])
            vb_ref[r0:r0 + tk, 0:HEAD_DIM] = _rows_of(v_parts, r0, tk).astype(BF16)
            vb_ref[r0:r0 + tk, HEAD_DIM:] = _ones_col(tk)

    qpos0 = q_off if n_q == 1 else q_off + qi * tq
    cq_row = LOG2E * cq_ref[0, h, q_off // tq + qi]
    eye = lax.broadcasted_iota(I32, (tq, tq), 0) == lax.broadcasted_iota(I32, (tq, tq), 1)
    cq = jnp.sum(jnp.where(eye, cq_row, 0.0), axis=1, keepdims=True)
    qx = jnp.concatenate([q_ref[...], _bias_cols(cq)], axis=1)

    n_split, rs = m_ref.shape[0], m_ref.shape[1]
    qxs = [qx[i * rs:(i + 1) * rs, :] for i in range(n_split)]
    row = lax.broadcasted_iota(I32, (rs, tk), 0)
    col = lax.broadcasted_iota(I32, (rs, tk), 1)

    def scores(kv):
        kt = kt_ref[kv]
        return tuple(jnp.dot(x, kt, preferred_element_type=F32) for x in qxs)

    def consume(ss, kv, masked):
        k0 = pl.multiple_of(kv * tk, tk)
        vblk = vb_ref[pl.ds(k0, tk), :]
        for i, s in enumerate(ss):
            if masked:
                s = jnp.where(kv * tk + col <= qpos0 + i * rs + row, s, NEG_INF)
            m = m_ref[i]
            m_new = jnp.maximum(m, jnp.max(s, axis=1, keepdims=True))
            p = jnp.exp2(s - m_new).astype(BF16)
            pv = jnp.dot(p, vblk, preferred_element_type=F32)
            acc_ref[i] = jnp.exp2(m - m_new) * acc_ref[i] + pv
            m_ref[i] = m_new

    m_ref[...] = jnp.full(m_ref.shape, NEG_INF, F32)
    acc_ref[...] = jnp.zeros(acc_ref.shape, F32)
    n_full = qpos0 // tk

    _run_blocks(n_full, scores, consume, sa_ref, sb_ref)
    for i in range(n_split):
        acc = acc_ref[i]
        o_ref[i * rs:(i + 1) * rs, :] = (acc[:, :HEAD_DIM] / acc[:, HEAD_DIM:HEAD_DIM + 1]).astype(o_ref.dtype)


def fox_attention(q, k, v, c, n_batch, t_q, t_new, q_off, tq, tk, cache=None):
    n_q = t_q // tq
    tc = c.shape[2]
    assert t_q % tq == 0 and q_off % tq == 0 and tc % tq == 0 and tc % tk == 0
    assert (q_off % tk) + tq <= tk if n_q == 1 else (tq == tk and q_off % tk == 0)
    c_q = c.reshape(n_batch, H_FOX, tc // tq, 1, tq)
    c_k = c.reshape(n_batch, H_FOX, tc // tk, 1, tk)
    n_split = 2 if tq % 256 == 0 else 1
    (q, qg), (k, kg), (v, vg) = q, k, v
    in_specs = [pl.BlockSpec((None, tq, HEAD_DIM), lambda b, h, i: (qg, b * n_q + i, h)),
                pl.BlockSpec((None, t_new, HEAD_DIM), lambda b, h, i: (kg, b, h)),
                pl.BlockSpec((None, t_new, HEAD_DIM), lambda b, h, i: (vg, b, h))]
    args = [q, k, v]
    if cache is not None:
        past = cache[0].shape[1]
        assert past + t_new <= tc
        old = pl.BlockSpec((past, HEAD_DIM), lambda b, h, i: (b, h))
        in_specs += [old, old]
        args += [x.reshape(n_batch * past, H_FOX * HEAD_DIM) for x in cache]
    else:
        assert t_new == tc
    in_specs += [pl.BlockSpec((1, H_FOX, tc // tq, 1, tq), lambda b, h, i: (b, 0, 0, 0, 0)),
                 pl.BlockSpec((1, H_FOX, tc // tk, 1, tk), lambda b, h, i: (b, 0, 0, 0, 0))]
    return pl.pallas_call(
        functools.partial(_fox_kernel, tq=tq, tk=tk, q_off=q_off, n_q=n_q, has_cache=cache is not None),
        grid=(n_batch, H_FOX, n_q),
        in_specs=in_specs,
        out_specs=pl.BlockSpec((tq, HEAD_DIM), lambda b, h, i: (b * n_q + i, h)),
        out_shape=jax.ShapeDtypeStruct((n_batch * t_q, H_FOX * HEAD_DIM), BF16),
        scratch_shapes=[pltpu.VMEM((tc // tk, 2 * HEAD_DIM, tk), BF16), pltpu.VMEM((tc, 2 * HEAD_DIM), BF16),
                        pltpu.VMEM((n_split, tq // n_split, 1), F32),
                        pltpu.VMEM((n_split, tq // n_split, 2 * HEAD_DIM), F32),
                        pltpu.VMEM((n_split, tq // n_split, tk), F32), pltpu.VMEM((n_split, tq // n_split, tk), F32)],
        compiler_params=_params(("parallel", "parallel", "arbitrary")),
        name="fox_attention",
    )(*args, c_q, c_k)


def _diff_kernel(slope_ref, q_ref, k_ref, v_ref, *refs, tq, tk, q_off, n_q, n_valid, has_cache):
    if has_cache:
        kc0_ref, kc1_ref, vc_ref = refs[:3]
        refs = refs[3:]
    (lq1_ref, lk1_ref, lq2_ref, lk2_ref, sg_ref, o_ref, kt_ref, vb_ref, m_ref, l_ref, acc_ref,
     sa_ref, sb_ref) = refs
    h = pl.program_id(1)
    qi = pl.program_id(2)
    dv = 2 * HEAD_DIM
    slope2 = slope_ref[h] * LOG2E

    @pl.when(qi == 0)
    def _():
        v_parts = (vc_ref, v_ref) if has_cache else (v_ref,)
        for blk in range(kt_ref.shape[1]):
            r0 = blk * tk
            kpos_row = (r0 + lax.broadcasted_iota(I32, (1, tk), 1)).astype(F32)
            key_bias = _bias_rows(slope2 * kpos_row)
            for mp in range(2):
                new_rows = k_ref.at[:, mp * HEAD_DIM:(mp + 1) * HEAD_DIM]
                k_parts = ((kc0_ref, kc1_ref)[mp], new_rows) if has_cache else (new_rows,)
                kt_ref[mp, blk, 0:HEAD_DIM, :] = _rows_of(k_parts, r0, tk).T.astype(BF16)
                kt_ref[mp, blk, HEAD_DIM:, :] = key_bias
            vb_ref[r0:r0 + tk, :] = _rows_of(v_parts, r0, tk).astype(BF16)

    lam = (jnp.exp(jnp.sum(lq1_ref[...] * lk1_ref[...], axis=1, keepdims=True))
           - jnp.exp(jnp.sum(lq2_ref[...] * lk2_ref[...], axis=1, keepdims=True)) + LAM_INIT_L0)
    qpos0 = q_off if n_q == 1 else q_off + qi * tq
    qpos_col = (qpos0 + lax.broadcasted_iota(I32, (tq, 1), 0)).astype(F32)
    query_bias = _bias_cols(-slope2 * qpos_col)
    q = q_ref[...]
    qx = [jnp.concatenate([q[:, mp * HEAD_DIM:(mp + 1) * HEAD_DIM], query_bias], axis=1) for mp in range(2)]
    row = lax.broadcasted_iota(I32, (tq, tk), 0)
    col = lax.broadcasted_iota(I32, (tq, tk), 1)

    def scores(kv):
        return tuple(jnp.dot(qx[mp], kt_ref[mp, kv], preferred_element_type=F32) for mp in range(2))

    def consume(ss, kv, masked):
        k0 = pl.multiple_of(kv * tk, tk)
        vblk = vb_ref[pl.ds(k0, tk), :]
        if masked:
            qpos = qpos0 + row
            kpos = kv * tk + col
            visible = ((kpos >> 6) <= (qpos >> 6)) & (kpos < n_valid)
            fix = (2.0 * slope2) * jnp.maximum(kpos - qpos, 0).astype(F32)
        for mp in range(2):
            s = ss[mp]
            if masked:
                s = jnp.where(visible, s - fix, NEG_INF)
            m = m_ref[mp]
            m_new = jnp.maximum(m, jnp.max(s, axis=1, keepdims=True))
            alpha = jnp.exp2(m - m_new)
            p = jnp.exp2(s - m_new)
            l_ref[mp] = alpha * l_ref[mp] + jnp.sum(p, axis=1, keepdims=True)
            acc_ref[mp] = alpha * acc_ref[mp] + jnp.dot(p.astype(BF16), vblk, preferred_element_type=F32)
            m_ref[mp] = m_new

    m_ref[...] = jnp.full(m_ref.shape, NEG_INF, F32)
    l_ref[...] = jnp.zeros(l_ref.shape, F32)
    acc_ref[...] = jnp.zeros(acc_ref.shape, F32)
    n_full = qpos0 // tk

    _run_blocks(n_full, scores, consume, sa_ref, sb_ref)
    o = acc_ref[0] / l_ref[0] - lam * (acc_ref[1] / l_ref[1])
    o_ref[...] = (_rms(o, sg_ref[...]) * (1.0 - LAM_INIT_L0)).astype(o_ref.dtype)


def diff_attention(q, k, v, lam_vecs, subln_g, n_batch, t_q, t_new, t_s, q_off, tq, tk, cache=None):
    n_q = t_q // tq
    assert t_q % tq == 0 and t_s % tk == 0
    dv = 2 * HEAD_DIM
    slopes = 2.0 ** (-8.0 * jnp.arange(1, H_DIFF + 1, dtype=F32) / H_DIFF)
    vec = pl.BlockSpec((1, HEAD_DIM), lambda b, h, i, s: (0, 0))
    (q, qg), (k, kg), (v, vg) = q, k, v
    in_specs = [pl.BlockSpec((None, tq, dv), lambda b, h, i, s: (qg, b * n_q + i, h)),
                pl.BlockSpec((None, t_new, dv), lambda b, h, i, s: (kg, b, h)),
                pl.BlockSpec((None, t_new, dv), lambda b, h, i, s: (vg, b, h))]
    args = [slopes, q, k, v]
    past = 0
    if cache is not None:
        past = cache[0].shape[1]
        in_specs += [
            pl.BlockSpec((past, HEAD_DIM), lambda b, h, i, s: (b, 2 * h)),
            pl.BlockSpec((past, HEAD_DIM), lambda b, h, i, s: (b, 2 * h + 1)),
            pl.BlockSpec((past, dv), lambda b, h, i, s: (b, h)),
        ]
        ck2d = cache[0].reshape(n_batch * past, H_DIFF * dv)
        args += [ck2d, ck2d, cache[1].reshape(n_batch * past, H_DIFF * dv)]
    n_valid = past + t_new
    assert n_valid <= t_s and tq % CHUNK == 0 or n_q == 1
    if n_q == 1:
        last_visible = min(((q_off + tq - 1) // CHUNK + 1) * CHUNK, n_valid)
        assert (q_off // tk) * tk <= n_valid and last_visible <= (q_off // tk + 1) * tk
    else:
        assert tq == tk and q_off % tk == 0
    in_specs += [vec, vec, vec, vec, pl.BlockSpec((1, dv), lambda b, h, i, s: (0, 0))]
    grid_spec = pltpu.PrefetchScalarGridSpec(
        num_scalar_prefetch=1,
        grid=(n_batch, H_DIFF, n_q),
        in_specs=in_specs,
        out_specs=pl.BlockSpec((tq, dv), lambda b, h, i, s: (b * n_q + i, h)),
        scratch_shapes=[pltpu.VMEM((2, t_s // tk, dv, tk), BF16), pltpu.VMEM((t_s, dv), BF16),
                        pltpu.VMEM((2, tq, 1), F32), pltpu.VMEM((2, tq, 1), F32), pltpu.VMEM((2, tq, dv), F32),
                        pltpu.VMEM((2, tq, tk), F32), pltpu.VMEM((2, tq, tk), F32)],
    )
    return pl.pallas_call(
        functools.partial(_diff_kernel, tq=tq, tk=tk, q_off=q_off, n_q=n_q, n_valid=n_valid,
                          has_cache=cache is not None),
        grid_spec=grid_spec,
        out_shape=jax.ShapeDtypeStruct((n_batch * t_q, H_DIFF * dv), BF16),
        compiler_params=_params(("parallel", "parallel", "arbitrary")),
        name="diff_attention",
    )(*args, *[x.reshape(1, HEAD_DIM) for x in lam_vecs], subln_g.reshape(1, dv))


REL_PAD = 384


def _band_bias_kernel(t_ref, o_ref, *, nk, n_valid):
    blk = pl.program_id(0)
    row = lax.broadcasted_iota(I32, (REL_PAD, nk), 0)
    j = lax.broadcasted_iota(I32, (REL_PAD, nk), 1)
    jc = lax.broadcasted_iota(I32, (1, nk), 1) >> 6
    jv = lax.broadcasted_iota(I32, (1, nk), 1) < n_valid
    for r in range(8):
        i = blk * 8 + r
        idx = jnp.clip(i - j + BAND_PAST, -REL_CLIP, REL_CLIP) + REL_CLIP
        onehot = jnp.where(row == idx, 1.0, 0.0)
        bias = jnp.dot(t_ref[...], onehot, preferred_element_type=F32, precision=lax.Precision.HIGHEST)
        ic = i >> 6
        visible = (jc - N_PREV_CHUNKS <= ic) & (jc >= ic) & jv
        o_ref[:, r, :] = jnp.where(visible, bias * LOG2E, NEG_INF)


def band_bias(table_padded, tq, nk, n_valid):
    assert CHUNK == 64
    return pl.pallas_call(
        functools.partial(_band_bias_kernel, nk=nk, n_valid=n_valid),
        grid=(tq // 8,),
        in_specs=[pl.BlockSpec((H_BAND, REL_PAD), lambda i: (0, 0))],
        out_specs=pl.BlockSpec((H_BAND, 8, nk), lambda i: (0, i, 0)),
        out_shape=jax.ShapeDtypeStruct((H_BAND, tq, nk), F32),
        compiler_params=_params(("parallel",)),
        name="band_bias",
    )(table_padded)


def _band_kernel(*refs, tq, nk, n_g, q_off, has_cache):
    if has_cache:
        q_ref, k_ref, v_ref, kc_ref, vc_ref, b_ref, o_ref, kb_ref, vb_ref, sa_ref, sb_ref = refs
        k_parts, v_parts = (kc_ref, k_ref), (vc_ref, v_ref)
        front = 0
    else:
        q_ref, k_ref, v_ref, b_ref, o_ref, kb_ref, vb_ref, sa_ref, sb_ref = refs
        k_parts, v_parts = (k_ref,), (v_ref,)
        front = kb_ref.shape[0] - k_ref.shape[0]
        kb_ref[0:front, :] = jnp.zeros((front, HEAD_DIM), BF16)
        vb_ref[0:front, 0:HEAD_DIM] = jnp.zeros((front, HEAD_DIM), BF16)
    t_s = kb_ref.shape[0]
    kb_ref[front:, :] = _rows_of(k_parts, 0, t_s - front).astype(BF16)
    vb_ref[front:, 0:HEAD_DIM] = _rows_of(v_parts, 0, t_s - front).astype(BF16)
    vb_ref[:, HEAD_DIM:] = _ones_col(t_s)
    jrow = lax.broadcasted_iota(I32, (1, nk), 1)

    def row0(g):
        return g * tq if isinstance(g, int) else pl.multiple_of(g * tq, tq)

    def scores(g):
        r0 = row0(g)
        return lax.dot_general(q_ref[pl.ds(r0, tq), :], kb_ref[pl.ds(r0, nk), :], (((1,), (1,)), ((), ())),
                               preferred_element_type=F32)

    def consume(s, g, clip_start):
        r0 = row0(g)
        s = s + b_ref[0]
        if clip_start:
            s = s + jnp.where(jrow >= BAND_PAST - (q_off + g * tq), 0.0, NEG_INF)
        m = jnp.max(s, axis=1, keepdims=True)
        p = jnp.exp2(s - m).astype(BF16)
        o = jnp.dot(p, vb_ref[pl.ds(r0, nk), :], preferred_element_type=F32)
        o_ref[pl.ds(r0, tq), :] = (o[:, :HEAD_DIM] / o[:, HEAD_DIM:HEAD_DIM + 1]).astype(o_ref.dtype)

    n_clip = min(n_g, max(0, -(-(BAND_PAST - q_off) // tq)))
    for g in range(n_clip):
        consume(scores(g), g, True)
    n_rest = n_g - n_clip
    if n_rest == 1:
        consume(scores(n_clip), n_clip, False)
    elif n_rest > 1:
        assert n_rest % 2 == 0
        sa_ref[...] = scores(n_clip)

        def pair(i, carry):
            g = n_clip + 2 * i
            sb_ref[...] = scores(g + 1)
            consume(sa_ref[...], g, False)
            sa_ref[...] = scores(jnp.minimum(g + 2, n_g - 1))
            consume(sb_ref[...], g + 1, False)
            return carry

        lax.fori_loop(0, n_rest // 2, pair, 0)


def band_attention(q, k, v, bias, n_batch, t_q, t_new, q_off, tq, cache=None):
    n_g = t_q // tq
    nk = bias.shape[2]
    t_s = (n_g - 1) * tq + nk
    rows = pl.BlockSpec((t_q, HEAD_DIM), lambda b, h: (b, h))
    (q, qg), (k, kg), (v, vg) = q, k, v
    in_specs = [pl.BlockSpec((None, t_q, HEAD_DIM), lambda b, h: (qg, b, h)),
                pl.BlockSpec((None, t_new, HEAD_DIM), lambda b, h: (kg, b, h)),
                pl.BlockSpec((None, t_new, HEAD_DIM), lambda b, h: (vg, b, h))]
    args = [q, k, v]
    if cache is not None:
        assert cache[0].shape[1] == BAND_PAST and BAND_PAST + t_new <= t_s and n_g == 1
        old = pl.BlockSpec((BAND_PAST, HEAD_DIM), lambda b, h: (b, h))
        in_specs += [old, old]
        args += [x.reshape(n_batch * BAND_PAST, H_BAND * HEAD_DIM) for x in cache]
    else:
        assert BAND_PAST + t_new == t_s
    in_specs.append(pl.BlockSpec((1, tq, nk), lambda b, h: (h, 0, 0)))
    return pl.pallas_call(
        functools.partial(_band_kernel, tq=tq, nk=nk, n_g=n_g, q_off=q_off, has_cache=cache is not None),
        grid=(n_batch, H_BAND),
        in_specs=in_specs,
        out_specs=rows,
        out_shape=jax.ShapeDtypeStruct((n_batch * t_q, H_BAND * HEAD_DIM), BF16),
        scratch_shapes=[pltpu.VMEM((t_s, HEAD_DIM), BF16), pltpu.VMEM((t_s, 2 * HEAD_DIM), BF16),
                        pltpu.VMEM((tq, nk), F32), pltpu.VMEM((tq, nk), F32)],
        compiler_params=_params(("parallel", "parallel")),
        name="band_attention",
    )(*args, bias)


def _even_layer(h, n_batch, t, caches, w, sample):
    (g_mix, w_q, w_kv, w_f, b_f, lam_vecs, subln_g, w_out_a, w_out_b, g_ffn, wg, wu, wd) = w
    hw = H_FOX * HEAD_DIM
    tm = 512 if not sample else h.shape[0]
    tm_in = 1024 if not sample else tm
    q, xn, logf = norm_proj(h, g_mix, w_q, hw, Q_SCALE, wf=w_f, bf=b_f, tm=tm_in)
    kv = proj(xn, w_kv, tm=tm_in)
    fk, fv, dk, dv = [(x[None], 0) for x in kv]
    logf_t = jnp.swapaxes(logf.reshape(n_batch, t, H_FOX), 1, 2)
    if not sample:
        c = cumsum_time(logf_t, 512)
        oa = fox_attention((q, 0), fk, fv, c, n_batch, t, t, 0, 512, 512)
        ob = diff_attention((q, 1), dk, dv, lam_vecs, subln_g, n_batch, t, t, t, 0, 512, 512)
    else:
        cfk, cfv, cfl, cdk, cdv = caches
        past = cfk.shape[1]
        t_k = past + t
        t_pad = -(-t_k // 384) * 384
        lf = jnp.concatenate([jnp.swapaxes(cfl, 1, 2), logf_t], axis=2)
        c = cumsum_time(jnp.pad(lf, ((0, 0), (0, 0), (0, t_pad - t_k))), 384)
        oa = fox_attention((q, 0), fk, fv, c, n_batch, t, t, past, t, 384, cache=(cfk, cfv))
        ob = diff_attention((q, 1), dk, dv, lam_vecs, subln_g, n_batch, t, t, t_pad, past, t, 384,
                            cache=(cdk, cdv))
    h = proj_residual([oa, ob], [w_out_a, w_out_b], h, tm=tm)
    h = ffn_dense(h, g_ffn, wg, wu, wd, tm=tm)
    return h, (kv[0], kv[1], logf, kv[2], kv[3])


def _odd_attention(h, n_batch, t, caches, w, sample):
    (g_mix, w_q, w_kv, table, w_out) = w
    d = D_MODEL
    tm = 512 if not sample else h.shape[0]
    tm_in = 1024 if not sample else tm
    q, xn = norm_proj(h, g_mix, w_q, d, Q_SCALE, tm=tm_in)
    kv = proj(xn, w_kv, tm=tm_in)
    bk, bv = [(x[None], 0) for x in kv]
    if not sample:
        tq = 2 * CHUNK
        bias = band_bias(table, tq, BAND_PAST + tq, BAND_PAST + tq)
        o = band_attention((q, 0), bk, bv, bias, n_batch, t, t, 0, tq)
    else:
        cbk, cbv, past_len = caches
        pc = cbk.shape[1]
        assert pc == BAND_PAST and t <= CHUNK and past_len % CHUNK == 0
        nk = -(-(pc + t) // LANES) * LANES
        bias = band_bias(table, t, nk, pc + t)
        o = band_attention((q, 0), bk, bv, bias, n_batch, t, t, past_len, t, cache=(cbk, cbv))
    h = proj_residual([o], [w_out], h, tm=tm)
    return h, (kv[0], kv[1])


def kernel(x_prompt, x_sample, cache_fox_k, cache_fox_v, cache_fox_logf, cache_diff_k, cache_diff_v, cache_band_k, cache_band_v, norm_mix_even, w_in_even, b_forget, lam_q1, lam_k1, lam_q2, lam_k2, subln_g, w_out_even, norm_ffn_even, w_gate, w_up, w_down, norm_mix_odd, w_in_odd, rel_bias, w_out_odd, norm_ffn_odd, w_router, w_gate_e, w_up_e, w_down_e, norm_final):
    bp, tp, d = x_prompt.shape
    bs, ts, _ = x_sample.shape
    past = cache_fox_k.shape[2]
    hw = H_FOX * HEAD_DIM
    fox_w = 3 * hw + H_FOX

    w_in0 = w_in_even[0]
    w_q0 = jnp.concatenate([w_in0[:, :hw], w_in0[:, fox_w:fox_w + hw]], axis=1).astype(BF16)
    w_kv0 = jnp.stack([w_in0[:, hw:2 * hw], w_in0[:, 2 * hw:3 * hw], w_in0[:, fox_w + hw:fox_w + 2 * hw],
                       w_in0[:, fox_w + 2 * hw:]]).astype(BF16)
    w_f = jnp.pad(w_in0[:, 3 * hw:fox_w], ((0, 0), (0, LANES - H_FOX))).astype(BF16)
    b_f = jnp.pad(b_forget[0], (0, LANES - H_FOX)).reshape(1, LANES)
    lam_vecs = (lam_q1[0], lam_k1[0], lam_q2[0], lam_k2[0])
    w_out0 = w_out_even[0].astype(BF16)
    even_w = (norm_mix_even[0], w_q0, w_kv0, w_f, b_f, lam_vecs, subln_g[0], w_out0[:hw], w_out0[hw:],
              norm_ffn_even[0], w_gate[0].astype(BF16), w_up[0].astype(BF16), w_down[0].astype(BF16))
    table = jnp.pad(rel_bias[0], ((0, 0), (0, REL_PAD - rel_bias.shape[2])))
    w_in1 = w_in_odd[0].astype(BF16)
    odd_w = (norm_mix_odd[0], w_in1[:, :d], jnp.stack([w_in1[:, d:2 * d], w_in1[:, 2 * d:]]), table,
             w_out_odd[0].astype(BF16))
    wr = jnp.pad(w_router[0], ((0, 0), (0, LANES - N_EXPERTS)))

    mp, ms = bp * tp, bs * ts
    h_p = x_prompt.reshape(mp, d)
    h_s = x_sample.reshape(ms, d)

    h_p, (fk_p, fv_p, fl_p, dk_p, dv_p) = _even_layer(h_p, bp, tp, None, even_w, False)
    caches_even = (cache_fox_k[0], cache_fox_v[0], cache_fox_logf[0], cache_diff_k[0], cache_diff_v[0])
    h_s, (fk_s, fv_s, fl_s, dk_s, dv_s) = _even_layer(h_s, bs, ts, caches_even, even_w, True)

    h_p, (bk_p, bv_p) = _odd_attention(h_p, bp, tp, None, odd_w, False)
    h_s, (bk_s, bv_s) = _odd_attention(h_s, bs, ts, (cache_band_k[0], cache_band_v[0], past), odd_w, True)

    m_all = mp + ms
    assert mp % 512 == 0 and mp % ms == 0
    zero_counts = jnp.zeros((1, LANES), F32)
    idx_p, gate_p, cnt_p, u_p = router(h_p, norm_ffn_odd[0], wr, zero_counts, 512)
    idx_s, gate_s, cnt, u_s = router(h_s, norm_ffn_odd[0], wr, cnt_p, ms)
    idx = jnp.concatenate([idx_p, idx_s], axis=0)
    u_all = jnp.concatenate([u_p, u_s], axis=0)
    counts = cnt[0, :N_EXPERTS].astype(I32)
    n_tiles_e = (counts + MOE_TILE - 1) // MOE_TILE
    tile_end = jnp.cumsum(n_tiles_e)
    tile_start = tile_end - n_tiles_e
    row_off = tile_start * MOE_TILE
    n_used = tile_end[-1:]
    nt = (2 * m_all) // MOE_TILE + N_EXPERTS
    tile_ids = jnp.arange(nt, dtype=I32)
    last = jnp.maximum(n_used[0] - 1, 0)
    tile_expert = jnp.sum((jnp.minimum(tile_ids, last)[:, None] >= tile_end[None, :]).astype(I32), axis=1)
    tile_expert = jnp.minimum(tile_expert, N_EXPERTS - 1)
    tile_rows = jnp.clip(counts[tile_expert] - (tile_ids - tile_start[tile_expert]) * MOE_TILE, 0, MOE_TILE)
    pos1 = row_off[idx[:, 0]] + idx[:, 2]
    pos2 = row_off[idx[:, 1]] + idx[:, 3]
    tokens = jnp.arange(m_all, dtype=I32)
    row_token = jnp.zeros((nt * MOE_TILE,), I32).at[pos1].set(tokens).at[pos2].set(tokens)
    y = ffn_moe(tile_expert, n_used.astype(I32), tile_rows.astype(I32), row_token, u_all,
                w_gate_e[0], w_up_e[0], w_down_e[0])
    y1 = jnp.take(y, pos1, axis=0, mode="clip")
    y2 = jnp.take(y, pos2, axis=0, mode="clip")
    y_p = combine_norm(h_p, y1, y2, gate_p, norm_final, 512, 0)
    y_s = combine_norm(h_s, y1, y2, gate_s, norm_final, ms, mp)

    keep = min(BAND_PAST, tp)
    bk_p = bk_p.reshape(bp, tp, H_BAND, HEAD_DIM)[:, tp - keep:][None]
    bv_p = bv_p.reshape(bp, tp, H_BAND, HEAD_DIM)[:, tp - keep:][None]
    return (
        y_p.reshape(bp, tp, d), y_s.reshape(bs, ts, d),
        fk_p.reshape(1, bp, tp, H_FOX, HEAD_DIM), fk_s.reshape(1, bs, ts, H_FOX, HEAD_DIM),
        fv_p.reshape(1, bp, tp, H_FOX, HEAD_DIM), fv_s.reshape(1, bs, ts, H_FOX, HEAD_DIM),
        fl_p.reshape(1, bp, tp, H_FOX), fl_s.reshape(1, bs, ts, H_FOX),
        dk_p.reshape(1, bp, tp, H_DIFF, 2, HEAD_DIM), dk_s.reshape(1, bs, ts, H_DIFF, 2, HEAD_DIM),
        dv_p.reshape(1, bp, tp, H_DIFF, 2 * HEAD_DIM), dv_s.reshape(1, bs, ts, H_DIFF, 2 * HEAD_DIM),
        bk_p, bk_s.reshape(1, bs, ts, H_BAND, HEAD_DIM),
        bv_p, bv_s.reshape(1, bs, ts, H_BAND, HEAD_DIM),
    )
```

```python
import functools
import math

import jax
import jax.numpy as jnp
from jax import lax
from jax.experimental import pallas as pl
from jax.experimental.pallas import tpu as pltpu

F32 = jnp.float32
BF16 = jnp.bfloat16
I32 = jnp.int32

D_MODEL = 2048
HEAD_DIM = 128
CHUNK = 64
H_FOX = 8
H_DIFF = 4
H_BAND = 16
N_PREV_CHUNKS = 8
BAND_PAST = N_PREV_CHUNKS * CHUNK
REL_CLIP = 128
D_FF = 5632
N_EXPERTS = 8
EPS = 1e-6
SCALE = HEAD_DIM ** -0.5
LAM_INIT_L0 = 0.8 - 0.6 * math.exp(-0.3 * 0)
NEG_INF = float("-inf")
LOG2E = 1.4426950408889634
Q_SCALE = SCALE * LOG2E

LANES = 128
VMEM_LIMIT = 56 * 1024 * 1024
MOE_TILE = 1024


def _params(sem):
    return pltpu.CompilerParams(dimension_semantics=sem, vmem_limit_bytes=VMEM_LIMIT)


def _rms(x, g):
    return (x * lax.rsqrt(jnp.mean(x * x, axis=-1, keepdims=True) + EPS)) * g


def _log_sigmoid(x):
    return jnp.minimum(x, 0.0) - jnp.log1p(jnp.exp(-jnp.abs(x)))


def _norm_proj_kernel(*refs, has_forget, scale):
    if has_forget:
        x_ref, g_ref, w_ref, wf_ref, bf_ref, o_ref, xn_ref, logf_ref = refs
    else:
        x_ref, g_ref, w_ref, o_ref, xn_ref = refs

    @pl.when(pl.program_id(1) == 0)
    def _():
        xn = _rms(x_ref[...], g_ref[...]).astype(BF16)
        xn_ref[...] = xn
        if has_forget:
            fa = jnp.dot(xn, wf_ref[...], preferred_element_type=F32)
            logf_ref[...] = _log_sigmoid(fa + bf_ref[...])[:, :H_FOX]

    z = jnp.dot(xn_ref[...], w_ref[...], preferred_element_type=F32)
    o_ref[...] = (z * scale).astype(o_ref.dtype)


def norm_proj(x, g, w, group_width, scale, wf=None, bf=None, tm=512, tn=512):
    m, d = x.shape
    n = w.shape[1]
    assert n % group_width == 0 and group_width % tn == 0 and m % tm == 0
    nb = group_width // tn
    has_forget = wf is not None
    in_specs = [
        pl.BlockSpec((tm, d), lambda i, j: (i, 0)),
        pl.BlockSpec((1, d), lambda i, j: (0, 0)),
        pl.BlockSpec((d, tn), lambda i, j: (0, j)),
    ]
    args = [x, g.reshape(1, d), w]
    out_specs = [pl.BlockSpec((None, tm, tn), lambda i, j: (j // nb, i, j % nb)),
                 pl.BlockSpec((tm, d), lambda i, j: (i, 0))]
    out_shape = [jax.ShapeDtypeStruct((n // group_width, m, group_width), BF16),
                 jax.ShapeDtypeStruct((m, d), BF16)]
    if has_forget:
        in_specs += [pl.BlockSpec((d, LANES), lambda i, j: (0, 0)),
                     pl.BlockSpec((1, LANES), lambda i, j: (0, 0))]
        args += [wf, bf]
        out_specs.append(pl.BlockSpec((tm, H_FOX), lambda i, j: (i, 0)))
        out_shape.append(jax.ShapeDtypeStruct((m, H_FOX), F32))
    return pl.pallas_call(
        functools.partial(_norm_proj_kernel, has_forget=has_forget, scale=float(scale)),
        grid=(m // tm, n // tn),
        in_specs=in_specs,
        out_specs=out_specs,
        out_shape=out_shape,
        compiler_params=_params(("parallel", "arbitrary")),
        name="norm_proj",
    )(*args)


def _proj_kernel(x_ref, w_ref, *o_refs):
    x = x_ref[...]
    for g, o_ref in enumerate(o_refs):
        o_ref[...] = jnp.dot(x, w_ref[g], preferred_element_type=F32)


def proj(x, w, tm=512, tn=256):
    m, d = x.shape
    n_groups, _, gw = w.shape
    assert gw % tn == 0 and m % tm == 0
    return pl.pallas_call(
        _proj_kernel,
        grid=(m // tm, gw // tn),
        in_specs=[pl.BlockSpec((tm, d), lambda i, j: (i, 0)),
                  pl.BlockSpec((n_groups, d, tn), lambda i, j: (0, 0, j))],
        out_specs=[pl.BlockSpec((tm, tn), lambda i, j: (i, j))] * n_groups,
        out_shape=[jax.ShapeDtypeStruct((m, gw), F32)] * n_groups,
        compiler_params=_params(("parallel", "arbitrary")),
        name="proj",
    )(x, w)


def _proj_res_kernel(*refs, n_in):
    xs = refs[:n_in]
    ws = refs[n_in:2 * n_in]
    h_ref = refs[2 * n_in]
    o_ref = refs[2 * n_in + 1]
    acc = h_ref[...]
    for x_ref, w_ref in zip(xs, ws):
        acc = acc + jnp.dot(x_ref[...], w_ref[...], preferred_element_type=F32)
    o_ref[...] = acc


def proj_residual(xs, ws, h, tm=512, tn=2048):
    m, n = h.shape
    assert m % tm == 0 and n % tn == 0
    n_in = len(xs)
    in_specs = [pl.BlockSpec((tm, x.shape[1]), lambda i, j: (i, 0)) for x in xs]
    in_specs += [pl.BlockSpec((w.shape[0], tn), lambda i, j: (0, j)) for w in ws]
    in_specs.append(pl.BlockSpec((tm, tn), lambda i, j: (i, j)))
    return pl.pallas_call(
        functools.partial(_proj_res_kernel, n_in=n_in),
        grid=(m // tm, n // tn),
        in_specs=in_specs,
        out_specs=pl.BlockSpec((tm, tn), lambda i, j: (i, j)),
        out_shape=jax.ShapeDtypeStruct((m, n), F32),
        compiler_params=_params(("parallel", "arbitrary")),
        name="proj_residual",
    )(*xs, *ws, h)


def _swiglu_step(x, wg_ref, wu_ref, wd_ref):
    a = jnp.dot(x, wg_ref[...], preferred_element_type=F32)
    b = jnp.dot(x, wu_ref[...], preferred_element_type=F32)
    mid = (a * jax.nn.sigmoid(a)) * b
    return jnp.dot(mid.astype(BF16), wd_ref[...], preferred_element_type=F32)


def _ffn_dense_kernel(h_ref, g_ref, wg_ref, wu_ref, wd_ref, o_ref, xn_ref, acc_ref):
    j = pl.program_id(1)

    @pl.when(j == 0)
    def _():
        h = h_ref[...]
        xn_ref[...] = _rms(h, g_ref[...]).astype(BF16)
        acc_ref[...] = h

    acc_ref[...] += _swiglu_step(xn_ref[...], wg_ref, wu_ref, wd_ref)

    @pl.when(j == pl.num_programs(1) - 1)
    def _():
        o_ref[...] = acc_ref[...]


def ffn_dense(h, g, wg, wu, wd, tm=512, tf=512):
    m, d = h.shape
    f = wg.shape[1]
    assert m % tm == 0 and f % tf == 0
    return pl.pallas_call(
        _ffn_dense_kernel,
        grid=(m // tm, f // tf),
        in_specs=[
            pl.BlockSpec((tm, d), lambda i, j: (i, 0)),
            pl.BlockSpec((1, d), lambda i, j: (0, 0)),
            pl.BlockSpec((d, tf), lambda i, j: (0, j)),
            pl.BlockSpec((d, tf), lambda i, j: (0, j)),
            pl.BlockSpec((tf, d), lambda i, j: (j, 0)),
        ],
        out_specs=pl.BlockSpec((tm, d), lambda i, j: (i, 0)),
        out_shape=jax.ShapeDtypeStruct((m, d), F32),
        scratch_shapes=[pltpu.VMEM((tm, d), BF16), pltpu.VMEM((tm, d), F32)],
        compiler_params=_params(("parallel", "arbitrary")),
        name="ffn_dense",
    )(h, g.reshape(1, d), wg, wu, wd)


GATHER_STEPS = 16


def _row_copy(u_hbm, xbuf_ref, sem_ref, src_row, dst_row):
    return pltpu.make_async_copy(u_hbm.at[pl.ds(src_row, 1)], xbuf_ref.at[pl.ds(dst_row, 1)], sem_ref.at[0])


def _ffn_moe_kernel(te_ref, nu_ref, tr_ref, tok_ref, u_hbm, wg_ref, wu_ref, wd_ref, o_ref,
                    mid_ref, xbuf_ref, x_ref, sem_ref, *, nj_a):
    t = pl.program_id(0)
    j = pl.program_id(1)
    half = MOE_TILE // 2
    per_step = MOE_TILE // GATHER_STEPS

    def issue(tile, first_row, n_rows):
        for r in range(n_rows):
            row = first_row + r
            _row_copy(u_hbm, xbuf_ref, sem_ref, tok_ref[tile * MOE_TILE + row], row).start()

    @pl.when((t == 0) & (j == 0))
    def _():
        def body(c, carry):
            issue(0, c * per_step, per_step)
            return carry
        lax.fori_loop(0, GATHER_STEPS, body, 0)

    @pl.when((t < nu_ref[0]) & (j == 0))
    def _():
        def body(r, carry):
            _row_copy(u_hbm, xbuf_ref, sem_ref, 0, r).wait()
            return carry
        lax.fori_loop(0, MOE_TILE, body, 0)
        x_ref[...] = xbuf_ref[...].astype(BF16)

    @pl.when((t + 1 < nu_ref[0]) & (j < GATHER_STEPS))
    def _():
        issue(t + 1, j * per_step, per_step)

    @pl.when(j < nj_a)
    def _():
        wg = wg_ref[...].astype(BF16)
        wu = wu_ref[...].astype(BF16)
        for s in range(2):
            @pl.when(s * half < tr_ref[t])
            def _(s=s):
                x = x_ref[s * half:(s + 1) * half, :]
                a = jnp.dot(x, wg, preferred_element_type=F32)
                b = jnp.dot(x, wu, preferred_element_type=F32)
                mid_ref[s, j] = ((a * jax.nn.sigmoid(a)) * b).astype(BF16)

    @pl.when(j >= nj_a)
    def _():
        wd = wd_ref[...].astype(BF16)
        for s in range(2):
            rows = slice(s * half, (s + 1) * half)

            @pl.when(s * half < tr_ref[t])
            def _(s=s, rows=rows):
                mid = jnp.concatenate([mid_ref[s, jm] for jm in range(nj_a)], axis=1)
                o_ref[rows, :] = jnp.dot(mid, wd, preferred_element_type=F32)

            @pl.when(s * half >= tr_ref[t])
            def _(rows=rows):
                o_ref[rows, :] = jnp.zeros((half, o_ref.shape[1]), F32)


def ffn_moe(tile_expert, n_used, tile_rows, row_token, u, wg, wu, wd, tf=256, tn=256):
    p = row_token.shape[0]
    d = u.shape[1]
    f = wg.shape[2]
    tm = MOE_TILE
    nt, nj_a, nj_b = p // tm, f // tf, d // tn
    assert nj_a + nj_b >= GATHER_STEPS and tm % GATHER_STEPS == 0

    def used(t, nu):
        return t < nu[0]

    def col_a(t, j, nu):
        return jnp.where(used(t, nu), jnp.minimum(j, nj_a - 1), nj_a - 1)

    def col_b(t, j, nu):
        return jnp.where(used(t, nu), jnp.clip(j - nj_a, 0, nj_b - 1), nj_b - 1)

    grid_spec = pltpu.PrefetchScalarGridSpec(
        num_scalar_prefetch=4,
        grid=(nt, nj_a + nj_b),
        in_specs=[
            pl.BlockSpec(memory_space=pl.ANY),
            pl.BlockSpec((None, d, tf), lambda t, j, te, nu, tr, tok: (te[t], 0, col_a(t, j, nu))),
            pl.BlockSpec((None, d, tf), lambda t, j, te, nu, tr, tok: (te[t], 0, col_a(t, j, nu))),
            pl.BlockSpec((None, f, tn), lambda t, j, te, nu, tr, tok: (te[t], 0, col_b(t, j, nu))),
        ],
        out_specs=pl.BlockSpec((tm, tn), lambda t, j, te, nu, tr, tok: (t, jnp.clip(j - nj_a, 0, nj_b - 1))),
        scratch_shapes=[pltpu.VMEM((2, nj_a, tm // 2, tf), BF16), pltpu.VMEM((tm, d), F32),
                        pltpu.VMEM((tm, d), BF16), pltpu.SemaphoreType.DMA((1,))],
    )
    return pl.pallas_call(
        functools.partial(_ffn_moe_kernel, nj_a=nj_a),
        grid_spec=grid_spec,
        out_shape=jax.ShapeDtypeStruct((p, d), F32),
        compiler_params=_params(("arbitrary", "arbitrary")),
        name="ffn_moe",
    )(tile_expert, n_used, tile_rows, row_token, u, wg, wu, wd)


def _router_kernel(h_ref, g_ref, wr_ref, c0_ref, idx_ref, gate_ref, cnt_ref, u_ref, base_ref, *, tm):
    i = pl.program_id(0)

    @pl.when(i == 0)
    def _():
        base_ref[...] = c0_ref[...]

    u = _rms(h_ref[...], g_ref[...])
    u_ref[...] = u
    logits = jnp.dot(u, wr_ref[...], preferred_element_type=F32, precision=lax.Precision.HIGHEST)
    lane = lax.broadcasted_iota(I32, (tm, LANES), 1)
    logits = jnp.where(lane < N_EXPERTS, logits, NEG_INF)
    m1 = jnp.max(logits, axis=1, keepdims=True)
    i1 = jnp.min(jnp.where(logits == m1, lane, LANES), axis=1, keepdims=True)
    rest = jnp.where(lane == i1, NEG_INF, logits)
    m2 = jnp.max(rest, axis=1, keepdims=True)
    i2 = jnp.min(jnp.where(rest == m2, lane, LANES), axis=1, keepdims=True)
    e2 = jnp.exp(m2 - m1)
    g1 = 1.0 / (1.0 + e2)
    g2 = e2 / (1.0 + e2)

    sel1 = lane == i1
    sel2 = lane == i2
    onehot = jnp.where(sel1 | sel2, 1.0, 0.0)
    r = lax.broadcasted_iota(I32, (tm, tm), 0)
    c = lax.broadcasted_iota(I32, (tm, tm), 1)
    strict_lower = jnp.where(c < r, 1.0, 0.0).astype(BF16)
    before = jnp.dot(strict_lower, onehot.astype(BF16), preferred_element_type=F32) + base_ref[...]
    r1 = jnp.sum(jnp.where(sel1, before, 0.0), axis=1, keepdims=True)
    r2 = jnp.sum(jnp.where(sel2, before, 0.0), axis=1, keepdims=True)
    base_ref[...] += jnp.sum(onehot, axis=0, keepdims=True)

    idx = jnp.where(lane == 0, i1, jnp.where(lane == 1, i2,
          jnp.where(lane == 2, r1.astype(I32), r2.astype(I32))))
    idx_ref[...] = idx[:, :8]
    gate_ref[...] = jnp.where(lane == 0, g1, g2)[:, :8]

    @pl.when(i == pl.num_programs(0) - 1)
    def _():
        cnt_ref[...] = base_ref[...]


def router(h, g, wr, count0, tm):
    m, d = h.shape
    assert m % tm == 0
    return pl.pallas_call(
        functools.partial(_router_kernel, tm=tm),
        grid=(m // tm,),
        in_specs=[
            pl.BlockSpec((tm, d), lambda i: (i, 0)),
            pl.BlockSpec((1, d), lambda i: (0, 0)),
            pl.BlockSpec((d, LANES), lambda i: (0, 0)),
            pl.BlockSpec((1, LANES), lambda i: (0, 0)),
        ],
        out_specs=[
            pl.BlockSpec((tm, 8), lambda i: (i, 0)),
            pl.BlockSpec((tm, 8), lambda i: (i, 0)),
            pl.BlockSpec((1, LANES), lambda i: (0, 0)),
            pl.BlockSpec((tm, d), lambda i: (i, 0)),
        ],
        out_shape=[
            jax.ShapeDtypeStruct((m, 8), I32),
            jax.ShapeDtypeStruct((m, 8), F32),
            jax.ShapeDtypeStruct((1, LANES), F32),
            jax.ShapeDtypeStruct((m, d), F32),
        ],
        scratch_shapes=[pltpu.VMEM((1, LANES), F32)],
        compiler_params=_params(("arbitrary",)),
        name="router",
    )(h, g.reshape(1, d), wr, count0)


def _combine_kernel(h_ref, y1_ref, y2_ref, gate_ref, g_ref, o_ref):
    gates = gate_ref[...]
    g1 = gates[:, 0:1]
    g2 = gates[:, 1:2]
    h = h_ref[...] + (g1 * y1_ref[...] + g2 * y2_ref[...])
    o_ref[...] = _rms(h, g_ref[...])


def combine_norm(h, y1, y2, gates, g, tm, y_row0):
    m, d = h.shape
    assert m % tm == 0 and y_row0 % tm == 0
    blk0 = y_row0 // tm
    row = pl.BlockSpec((tm, d), lambda i: (i, 0))
    yrow = pl.BlockSpec((tm, d), lambda i: (i + blk0, 0))
    return pl.pallas_call(
        _combine_kernel,
        grid=(m // tm,),
        in_specs=[row, yrow, yrow, pl.BlockSpec((tm, 8), lambda i: (i, 0)),
                  pl.BlockSpec((1, d), lambda i: (0, 0))],
        out_specs=row,
        out_shape=jax.ShapeDtypeStruct((m, d), F32),
        compiler_params=_params(("parallel",)),
        name="combine_norm",
    )(h, y1, y2, gates, g.reshape(1, d))


def _cumsum_kernel(x_ref, o_ref, *, tb, n_blk):
    r = lax.broadcasted_iota(I32, (tb, tb), 0)
    c = lax.broadcasted_iota(I32, (tb, tb), 1)
    upper = jnp.where(r <= c, 1.0, 0.0)
    carry = jnp.zeros((H_FOX, 1), F32)
    for b in range(n_blk):
        blk = x_ref[0, :, b * tb:(b + 1) * tb]
        cs = jnp.dot(blk, upper, preferred_element_type=F32, precision=lax.Precision.HIGHEST) + carry
        o_ref[0, :, b * tb:(b + 1) * tb] = cs
        carry = cs[:, tb - 1:tb]


def cumsum_time(x, tb):
    b, h, t = x.shape
    assert t % tb == 0
    spec = pl.BlockSpec((1, h, t), lambda i: (i, 0, 0))
    return pl.pallas_call(
        functools.partial(_cumsum_kernel, tb=tb, n_blk=t // tb),
        grid=(b,),
        in_specs=[spec],
        out_specs=spec,
        out_shape=jax.ShapeDtypeStruct((b, h, t), F32),
        compiler_params=_params(("parallel",)),
        name="cumsum_time",
    )(x)


def _rows_of(parts, r0, n):
    pieces, start = [], 0
    for part in parts:
        lo, hi = max(r0, start), min(r0 + n, start + part.shape[0])
        if lo < hi:
            pieces.append(part[lo - start:hi - start, :].astype(F32))
        start += part.shape[0]
    got = sum(p.shape[0] for p in pieces)
    if got < n:
        pieces.append(jnp.zeros((n - got, parts[0].shape[1]), F32))
    return pieces[0] if len(pieces) == 1 else jnp.concatenate(pieces, axis=0)


def _split3(x):
    hi = x.astype(BF16).astype(F32)
    r = x - hi
    mid = r.astype(BF16).astype(F32)
    return hi, mid, (r - mid).astype(BF16).astype(F32)


def _bias_cols(col):
    hi, mid, lo = _split3(col)
    lane = lax.broadcasted_iota(I32, (col.shape[0], LANES), 1)
    e = jnp.where(lane == 0, hi, jnp.where(lane == 1, mid, jnp.where(lane == 2, lo,
        jnp.where(lane < 6, 1.0, 0.0))))
    return e.astype(BF16)


def _bias_rows(row):
    hi, mid, lo = _split3(row)
    sub = lax.broadcasted_iota(I32, (LANES, row.shape[1]), 0)
    e = jnp.where(sub < 3, 1.0, jnp.where(sub == 3, hi, jnp.where(sub == 4, mid,
        jnp.where(sub == 5, lo, 0.0))))
    return e.astype(BF16)


def _ones_col(n):
    return jnp.where(lax.broadcasted_iota(I32, (n, LANES), 1) == 0, 1.0, 0.0).astype(BF16)


def _run_blocks(n_full, scores, consume, sa_ref, sb_ref):
    n = sa_ref.shape[0]

    def put(ref, arrays):
        for i, a in enumerate(arrays):
            ref[i] = a

    def get(ref):
        return tuple(ref[i] for i in range(n))

    def pair(i, carry):
        kv = 2 * i
        put(sb_ref, scores(kv + 1))
        consume(get(sa_ref), kv, False)
        put(sa_ref, scores(kv + 2))
        consume(get(sb_ref), kv + 1, False)
        return carry

    def tail_odd():
        put(sb_ref, scores(n_full))
        consume(get(sa_ref), n_full - 1, False)
        consume(get(sb_ref), n_full, True)

    def tail_even():
        consume(get(sa_ref), n_full, True)

    put(sa_ref, scores(0))
    lax.fori_loop(0, n_full // 2, pair, 0)
    if isinstance(n_full, int):
        (tail_odd if n_full % 2 else tail_even)()
    else:
        pl.when(n_full % 2 == 1)(tail_odd)
        pl.when(n_full % 2 == 0)(tail_even)


def _fox_kernel(*refs, tq, tk, q_off, n_q, has_cache):
    if has_cache:
        (q_ref, k_ref, v_ref, kc_ref, vc_ref, cq_ref, ck_ref, o_ref, kt_ref, vb_ref, m_ref, acc_ref,
         sa_ref, sb_ref) = refs
        k_parts, v_parts = (kc_ref, k_ref), (vc_ref, v_ref)
    else:
        q_ref, k_ref, v_ref, cq_ref, ck_ref, o_ref, kt_ref, vb_ref, m_ref, acc_ref, sa_ref, sb_ref = refs
        k_parts, v_parts = (k_ref,), (v_ref,)
    h = pl.program_id(1)
    qi = pl.program_id(2)

    @pl.when(qi == 0)
    def _():
        for blk in range(kt_ref.shape[0]):
            r0 = blk * tk
            kt_ref[blk, 0:HEAD_DIM, :] = _rows_of(k_parts, r0, tk).T.astype(BF16)
            kt_ref[blk, HEAD_DIM:, :] = _bias_rows(-LOG2E * ck_ref[0, h, blk])
            vb_ref[r0:r0 + tk, 0:HEAD_DIM] = _rows_of(v_parts, r0, tk).astype(BF16)
            vb_ref[r0:r0 + tk, HEAD_DIM:] = _ones_col(tk)

    qpos0 = q_off if n_q == 1 else q_off + qi * tq
    cq_row = LOG2E * cq_ref[0, h, q_off // tq + qi]
    eye = lax.broadcasted_iota(I32, (tq, tq), 0) == lax.broadcasted_iota(I32, (tq, tq), 1)
    cq = jnp.sum(jnp.where(eye, cq_row, 0.0), axis=1, keepdims=True)
    qx = jnp.concatenate([q_ref[...], _bias_cols(cq)], axis=1)

    n_split, rs = m_ref.shape[0], m_ref.shape[1]
    qxs = [qx[i * rs:(i + 1) * rs, :] for i in range(n_split)]
    row = lax.broadcasted_iota(I32, (rs, tk), 0)
    col = lax.broadcasted_iota(I32, (rs, tk), 1)

    def scores(kv):
        kt = kt_ref[kv]
        return tuple(jnp.dot(x, kt, preferred_element_type=F32) for x in qxs)

    def consume(ss, kv, masked):
        k0 = pl.multiple_of(kv * tk, tk)
        vblk = vb_ref[pl.ds(k0, tk), :]
        for i, s in enumerate(ss):
            if masked:
                s = jnp.where(kv * tk + col <= qpos0 + i * rs + row, s, NEG_INF)
            m = m_ref[i]
            m_new = jnp.maximum(m, jnp.max(s, axis=1, keepdims=True))
            p = jnp.exp2(s - m_new).astype(BF16)
            pv = jnp.dot(p, vblk, preferred_element_type=F32)
            acc_ref[i] = jnp.exp2(m - m_new) * acc_ref[i] + pv
            m_ref[i] = m_new

    m_ref[...] = jnp.full(m_ref.shape, NEG_INF, F32)
    acc_ref[...] = jnp.zeros(acc_ref.shape, F32)
    n_full = qpos0 // tk

    _run_blocks(n_full, scores, consume, sa_ref, sb_ref)
    for i in range(n_split):
        acc = acc_ref[i]
        o_ref[i * rs:(i + 1) * rs, :] = (acc[:, :HEAD_DIM] / acc[:, HEAD_DIM:HEAD_DIM + 1]).astype(o_ref.dtype)


def fox_attention(q, k, v, c, n_batch, t_q, t_new, q_off, tq, tk, cache=None):
    n_q = t_q // tq
    tc = c.shape[2]
    assert t_q % tq == 0 and q_off % tq == 0 and tc % tq == 0 and tc % tk == 0
    assert (q_off % tk) + tq <= tk if n_q == 1 else (tq == tk and q_off % tk == 0)
    c_q = c.reshape(n_batch, H_FOX, tc // tq, 1, tq)
    c_k = c.reshape(n_batch, H_FOX, tc // tk, 1, tk)
    n_split = 2 if tq % 256 == 0 else 1
    (q, qg), (k, kg), (v, vg) = q, k, v
    in_specs = [pl.BlockSpec((None, tq, HEAD_DIM), lambda b, h, i: (qg, b * n_q + i, h)),
                pl.BlockSpec((None, t_new, HEAD_DIM), lambda b, h, i: (kg, b, h)),
                pl.BlockSpec((None, t_new, HEAD_DIM), lambda b, h, i: (vg, b, h))]
    args = [q, k, v]
    if cache is not None:
        past = cache[0].shape[1]
        assert past + t_new <= tc
        old = pl.BlockSpec((past, HEAD_DIM), lambda b, h, i: (b, h))
        in_specs += [old, old]
        args += [x.reshape(n_batch * past, H_FOX * HEAD_DIM) for x in cache]
    else:
        assert t_new == tc
    in_specs += [pl.BlockSpec((1, H_FOX, tc // tq, 1, tq), lambda b, h, i: (b, 0, 0, 0, 0)),
                 pl.BlockSpec((1, H_FOX, tc // tk, 1, tk), lambda b, h, i: (b, 0, 0, 0, 0))]
    return pl.pallas_call(
        functools.partial(_fox_kernel, tq=tq, tk=tk, q_off=q_off, n_q=n_q, has_cache=cache is not None),
        grid=(n_batch, H_FOX, n_q),
        in_specs=in_specs,
        out_specs=pl.BlockSpec((tq, HEAD_DIM), lambda b, h, i: (b * n_q + i, h)),
        out_shape=jax.ShapeDtypeStruct((n_batch * t_q, H_FOX * HEAD_DIM), BF16),
        scratch_shapes=[pltpu.VMEM((tc // tk, 2 * HEAD_DIM, tk), BF16), pltpu.VMEM((tc, 2 * HEAD_DIM), BF16),
                        pltpu.VMEM((n_split, tq // n_split, 1), F32),
                        pltpu.VMEM((n_split, tq // n_split, 2 * HEAD_DIM), F32),
                        pltpu.VMEM((n_split, tq // n_split, tk), F32), pltpu.VMEM((n_split, tq // n_split, tk), F32)],
        compiler_params=_params(("parallel", "parallel", "arbitrary")),
        name="fox_attention",
    )(*args, c_q, c_k)


def _diff_kernel(slope_ref, q_ref, k_ref, v_ref, *refs, tq, tk, q_off, n_q, n_valid, has_cache):
    if has_cache:
        kc0_ref, kc1_ref, vc_ref = refs[:3]
        refs = refs[3:]
    (lq1_ref, lk1_ref, lq2_ref, lk2_ref, sg_ref, o_ref, kt_ref, vb_ref, m_ref, l_ref, acc_ref,
     sa_ref, sb_ref) = refs
    h = pl.program_id(1)
    qi = pl.program_id(2)
    dv = 2 * HEAD_DIM
    slope2 = slope_ref[h] * LOG2E

    @pl.when(qi == 0)
    def _():
        v_parts = (vc_ref, v_ref) if has_cache else (v_ref,)
        for blk in range(kt_ref.shape[1]):
            r0 = blk * tk
            kpos_row = (r0 + lax.broadcasted_iota(I32, (1, tk), 1)).astype(F32)
            key_bias = _bias_rows(slope2 * kpos_row)
            for mp in range(2):
                new_rows = k_ref.at[:, mp * HEAD_DIM:(mp + 1) * HEAD_DIM]
                k_parts = ((kc0_ref, kc1_ref)[mp], new_rows) if has_cache else (new_rows,)
                kt_ref[mp, blk, 0:HEAD_DIM, :] = _rows_of(k_parts, r0, tk).T.astype(BF16)
                kt_ref[mp, blk, HEAD_DIM:, :] = key_bias
            vb_ref[r0:r0 + tk, :] = _rows_of(v_parts, r0, tk).astype(BF16)

    lam = (jnp.exp(jnp.sum(lq1_ref[...] * lk1_ref[...], axis=1, keepdims=True))
           - jnp.exp(jnp.sum(lq2_ref[...] * lk2_ref[...], axis=1, keepdims=True)) + LAM_INIT_L0)
    qpos0 = q_off if n_q == 1 else q_off + qi * tq
    qpos_col = (qpos0 + lax.broadcasted_iota(I32, (tq, 1), 0)).astype(F32)
    query_bias = _bias_cols(-slope2 * qpos_col)
    q = q_ref[...]
    qx = [jnp.concatenate([q[:, mp * HEAD_DIM:(mp + 1) * HEAD_DIM], query_bias], axis=1) for mp in range(2)]
    row = lax.broadcasted_iota(I32, (tq, tk), 0)
    col = lax.broadcasted_iota(I32, (tq, tk), 1)

    def scores(kv):
        return tuple(jnp.dot(qx[mp], kt_ref[mp, kv], preferred_element_type=F32) for mp in range(2))

    def consume(ss, kv, masked):
        k0 = pl.multiple_of(kv * tk, tk)
        vblk = vb_ref[pl.ds(k0, tk), :]
        if masked:
            qpos = qpos0 + row
            kpos = kv * tk + col
            visible = ((kpos >> 6) <= (qpos >> 6)) & (kpos < n_valid)
            fix = (2.0 * slope2) * jnp.maximum(kpos - qpos, 0).astype(F32)
        for mp in range(2):
            s = ss[mp]
            if masked:
                s = jnp.where(visible, s - fix, NEG_INF)
            m = m_ref[mp]
            m_new = jnp.maximum(m, jnp.max(s, axis=1, keepdims=True))
            alpha = jnp.exp2(m - m_new)
            p = jnp.exp2(s - m_new)
            l_ref[mp] = alpha * l_ref[mp] + jnp.sum(p, axis=1, keepdims=True)
            acc_ref[mp] = alpha * acc_ref[mp] + jnp.dot(p.astype(BF16), vblk, preferred_element_type=F32)
            m_ref[mp] = m_new

    m_ref[...] = jnp.full(m_ref.shape, NEG_INF, F32)
    l_ref[...] = jnp.zeros(l_ref.shape, F32)
    acc_ref[...] = jnp.zeros(acc_ref.shape, F32)
    n_full = qpos0 // tk

    _run_blocks(n_full, scores, consume, sa_ref, sb_ref)
    o = acc_ref[0] / l_ref[0] - lam * (acc_ref[1] / l_ref[1])
    o_ref[...] = (_rms(o, sg_ref[...]) * (1.0 - LAM_INIT_L0)).astype(o_ref.dtype)


def diff_attention(q, k, v, lam_vecs, subln_g, n_batch, t_q, t_new, t_s, q_off, tq, tk, cache=None):
    n_q = t_q // tq
    assert t_q % tq == 0 and t_s % tk == 0
    dv = 2 * HEAD_DIM
    slopes = 2.0 ** (-8.0 * jnp.arange(1, H_DIFF + 1, dtype=F32) / H_DIFF)
    vec = pl.BlockSpec((1, HEAD_DIM), lambda b, h, i, s: (0, 0))
    (q, qg), (k, kg), (v, vg) = q, k, v
    in_specs = [pl.BlockSpec((None, tq, dv), lambda b, h, i, s: (qg, b * n_q + i, h)),
                pl.BlockSpec((None, t_new, dv), lambda b, h, i, s: (kg, b, h)),
                pl.BlockSpec((None, t_new, dv), lambda b, h, i, s: (vg, b, h))]
    args = [slopes, q, k, v]
    past = 0
    if cache is not None:
        past = cache[0].shape[1]
        in_specs += [
            pl.BlockSpec((past, HEAD_DIM), lambda b, h, i, s: (b, 2 * h)),
            pl.BlockSpec((past, HEAD_DIM), lambda b, h, i, s: (b, 2 * h + 1)),
            pl.BlockSpec((past, dv), lambda b, h, i, s: (b, h)),
        ]
        ck2d = cache[0].reshape(n_batch * past, H_DIFF * dv)
        args += [ck2d, ck2d, cache[1].reshape(n_batch * past, H_DIFF * dv)]
    n_valid = past + t_new
    assert n_valid <= t_s and tq % CHUNK == 0 or n_q == 1
    if n_q == 1:
        last_visible = min(((q_off + tq - 1) // CHUNK + 1) * CHUNK, n_valid)
        assert (q_off // tk) * tk <= n_valid and last_visible <= (q_off // tk + 1) * tk
    else:
        assert tq == tk and q_off % tk == 0
    in_specs += [vec, vec, vec, vec, pl.BlockSpec((1, dv), lambda b, h, i, s: (0, 0))]
    grid_spec = pltpu.PrefetchScalarGridSpec(
        num_scalar_prefetch=1,
        grid=(n_batch, H_DIFF, n_q),
        in_specs=in_specs,
        out_specs=pl.BlockSpec((tq, dv), lambda b, h, i, s: (b * n_q + i, h)),
        scratch_shapes=[pltpu.VMEM((2, t_s // tk, dv, tk), BF16), pltpu.VMEM((t_s, dv), BF16),
                        pltpu.VMEM((2, tq, 1), F32), pltpu.VMEM((2, tq, 1), F32), pltpu.VMEM((2, tq, dv), F32),
                        pltpu.VMEM((2, tq, tk), F32), pltpu.VMEM((2, tq, tk), F32)],
    )
    return pl.pallas_call(
        functools.partial(_diff_kernel, tq=tq, tk=tk, q_off=q_off, n_q=n_q, n_valid=n_valid,
                          has_cache=cache is not None),
        grid_spec=grid_spec,
        out_shape=jax.ShapeDtypeStruct((n_batch * t_q, H_DIFF * dv), BF16),
        compiler_params=_params(("parallel", "parallel", "arbitrary")),
        name="diff_attention",
    )(*args, *[x.reshape(1, HEAD_DIM) for x in lam_vecs], subln_g.reshape(1, dv))


REL_PAD = 384


def _band_bias_kernel(t_ref, o_ref, *, nk, n_valid):
    blk = pl.program_id(0)
    row = lax.broadcasted_iota(I32, (REL_PAD, nk), 0)
    j = lax.broadcasted_iota(I32, (REL_PAD, nk), 1)
    jc = lax.broadcasted_iota(I32, (1, nk), 1) >> 6
    jv = lax.broadcasted_iota(I32, (1, nk), 1) < n_valid
    for r in range(8):
        i = blk * 8 + r
        idx = jnp.clip(i - j + BAND_PAST, -REL_CLIP, REL_CLIP) + REL_CLIP
        onehot = jnp.where(row == idx, 1.0, 0.0)
        bias = jnp.dot(t_ref[...], onehot, preferred_element_type=F32, precision=lax.Precision.HIGHEST)
        ic = i >> 6
        visible = (jc - N_PREV_CHUNKS <= ic) & (jc >= ic) & jv
        o_ref[:, r, :] = jnp.where(visible, bias * LOG2E, NEG_INF)


def band_bias(table_padded, tq, nk, n_valid):
    assert CHUNK == 64
    return pl.pallas_call(
        functools.partial(_band_bias_kernel, nk=nk, n_valid=n_valid),
        grid=(tq // 8,),
        in_specs=[pl.BlockSpec((H_BAND, REL_PAD), lambda i: (0, 0))],
        out_specs=pl.BlockSpec((H_BAND, 8, nk), lambda i: (0, i, 0)),
        out_shape=jax.ShapeDtypeStruct((H_BAND, tq, nk), F32),
        compiler_params=_params(("parallel",)),
        name="band_bias",
    )(table_padded)


def _band_kernel(*refs, tq, nk, n_g, q_off, has_cache):
    if has_cache:
        q_ref, k_ref, v_ref, kc_ref, vc_ref, b_ref, o_ref, kb_ref, vb_ref, sa_ref, sb_ref = refs
        k_parts, v_parts = (kc_ref, k_ref), (vc_ref, v_ref)
        front = 0
    else:
        q_ref, k_ref, v_ref, b_ref, o_ref, kb_ref, vb_ref, sa_ref, sb_ref = refs
        k_parts, v_parts = (k_ref,), (v_ref,)
        front = kb_ref.shape[0] - k_ref.shape[0]
        kb_ref[0:front, :] = jnp.zeros((front, HEAD_DIM), BF16)
        vb_ref[0:front, 0:HEAD_DIM] = jnp.zeros((front, HEAD_DIM), BF16)
    t_s = kb_ref.shape[0]
    kb_ref[front:, :] = _rows_of(k_parts, 0, t_s - front).astype(BF16)
    vb_ref[front:, 0:HEAD_DIM] = _rows_of(v_parts, 0, t_s - front).astype(BF16)
    vb_ref[:, HEAD_DIM:] = _ones_col(t_s)
    jrow = lax.broadcasted_iota(I32, (1, nk), 1)

    def row0(g):
        return g * tq if isinstance(g, int) else pl.multiple_of(g * tq, tq)

    def scores(g):
        r0 = row0(g)
        return lax.dot_general(q_ref[pl.ds(r0, tq), :], kb_ref[pl.ds(r0, nk), :], (((1,), (1,)), ((), ())),
                               preferred_element_type=F32)

    def consume(s, g, clip_start):
        r0 = row0(g)
        s = s + b_ref[0]
        if clip_start:
            s = s + jnp.where(jrow >= BAND_PAST - (q_off + g * tq), 0.0, NEG_INF)
        m = jnp.max(s, axis=1, keepdims=True)
        p = jnp.exp2(s - m).astype(BF16)
        o = jnp.dot(p, vb_ref[pl.ds(r0, nk), :], preferred_element_type=F32)
        o_ref[pl.ds(r0, tq), :] = (o[:, :HEAD_DIM] / o[:, HEAD_DIM:HEAD_DIM + 1]).astype(o_ref.dtype)

    n_clip = min(n_g, max(0, -(-(BAND_PAST - q_off) // tq)))
    for g in range(n_clip):
        consume(scores(g), g, True)
    n_rest = n_g - n_clip
    if n_rest == 1:
        consume(scores(n_clip), n_clip, False)
    elif n_rest > 1:
        assert n_rest % 2 == 0
        sa_ref[...] = scores(n_clip)

        def pair(i, carry):
            g = n_clip + 2 * i
            sb_ref[...] = scores(g + 1)
            consume(sa_ref[...], g, False)
            sa_ref[...] = scores(jnp.minimum(g + 2, n_g - 1))
            consume(sb_ref[...], g + 1, False)
            return carry

        lax.fori_loop(0, n_rest // 2, pair, 0)


def band_attention(q, k, v, bias, n_batch, t_q, t_new, q_off, tq, cache=None):
    n_g = t_q // tq
    nk = bias.shape[2]
    t_s = (n_g - 1) * tq + nk
    rows = pl.BlockSpec((t_q, HEAD_DIM), lambda b, h: (b, h))
    (q, qg), (k, kg), (v, vg) = q, k, v
    in_specs = [pl.BlockSpec((None, t_q, HEAD_DIM), lambda b, h: (qg, b, h)),
                pl.BlockSpec((None, t_new, HEAD_DIM), lambda b, h: (kg, b, h)),
                pl.BlockSpec((None, t_new, HEAD_DIM), lambda b, h: (vg, b, h))]
    args = [q, k, v]
    if cache is not None:
        assert cache[0].shape[1] == BAND_PAST and BAND_PAST + t_new <= t_s and n_g == 1
        old = pl.BlockSpec((BAND_PAST, HEAD_DIM), lambda b, h: (b, h))
        in_specs += [old, old]
        args += [x.reshape(n_batch * BAND_PAST, H_BAND * HEAD_DIM) for x in cache]
    else:
        assert BAND_PAST + t_new == t_s
    in_specs.append(pl.BlockSpec((1, tq, nk), lambda b, h: (h, 0, 0)))
    return pl.pallas_call(
        functools.partial(_band_kernel, tq=tq, nk=nk, n_g=n_g, q_off=q_off, has_cache=cache is not None),
        grid=(n_batch, H_BAND),
        in_specs=in_specs,
        out_specs=rows,
        out_shape=jax.ShapeDtypeStruct((n_batch * t_q, H_BAND * HEAD_DIM), BF16),
        scratch_shapes=[pltpu.VMEM((t_s, HEAD_DIM), BF16), pltpu.VMEM((t_s, 2 * HEAD_DIM), BF16),
                        pltpu.VMEM((tq, nk), F32), pltpu.VMEM((tq, nk), F32)],
        compiler_params=_params(("parallel", "parallel")),
        name="band_attention",
    )(*args, bias)


def _even_layer(h, n_batch, t, caches, w, sample):
    (g_mix, w_q, w_kv, w_f, b_f, lam_vecs, subln_g, w_out_a, w_out_b, g_ffn, wg, wu, wd) = w
    hw = H_FOX * HEAD_DIM
    tm = 512 if not sample else h.shape[0]
    tm_in = 1024 if not sample else tm
    q, xn, logf = norm_proj(h, g_mix, w_q, hw, Q_SCALE, wf=w_f, bf=b_f, tm=tm_in)
    kv = proj(xn, w_kv, tm=tm_in)
    fk, fv, dk, dv = [(x[None], 0) for x in kv]
    logf_t = jnp.swapaxes(logf.reshape(n_batch, t, H_FOX), 1, 2)
    if not sample:
        c = cumsum_time(logf_t, 512)
        oa = fox_attention((q, 0), fk, fv, c, n_batch, t, t, 0, 512, 512)
        ob = diff_attention((q, 1), dk, dv, lam_vecs, subln_g, n_batch, t, t, t, 0, 512, 512)
    else:
        cfk, cfv, cfl, cdk, cdv = caches
        past = cfk.shape[1]
        t_k = past + t
        t_pad = -(-t_k // 384) * 384
        lf = jnp.concatenate([jnp.swapaxes(cfl, 1, 2), logf_t], axis=2)
        c = cumsum_time(jnp.pad(lf, ((0, 0), (0, 0), (0, t_pad - t_k))), 384)
        oa = fox_attention((q, 0), fk, fv, c, n_batch, t, t, past, t, 384, cache=(cfk, cfv))
        ob = diff_attention((q, 1), dk, dv, lam_vecs, subln_g, n_batch, t, t, t_pad, past, t, 384,
                            cache=(cdk, cdv))
    h = proj_residual([oa, ob], [w_out_a, w_out_b], h, tm=tm)
    h = ffn_dense(h, g_ffn, wg, wu, wd, tm=tm)
    return h, (kv[0], kv[1], logf, kv[2], kv[3])


def _odd_attention(h, n_batch, t, caches, w, sample):
    (g_mix, w_q, w_kv, table, w_out) = w
    d = D_MODEL
    tm = 512 if not sample else h.shape[0]
    tm_in = 1024 if not sample else tm
    q, xn = norm_proj(h, g_mix, w_q, d, Q_SCALE, tm=tm_in)
    kv = proj(xn, w_kv, tm=tm_in)
    bk, bv = [(x[None], 0) for x in kv]
    if not sample:
        tq = 2 * CHUNK
        bias = band_bias(table, tq, BAND_PAST + tq, BAND_PAST + tq)
        o = band_attention((q, 0), bk, bv, bias, n_batch, t, t, 0, tq)
    else:
        cbk, cbv, past_len = caches
        pc = cbk.shape[1]
        assert pc == BAND_PAST and t <= CHUNK and past_len % CHUNK == 0
        nk = -(-(pc + t) // LANES) * LANES
        bias = band_bias(table, t, nk, pc + t)
        o = band_attention((q, 0), bk, bv, bias, n_batch, t, t, past_len, t, cache=(cbk, cbv))
    h = proj_residual([o], [w_out], h, tm=tm)
    return h, (kv[0], kv[1])


def kernel(x_prompt, x_sample, cache_fox_k, cache_fox_v, cache_fox_logf, cache_diff_k, cache_diff_v, cache_band_k, cache_band_v, norm_mix_even, w_in_even, b_forget, lam_q1, lam_k1, lam_q2, lam_k2, subln_g, w_out_even, norm_ffn_even, w_gate, w_up, w_down, norm_mix_odd, w_in_odd, rel_bias, w_out_odd, norm_ffn_odd, w_router, w_gate_e, w_up_e, w_down_e, norm_final):
    bp, tp, d = x_prompt.shape
    bs, ts, _ = x_sample.shape
    past = cache_fox_k.shape[2]
    hw = H_FOX * HEAD_DIM
    fox_w = 3 * hw + H_FOX

    w_in0 = w_in_even[0]
    w_q0 = jnp.concatenate([w_in0[:, :hw], w_in0[:, fox_w:fox_w + hw]], axis=1).astype(BF16)
    w_kv0 = jnp.stack([w_in0[:, hw:2 * hw], w_in0[:, 2 * hw:3 * hw], w_in0[:, fox_w + hw:fox_w + 2 * hw],
                       w_in0[:, fox_w + 2 * hw:]]).astype(BF16)
    w_f = jnp.pad(w_in0[:, 3 * hw:fox_w], ((0, 0), (0, LANES - H_FOX))).astype(BF16)
    b_f = jnp.pad(b_forget[0], (0, LANES - H_FOX)).reshape(1, LANES)
    lam_vecs = (lam_q1[0], lam_k1[0], lam_q2[0], lam_k2[0])
    w_out0 = w_out_even[0].astype(BF16)
    even_w = (norm_mix_even[0], w_q0, w_kv0, w_f, b_f, lam_vecs, subln_g[0], w_out0[:hw], w_out0[hw:],
              norm_ffn_even[0], w_gate[0].astype(BF16), w_up[0].astype(BF16), w_down[0].astype(BF16))
    table = jnp.pad(rel_bias[0], ((0, 0), (0, REL_PAD - rel_bias.shape[2])))
    w_in1 = w_in_odd[0].astype(BF16)
    odd_w = (norm_mix_odd[0], w_in1[:, :d], jnp.stack([w_in1[:, d:2 * d], w_in1[:, 2 * d:]]), table,
             w_out_odd[0].astype(BF16))
    wr = jnp.pad(w_router[0], ((0, 0), (0, LANES - N_EXPERTS)))

    mp, ms = bp * tp, bs * ts
    h_p = x_prompt.reshape(mp, d)
    h_s = x_sample.reshape(ms, d)

    h_p, (fk_p, fv_p, fl_p, dk_p, dv_p) = _even_layer(h_p, bp, tp, None, even_w, False)
    caches_even = (cache_fox_k[0], cache_fox_v[0], cache_fox_logf[0], cache_diff_k[0], cache_diff_v[0])
    h_s, (fk_s, fv_s, fl_s, dk_s, dv_s) = _even_layer(h_s, bs, ts, caches_even, even_w, True)

    h_p, (bk_p, bv_p) = _odd_attention(h_p, bp, tp, None, odd_w, False)
    h_s, (bk_s, bv_s) = _odd_attention(h_s, bs, ts, (cache_band_k[0], cache_band_v[0], past), odd_w, True)

    m_all = mp + ms
    assert mp % 512 == 0 and mp % ms == 0
    zero_counts = jnp.zeros((1, LANES), F32)
    idx_p, gate_p, cnt_p, u_p = router(h_p, norm_ffn_odd[0], wr, zero_counts, 512)
    idx_s, gate_s, cnt, u_s = router(h_s, norm_ffn_odd[0], wr, cnt_p, ms)
    idx = jnp.concatenate([idx_p, idx_s], axis=0)
    u_all = jnp.concatenate([u_p, u_s], axis=0)
    counts = cnt[0, :N_EXPERTS].astype(I32)
    n_tiles_e = (counts + MOE_TILE - 1) // MOE_TILE
    tile_end = jnp.cumsum(n_tiles_e)
    tile_start = tile_end - n_tiles_e
    row_off = tile_start * MOE_TILE
    n_used = tile_end[-1:]
    nt = (2 * m_all) // MOE_TILE + N_EXPERTS
    tile_ids = jnp.arange(nt, dtype=I32)
    last = jnp.maximum(n_used[0] - 1, 0)
    tile_expert = jnp.sum((jnp.minimum(tile_ids, last)[:, None] >= tile_end[None, :]).astype(I32), axis=1)
    tile_expert = jnp.minimum(tile_expert, N_EXPERTS - 1)
    tile_rows = jnp.clip(counts[tile_expert] - (tile_ids - tile_start[tile_expert]) * MOE_TILE, 0, MOE_TILE)
    pos1 = row_off[idx[:, 0]] + idx[:, 2]
    pos2 = row_off[idx[:, 1]] + idx[:, 3]
    tokens = jnp.arange(m_all, dtype=I32)
    row_token = jnp.zeros((nt * MOE_TILE,), I32).at[pos1].set(tokens).at[pos2].set(tokens)
    y = ffn_moe(tile_expert, n_used.astype(I32), tile_rows.astype(I32), row_token, u_all,
                w_gate_e[0], w_up_e[0], w_down_e[0])
    y1 = jnp.take(y, pos1, axis=0, mode="clip")
    y2 = jnp.take(y, pos2, axis=0, mode="clip")
    y_p = combine_norm(h_p, y1, y2, gate_p, norm_final, 512, 0)
    y_s = combine_norm(h_s, y1, y2, gate_s, norm_final, ms, mp)

    keep = min(BAND_PAST, tp)
    bk_p = bk_p.reshape(bp, tp, H_BAND, HEAD_DIM)[:, tp - keep:][None]
    bv_p = bv_p.reshape(bp, tp, H_BAND, HEAD_DIM)[:, tp - keep:][None]
    return (
        y_p.reshape(bp, tp, d), y_s.reshape(bs, ts, d),
        fk_p.reshape(1, bp, tp, H_FOX, HEAD_DIM), fk_s.reshape(1, bs, ts, H_FOX, HEAD_DIM),
        fv_p.reshape(1, bp, tp, H_FOX, HEAD_DIM), fv_s.reshape(1, bs, ts, H_FOX, HEAD_DIM),
        fl_p.reshape(1, bp, tp, H_FOX), fl_s.reshape(1, bs, ts, H_FOX),
        dk_p.reshape(1, bp, tp, H_DIFF, 2, HEAD_DIM), dk_s.reshape(1, bs, ts, H_DIFF, 2, HEAD_DIM),
        dv_p.reshape(1, bp, tp, H_DIFF, 2 * HEAD_DIM), dv_s.reshape(1, bs, ts, H_DIFF, 2 * HEAD_DIM),
        bk_p, bk_s.reshape(1, bs, ts, H_BAND, HEAD_DIM),
        bv_p, bv_s.reshape(1, bs, ts, H_BAND, HEAD_DIM),
    )
```

```python
import functools
import math

import jax
import jax.numpy as jnp
from jax import lax
from jax.experimental import pallas as pl
from jax.experimental.pallas import tpu as pltpu

F32 = jnp.float32
BF16 = jnp.bfloat16
I32 = jnp.int32

D_MODEL = 2048
HEAD_DIM = 128
CHUNK = 64
H_FOX = 8
H_DIFF = 4
H_BAND = 16
N_PREV_CHUNKS = 8
BAND_PAST = N_PREV_CHUNKS * CHUNK
REL_CLIP = 128
D_FF = 5632
N_EXPERTS = 8
EPS = 1e-6
SCALE = HEAD_DIM ** -0.5
LAM_INIT_L0 = 0.8 - 0.6 * math.exp(-0.3 * 0)
NEG_INF = float("-inf")
LOG2E = 1.4426950408889634
Q_SCALE = SCALE * LOG2E

LANES = 128
VMEM_LIMIT = 56 * 1024 * 1024
MOE_TILE = 1024


def _params(sem):
    return pltpu.CompilerParams(dimension_semantics=sem, vmem_limit_bytes=VMEM_LIMIT)


def _rms(x, g):
    return (x * lax.rsqrt(jnp.mean(x * x, axis=-1, keepdims=True) + EPS)) * g


def _log_sigmoid(x):
    return jnp.minimum(x, 0.0) - jnp.log1p(jnp.exp(-jnp.abs(x)))


def _norm_proj_kernel(*refs, has_forget, scale):
    if has_forget:
        x_ref, g_ref, w_ref, wf_ref, bf_ref, o_ref, xn_ref, logf_ref = refs
    else:
        x_ref, g_ref, w_ref, o_ref, xn_ref = refs

    @pl.when(pl.program_id(1) == 0)
    def _():
        xn = _rms(x_ref[...], g_ref[...]).astype(BF16)
        xn_ref[...] = xn
        if has_forget:
            fa = jnp.dot(xn, wf_ref[...], preferred_element_type=F32)
            logf_ref[...] = _log_sigmoid(fa + bf_ref[...])[:, :H_FOX]

    z = jnp.dot(xn_ref[...], w_ref[...], preferred_element_type=F32)
    o_ref[...] = (z * scale).astype(o_ref.dtype)


def norm_proj(x, g, w, group_width, scale, wf=None, bf=None, tm=512, tn=512):
    m, d = x.shape
    n = w.shape[1]
    assert n % group_width == 0 and group_width % tn == 0 and m % tm == 0
    nb = group_width // tn
    has_forget = wf is not None
    in_specs = [
        pl.BlockSpec((tm, d), lambda i, j: (i, 0)),
        pl.BlockSpec((1, d), lambda i, j: (0, 0)),
        pl.BlockSpec((d, tn), lambda i, j: (0, j)),
    ]
    args = [x, g.reshape(1, d), w]
    out_specs = [pl.BlockSpec((None, tm, tn), lambda i, j: (j // nb, i, j % nb)),
                 pl.BlockSpec((tm, d), lambda i, j: (i, 0))]
    out_shape = [jax.ShapeDtypeStruct((n // group_width, m, group_width), BF16),
                 jax.ShapeDtypeStruct((m, d), BF16)]
    if has_forget:
        in_specs += [pl.BlockSpec((d, LANES), lambda i, j: (0, 0)),
                     pl.BlockSpec((1, LANES), lambda i, j: (0, 0))]
        args += [wf, bf]
        out_specs.append(pl.BlockSpec((tm, H_FOX), lambda i, j: (i, 0)))
        out_shape.append(jax.ShapeDtypeStruct((m, H_FOX), F32))
    return pl.pallas_call(
        functools.partial(_norm_proj_kernel, has_forget=has_forget, scale=float(scale)),
        grid=(m // tm, n // tn),
        in_specs=in_specs,
        out_specs=out_specs,
        out_shape=out_shape,
        compiler_params=_params(("parallel", "arbitrary")),
        name="norm_proj",
    )(*args)


def _proj_kernel(x_ref, w_ref, *o_refs):
    x = x_ref[...]
    for g, o_ref in enumerate(o_refs):
        o_ref[...] = jnp.dot(x, w_ref[g], preferred_element_type=F32)


def proj(x, w, tm=512, tn=256):
    m, d = x.shape
    n_groups, _, gw = w.shape
    assert gw % tn == 0 and m % tm == 0
    return pl.pallas_call(
        _proj_kernel,
        grid=(m // tm, gw // tn),
        in_specs=[pl.BlockSpec((tm, d), lambda i, j: (i, 0)),
                  pl.BlockSpec((n_groups, d, tn), lambda i, j: (0, 0, j))],
        out_specs=[pl.BlockSpec((tm, tn), lambda i, j: (i, j))] * n_groups,
        out_shape=[jax.ShapeDtypeStruct((m, gw), F32)] * n_groups,
        compiler_params=_params(("parallel", "arbitrary")),
        name="proj",
    )(x, w)


def _proj_res_kernel(*refs, n_in):
    xs = refs[:n_in]
    ws = refs[n_in:2 * n_in]
    h_ref = refs[2 * n_in]
    o_ref = refs[2 * n_in + 1]
    acc = h_ref[...]
    for x_ref, w_ref in zip(xs, ws):
        acc = acc + jnp.dot(x_ref[...], w_ref[...], preferred_element_type=F32)
    o_ref[...] = acc


def proj_residual(xs, ws, h, tm=512, tn=2048):
    m, n = h.shape
    assert m % tm == 0 and n % tn == 0
    n_in = len(xs)
    in_specs = [pl.BlockSpec((tm, x.shape[1]), lambda i, j: (i, 0)) for x in xs]
    in_specs += [pl.BlockSpec((w.shape[0], tn), lambda i, j: (0, j)) for w in ws]
    in_specs.append(pl.BlockSpec((tm, tn), lambda i, j: (i, j)))
    return pl.pallas_call(
        functools.partial(_proj_res_kernel, n_in=n_in),
        grid=(m // tm, n // tn),
        in_specs=in_specs,
        out_specs=pl.BlockSpec((tm, tn), lambda i, j: (i, j)),
        out_shape=jax.ShapeDtypeStruct((m, n), F32),
        compiler_params=_params(("parallel", "arbitrary")),
        name="proj_residual",
    )(*xs, *ws, h)


def _swiglu_step(x, wg_ref, wu_ref, wd_ref):
    a = jnp.dot(x, wg_ref[...], preferred_element_type=F32)
    b = jnp.dot(x, wu_ref[...], preferred_element_type=F32)
    mid = (a * jax.nn.sigmoid(a)) * b
    return jnp.dot(mid.astype(BF16), wd_ref[...], preferred_element_type=F32)


def _ffn_dense_kernel(h_ref, g_ref, wg_ref, wu_ref, wd_ref, o_ref, xn_ref, acc_ref):
    j = pl.program_id(1)

    @pl.when(j == 0)
    def _():
        h = h_ref[...]
        xn_ref[...] = _rms(h, g_ref[...]).astype(BF16)
        acc_ref[...] = h

    acc_ref[...] += _swiglu_step(xn_ref[...], wg_ref, wu_ref, wd_ref)

    @pl.when(j == pl.num_programs(1) - 1)
    def _():
        o_ref[...] = acc_ref[...]


def ffn_dense(h, g, wg, wu, wd, tm=512, tf=512):
    m, d = h.shape
    f = wg.shape[1]
    assert m % tm == 0 and f % tf == 0
    return pl.pallas_call(
        _ffn_dense_kernel,
        grid=(m // tm, f // tf),
        in_specs=[
            pl.BlockSpec((tm, d), lambda i, j: (i, 0)),
            pl.BlockSpec((1, d), lambda i, j: (0, 0)),
            pl.BlockSpec((d, tf), lambda i, j: (0, j)),
            pl.BlockSpec((d, tf), lambda i, j: (0, j)),
            pl.BlockSpec((tf, d), lambda i, j: (j, 0)),
        ],
        out_specs=pl.BlockSpec((tm, d), lambda i, j: (i, 0)),
        out_shape=jax.ShapeDtypeStruct((m, d), F32),
        scratch_shapes=[pltpu.VMEM((tm, d), BF16), pltpu.VMEM((tm, d), F32)],
        compiler_params=_params(("parallel", "arbitrary")),
        name="ffn_dense",
    )(h, g.reshape(1, d), wg, wu, wd)


GATHER_STEPS = 16


def _row_copy(u_hbm, xbuf_ref, sem_ref, src_row, dst_row):
    return pltpu.make_async_copy(u_hbm.at[pl.ds(src_row, 1)], xbuf_ref.at[pl.ds(dst_row, 1)], sem_ref.at[0])


def _ffn_moe_kernel(te_ref, nu_ref, tr_ref, tok_ref, u_hbm, wg_ref, wu_ref, wd_ref, o_ref,
                    mid_ref, xbuf_ref, x_ref, sem_ref, *, nj_a):
    t = pl.program_id(0)
    j = pl.program_id(1)
    half = MOE_TILE // 2
    per_step = MOE_TILE // GATHER_STEPS

    def issue(tile, first_row, n_rows):
        for r in range(n_rows):
            row = first_row + r
            _row_copy(u_hbm, xbuf_ref, sem_ref, tok_ref[tile * MOE_TILE + row], row).start()

    @pl.when((t == 0) & (j == 0))
    def _():
        def body(c, carry):
            issue(0, c * per_step, per_step)
            return carry
        lax.fori_loop(0, GATHER_STEPS, body, 0)

    @pl.when((t < nu_ref[0]) & (j == 0))
    def _():
        def body(r, carry):
            _row_copy(u_hbm, xbuf_ref, sem_ref, 0, r).wait()
            return carry
        lax.fori_loop(0, MOE_TILE, body, 0, unroll=8)
        x_ref[...] = xbuf_ref[...].astype(BF16)

    @pl.when((t + 1 < nu_ref[0]) & (j < GATHER_STEPS))
    def _():
        issue(t + 1, j * per_step, per_step)

    @pl.when((j < nj_a) & (tr_ref[t] > 0))
    def _():
        wg = wg_ref[...].astype(BF16)
        wu = wu_ref[...].astype(BF16)
        for s in range(2):
            @pl.when(s * half < tr_ref[t])
            def _(s=s):
                x = x_ref[s * half:(s + 1) * half, :]
                a = jnp.dot(x, wg, preferred_element_type=F32)
                b = jnp.dot(x, wu, preferred_element_type=F32)
                mid_ref[s, j] = ((a * jax.nn.sigmoid(a)) * b).astype(BF16)

    @pl.when(j >= nj_a)
    def _():
        wd = wd_ref[...].astype(BF16)
        for s in range(2):
            rows = slice(s * half, (s + 1) * half)

            @pl.when(s * half < tr_ref[t])
            def _(s=s, rows=rows):
                mid = jnp.concatenate([mid_ref[s, jm] for jm in range(nj_a)], axis=1)
                o_ref[rows, :] = jnp.dot(mid, wd, preferred_element_type=F32)

            @pl.when(s * half >= tr_ref[t])
            def _(rows=rows):
                o_ref[rows, :] = jnp.zeros((half, o_ref.shape[1]), F32)


def ffn_moe(tile_expert, n_used, tile_rows, row_token, u, wg, wu, wd, tf=256, tn=256):
    p = row_token.shape[0]
    d = u.shape[1]
    f = wg.shape[2]
    tm = MOE_TILE
    nt, nj_a, nj_b = p // tm, f // tf, d // tn
    assert nj_a + nj_b >= GATHER_STEPS and tm % GATHER_STEPS == 0

    def used(t, nu):
        return t < nu[0]

    def col_a(t, j, nu):
        return jnp.where(used(t, nu), jnp.minimum(j, nj_a - 1), nj_a - 1)

    def col_b(t, j, nu):
        return jnp.where(used(t, nu), jnp.clip(j - nj_a, 0, nj_b - 1), nj_b - 1)

    grid_spec = pltpu.PrefetchScalarGridSpec(
        num_scalar_prefetch=4,
        grid=(nt, nj_a + nj_b),
        in_specs=[
            pl.BlockSpec(memory_space=pl.ANY),
            pl.BlockSpec((None, d, tf), lambda t, j, te, nu, tr, tok: (te[t], 0, col_a(t, j, nu))),
            pl.BlockSpec((None, d, tf), lambda t, j, te, nu, tr, tok: (te[t], 0, col_a(t, j, nu))),
            pl.BlockSpec((None, f, tn), lambda t, j, te, nu, tr, tok: (te[t], 0, col_b(t, j, nu))),
        ],
        out_specs=pl.BlockSpec((tm, tn), lambda t, j, te, nu, tr, tok: (t, jnp.clip(j - nj_a, 0, nj_b - 1))),
        scratch_shapes=[pltpu.VMEM((2, nj_a, tm // 2, tf), BF16), pltpu.VMEM((tm, d), F32),
                        pltpu.VMEM((tm, d), BF16), pltpu.SemaphoreType.DMA((1,))],
    )
    return pl.pallas_call(
        functools.partial(_ffn_moe_kernel, nj_a=nj_a),
        grid_spec=grid_spec,
        out_shape=jax.ShapeDtypeStruct((p, d), F32),
        compiler_params=_params(("arbitrary", "arbitrary")),
        name="ffn_moe",
    )(tile_expert, n_used, tile_rows, row_token, u, wg, wu, wd)


def _router_kernel(h_ref, g_ref, wr_ref, c0_ref, idx_ref, gate_ref, cnt_ref, u_ref, base_ref, *, tm):
    i = pl.program_id(0)

    @pl.when(i == 0)
    def _():
        base_ref[...] = c0_ref[...]

    u = _rms(h_ref[...], g_ref[...])
    u_ref[...] = u
    logits = jnp.dot(u, wr_ref[...], preferred_element_type=F32, precision=lax.Precision.HIGHEST)
    lane = lax.broadcasted_iota(I32, (tm, LANES), 1)
    logits = jnp.where(lane < N_EXPERTS, logits, NEG_INF)
    m1 = jnp.max(logits, axis=1, keepdims=True)
    i1 = jnp.min(jnp.where(logits == m1, lane, LANES), axis=1, keepdims=True)
    rest = jnp.where(lane == i1, NEG_INF, logits)
    m2 = jnp.max(rest, axis=1, keepdims=True)
    i2 = jnp.min(jnp.where(rest == m2, lane, LANES), axis=1, keepdims=True)
    e2 = jnp.exp(m2 - m1)
    g1 = 1.0 / (1.0 + e2)
    g2 = e2 / (1.0 + e2)

    sel1 = lane == i1
    sel2 = lane == i2
    onehot = jnp.where(sel1 | sel2, 1.0, 0.0)
    r = lax.broadcasted_iota(I32, (tm, tm), 0)
    c = lax.broadcasted_iota(I32, (tm, tm), 1)
    strict_lower = jnp.where(c < r, 1.0, 0.0).astype(BF16)
    before = jnp.dot(strict_lower, onehot.astype(BF16), preferred_element_type=F32) + base_ref[...]
    r1 = jnp.sum(jnp.where(sel1, before, 0.0), axis=1, keepdims=True)
    r2 = jnp.sum(jnp.where(sel2, before, 0.0), axis=1, keepdims=True)
    base_ref[...] += jnp.sum(onehot, axis=0, keepdims=True)

    idx = jnp.where(lane == 0, i1, jnp.where(lane == 1, i2,
          jnp.where(lane == 2, r1.astype(I32), r2.astype(I32))))
    idx_ref[...] = idx[:, :8]
    gate_ref[...] = jnp.where(lane == 0, g1, g2)[:, :8]

    @pl.when(i == pl.num_programs(0) - 1)
    def _():
        cnt_ref[...] = base_ref[...]


def router(h, g, wr, count0, tm):
    m, d = h.shape
    assert m % tm == 0
    return pl.pallas_call(
        functools.partial(_router_kernel, tm=tm),
        grid=(m // tm,),
        in_specs=[
            pl.BlockSpec((tm, d), lambda i: (i, 0)),
            pl.BlockSpec((1, d), lambda i: (0, 0)),
            pl.BlockSpec((d, LANES), lambda i: (0, 0)),
            pl.BlockSpec((1, LANES), lambda i: (0, 0)),
        ],
        out_specs=[
            pl.BlockSpec((tm, 8), lambda i: (i, 0)),
            pl.BlockSpec((tm, 8), lambda i: (i, 0)),
            pl.BlockSpec((1, LANES), lambda i: (0, 0)),
            pl.BlockSpec((tm, d), lambda i: (i, 0)),
        ],
        out_shape=[
            jax.ShapeDtypeStruct((m, 8), I32),
            jax.ShapeDtypeStruct((m, 8), F32),
            jax.ShapeDtypeStruct((1, LANES), F32),
            jax.ShapeDtypeStruct((m, d), F32),
        ],
        scratch_shapes=[pltpu.VMEM((1, LANES), F32)],
        compiler_params=_params(("arbitrary",)),
        name="router",
    )(h, g.reshape(1, d), wr, count0)


def _combine_kernel(h_ref, y1_ref, y2_ref, gate_ref, g_ref, o_ref):
    gates = gate_ref[...]
    g1 = gates[:, 0:1]
    g2 = gates[:, 1:2]
    h = h_ref[...] + (g1 * y1_ref[...] + g2 * y2_ref[...])
    o_ref[...] = _rms(h, g_ref[...])


def combine_norm(h, y1, y2, gates, g, tm, y_row0):
    m, d = h.shape
    assert m % tm == 0 and y_row0 % tm == 0
    blk0 = y_row0 // tm
    row = pl.BlockSpec((tm, d), lambda i: (i, 0))
    yrow = pl.BlockSpec((tm, d), lambda i: (i + blk0, 0))
    return pl.pallas_call(
        _combine_kernel,
        grid=(m // tm,),
        in_specs=[row, yrow, yrow, pl.BlockSpec((tm, 8), lambda i: (i, 0)),
                  pl.BlockSpec((1, d), lambda i: (0, 0))],
        out_specs=row,
        out_shape=jax.ShapeDtypeStruct((m, d), F32),
        compiler_params=_params(("parallel",)),
        name="combine_norm",
    )(h, y1, y2, gates, g.reshape(1, d))


def _cumsum_kernel(x_ref, o_ref, *, tb, n_blk):
    r = lax.broadcasted_iota(I32, (tb, tb), 0)
    c = lax.broadcasted_iota(I32, (tb, tb), 1)
    upper = jnp.where(r <= c, 1.0, 0.0)
    carry = jnp.zeros((H_FOX, 1), F32)
    for b in range(n_blk):
        blk = x_ref[0, :, b * tb:(b + 1) * tb]
        cs = jnp.dot(blk, upper, preferred_element_type=F32, precision=lax.Precision.HIGHEST) + carry
        o_ref[0, :, b * tb:(b + 1) * tb] = cs
        carry = cs[:, tb - 1:tb]


def cumsum_time(x, tb):
    b, h, t = x.shape
    assert t % tb == 0
    spec = pl.BlockSpec((1, h, t), lambda i: (i, 0, 0))
    return pl.pallas_call(
        functools.partial(_cumsum_kernel, tb=tb, n_blk=t // tb),
        grid=(b,),
        in_specs=[spec],
        out_specs=spec,
        out_shape=jax.ShapeDtypeStruct((b, h, t), F32),
        compiler_params=_params(("parallel",)),
        name="cumsum_time",
    )(x)


def _rows_of(parts, r0, n):
    pieces, start = [], 0
    for part in parts:
        lo, hi = max(r0, start), min(r0 + n, start + part.shape[0])
        if lo < hi:
            pieces.append(part[lo - start:hi - start, :].astype(F32))
        start += part.shape[0]
    got = sum(p.shape[0] for p in pieces)
    if got < n:
        pieces.append(jnp.zeros((n - got, parts[0].shape[1]), F32))
    return pieces[0] if len(pieces) == 1 else jnp.concatenate(pieces, axis=0)


def _split3(x):
    hi = x.astype(BF16).astype(F32)
    r = x - hi
    mid = r.astype(BF16).astype(F32)
    return hi, mid, (r - mid).astype(BF16).astype(F32)


def _bias_cols(col):
    hi, mid, lo = _split3(col)
    lane = lax.broadcasted_iota(I32, (col.shape[0], LANES), 1)
    e = jnp.where(lane == 0, hi, jnp.where(lane == 1, mid, jnp.where(lane == 2, lo,
        jnp.where(lane < 6, 1.0, 0.0))))
    return e.astype(BF16)


def _bias_rows(row):
    hi, mid, lo = _split3(row)
    sub = lax.broadcasted_iota(I32, (LANES, row.shape[1]), 0)
    e = jnp.where(sub < 3, 1.0, jnp.where(sub == 3, hi, jnp.where(sub == 4, mid,
        jnp.where(sub == 5, lo, 0.0))))
    return e.astype(BF16)


def _ones_col(n):
    return jnp.where(lax.broadcasted_iota(I32, (n, LANES), 1) == 0, 1.0, 0.0).astype(BF16)


def _run_blocks(n_full, scores, consume, sa_ref, sb_ref):
    n = sa_ref.shape[0]

    def put(ref, arrays):
        for i, a in enumerate(arrays):
            ref[i] = a

    def get(ref):
        return tuple(ref[i] for i in range(n))

    def pair(i, carry):
        kv = 2 * i
        put(sb_ref, scores(kv + 1))
        consume(get(sa_ref), kv, False)
        put(sa_ref, scores(kv + 2))
        consume(get(sb_ref), kv + 1, False)
        return carry

    def tail_odd():
        put(sb_ref, scores(n_full))
        consume(get(sa_ref), n_full - 1, False)
        consume(get(sb_ref), n_full, True)

    def tail_even():
        consume(get(sa_ref), n_full, True)

    put(sa_ref, scores(0))
    lax.fori_loop(0, n_full // 2, pair, 0)
    if isinstance(n_full, int):
        (tail_odd if n_full % 2 else tail_even)()
    else:
        pl.when(n_full % 2 == 1)(tail_odd)
        pl.when(n_full % 2 == 0)(tail_even)


def _fox_kernel(*refs, tq, tk, q_off, n_q, has_cache):
    if has_cache:
        (q_ref, k_ref, v_ref, kc_ref, vc_ref, cq_ref, ck_ref, o_ref, kt_ref, vb_ref, m_ref, acc_ref,
         sa_ref, sb_ref) = refs
        k_parts, v_parts = (kc_ref, k_ref), (vc_ref, v_ref)
    else:
        q_ref, k_ref, v_ref, cq_ref, ck_ref, o_ref, kt_ref, vb_ref, m_ref, acc_ref, sa_ref, sb_ref = refs
        k_parts, v_parts = (k_ref,), (v_ref,)
    h = pl.program_id(1)
    qi = pl.program_id(2)

    @pl.when(qi == 0)
    def _():
        for blk in range(kt_ref.shape[0]):
            r0 = blk * tk
            kt_ref[blk, 0:HEAD_DIM, :] = _rows_of(k_parts, r0, tk).T.astype(BF16)
            kt_ref[blk, HEAD_DIM:, :] = _bias_rows(-LOG2E * ck_ref[0, h, blk])
            vb_ref[r0:r0 + tk, 0:HEAD_DIM] = _rows_of(v_parts, r0, tk).astype(BF16)
            vb_ref[r0:r0 + tk, HEAD_DIM:] = _ones_col(tk)

    qpos0 = q_off if n_q == 1 else q_off + qi * tq
    cq_row = LOG2E * cq_ref[0, h, q_off // tq + qi]
    eye = lax.broadcasted_iota(I32, (tq, tq), 0) == lax.broadcasted_iota(I32, (tq, tq), 1)
    cq = jnp.sum(jnp.where(eye, cq_row, 0.0), axis=1, keepdims=True)
    qx = jnp.concatenate([q_ref[...], _bias_cols(cq)], axis=1)

    n_split, rs = m_ref.shape[0], m_ref.shape[1]
    qxs = [qx[i * rs:(i + 1) * rs, :] for i in range(n_split)]
    row = lax.broadcasted_iota(I32, (rs, tk), 0)
    col = lax.broadcasted_iota(I32, (rs, tk), 1)

    def scores(kv):
        kt = kt_ref[kv]
        return tuple(jnp.dot(x, kt, preferred_element_type=F32) for x in qxs)

    def consume(ss, kv, masked):
        k0 = pl.multiple_of(kv * tk, tk)
        vblk = vb_ref[pl.ds(k0, tk), :]
        for i, s in enumerate(ss):
            if masked:
                s = jnp.where(kv * tk + col <= qpos0 + i * rs + row, s, NEG_INF)
            m = m_ref[i]
            m_new = jnp.maximum(m, jnp.max(s, axis=1, keepdims=True))
            p = jnp.exp2(s - m_new).astype(BF16)
            pv = jnp.dot(p, vblk, preferred_element_type=F32)
            acc_ref[i] = jnp.exp2(m - m_new) * acc_ref[i] + pv
            m_ref[i] = m_new

    m_ref[...] = jnp.full(m_ref.shape, NEG_INF, F32)
    acc_ref[...] = jnp.zeros(acc_ref.shape, F32)
    n_full = qpos0 // tk

    _run_blocks(n_full, scores, consume, sa_ref, sb_ref)
    for i in range(n_split):
        acc = acc_ref[i]
        o_ref[i * rs:(i + 1) * rs, :] = (acc[:, :HEAD_DIM] / acc[:, HEAD_DIM:HEAD_DIM + 1]).astype(o_ref.dtype)


def fox_attention(q, k, v, c, n_batch, t_q, t_new, q_off, tq, tk, cache=None):
    n_q = t_q // tq
    tc = c.shape[2]
    assert t_q % tq == 0 and q_off % tq == 0 and tc % tq == 0 and tc % tk == 0
    assert (q_off % tk) + tq <= tk if n_q == 1 else (tq == tk and q_off % tk == 0)
    c_q = c.reshape(n_batch, H_FOX, tc // tq, 1, tq)
    c_k = c.reshape(n_batch, H_FOX, tc // tk, 1, tk)
    n_split = 2 if tq % 256 == 0 else 1
    (q, qg), (k, kg), (v, vg) = q, k, v
    in_specs = [pl.BlockSpec((None, tq, HEAD_DIM), lambda b, h, i: (qg, b * n_q + i, h)),
                pl.BlockSpec((None, t_new, HEAD_DIM), lambda b, h, i: (kg, b, h)),
                pl.BlockSpec((None, t_new, HEAD_DIM), lambda b, h, i: (vg, b, h))]
    args = [q, k, v]
    if cache is not None:
        past = cache[0].shape[1]
        assert past + t_new <= tc
        old = pl.BlockSpec((past, HEAD_DIM), lambda b, h, i: (b, h))
        in_specs += [old, old]
        args += [x.reshape(n_batch * past, H_FOX * HEAD_DIM) for x in cache]
    else:
        assert t_new == tc
    in_specs += [pl.BlockSpec((1, H_FOX, tc // tq, 1, tq), lambda b, h, i: (b, 0, 0, 0, 0)),
                 pl.BlockSpec((1, H_FOX, tc // tk, 1, tk), lambda b, h, i: (b, 0, 0, 0, 0))]
    return pl.pallas_call(
        functools.partial(_fox_kernel, tq=tq, tk=tk, q_off=q_off, n_q=n_q, has_cache=cache is not None),
        grid=(n_batch, H_FOX, n_q),
        in_specs=in_specs,
        out_specs=pl.BlockSpec((tq, HEAD_DIM), lambda b, h, i: (b * n_q + i, h)),
        out_shape=jax.ShapeDtypeStruct((n_batch * t_q, H_FOX * HEAD_DIM), BF16),
        scratch_shapes=[pltpu.VMEM((tc // tk, 2 * HEAD_DIM, tk), BF16), pltpu.VMEM((tc, 2 * HEAD_DIM), BF16),
                        pltpu.VMEM((n_split, tq // n_split, 1), F32),
                        pltpu.VMEM((n_split, tq // n_split, 2 * HEAD_DIM), F32),
                        pltpu.VMEM((n_split, tq // n_split, tk), F32), pltpu.VMEM((n_split, tq // n_split, tk), F32)],
        compiler_params=_params(("parallel", "parallel", "arbitrary")),
        name="fox_attention",
    )(*args, c_q, c_k)


def _diff_kernel(slope_ref, q_ref, k_ref, v_ref, *refs, tq, tk, q_off, n_q, n_valid, has_cache):
    if has_cache:
        kc0_ref, kc1_ref, vc_ref = refs[:3]
        refs = refs[3:]
    (lq1_ref, lk1_ref, lq2_ref, lk2_ref, sg_ref, o_ref, kt_ref, vb_ref, m_ref, l_ref, acc_ref,
     sa_ref, sb_ref) = refs
    h = pl.program_id(1)
    qi = pl.program_id(2)
    dv = 2 * HEAD_DIM
    slope2 = slope_ref[h] * LOG2E

    @pl.when(qi == 0)
    def _():
        v_parts = (vc_ref, v_ref) if has_cache else (v_ref,)
        for blk in range(kt_ref.shape[1]):
            r0 = blk * tk
            kpos_row = (r0 + lax.broadcasted_iota(I32, (1, tk), 1)).astype(F32)
            key_bias = _bias_rows(slope2 * kpos_row)
            for mp in range(2):
                new_rows = k_ref.at[:, mp * HEAD_DIM:(mp + 1) * HEAD_DIM]
                k_parts = ((kc0_ref, kc1_ref)[mp], new_rows) if has_cache else (new_rows,)
                kt_ref[mp, blk, 0:HEAD_DIM, :] = _rows_of(k_parts, r0, tk).T.astype(BF16)
                kt_ref[mp, blk, HEAD_DIM:, :] = key_bias
            vb_ref[r0:r0 + tk, :] = _rows_of(v_parts, r0, tk).astype(BF16)

    lam = (jnp.exp(jnp.sum(lq1_ref[...] * lk1_ref[...], axis=1, keepdims=True))
           - jnp.exp(jnp.sum(lq2_ref[...] * lk2_ref[...], axis=1, keepdims=True)) + LAM_INIT_L0)
    qpos0 = q_off if n_q == 1 else q_off + qi * tq
    qpos_col = (qpos0 + lax.broadcasted_iota(I32, (tq, 1), 0)).astype(F32)
    query_bias = _bias_cols(-slope2 * qpos_col)
    q = q_ref[...]
    n_split, rs = m_ref.shape[0] // 2, m_ref.shape[1]
    qx = [jnp.concatenate([q[i * rs:(i + 1) * rs, mp * HEAD_DIM:(mp + 1) * HEAD_DIM],
                           query_bias[i * rs:(i + 1) * rs, :]], axis=1)
          for mp in range(2) for i in range(n_split)]
    row = lax.broadcasted_iota(I32, (rs, tk), 0)
    col = lax.broadcasted_iota(I32, (rs, tk), 1)

    def scores(kv):
        return tuple(jnp.dot(qx[c], kt_ref[c // n_split, kv], preferred_element_type=F32)
                     for c in range(2 * n_split))

    def consume(ss, kv, masked):
        k0 = pl.multiple_of(kv * tk, tk)
        vblk = vb_ref[pl.ds(k0, tk), :]
        for c, s in enumerate(ss):
            if masked:
                qpos = qpos0 + (c % n_split) * rs + row
                kpos = kv * tk + col
                visible = ((kpos >> 6) <= (qpos >> 6)) & (kpos < n_valid)
                fix = (2.0 * slope2) * jnp.maximum(kpos - qpos, 0).astype(F32)
                s = jnp.where(visible, s - fix, NEG_INF)
            m = m_ref[c]
            m_new = jnp.maximum(m, jnp.max(s, axis=1, keepdims=True))
            alpha = jnp.exp2(m - m_new)
            p = jnp.exp2(s - m_new)
            l_ref[c] = alpha * l_ref[c] + jnp.sum(p, axis=1, keepdims=True)
            acc_ref[c] = alpha * acc_ref[c] + jnp.dot(p.astype(BF16), vblk, preferred_element_type=F32)
            m_ref[c] = m_new

    m_ref[...] = jnp.full(m_ref.shape, NEG_INF, F32)
    l_ref[...] = jnp.zeros(l_ref.shape, F32)
    acc_ref[...] = jnp.zeros(acc_ref.shape, F32)
    n_full = qpos0 // tk

    _run_blocks(n_full, scores, consume, sa_ref, sb_ref)
    for i in range(n_split):
        o = acc_ref[i] / l_ref[i] - lam * (acc_ref[n_split + i] / l_ref[n_split + i])
        o_ref[i * rs:(i + 1) * rs, :] = (_rms(o, sg_ref[...]) * (1.0 - LAM_INIT_L0)).astype(o_ref.dtype)


def diff_attention(q, k, v, lam_vecs, subln_g, n_batch, t_q, t_new, t_s, q_off, tq, tk, cache=None):
    n_q = t_q // tq
    assert t_q % tq == 0 and t_s % tk == 0
    dv = 2 * HEAD_DIM
    slopes = 2.0 ** (-8.0 * jnp.arange(1, H_DIFF + 1, dtype=F32) / H_DIFF)
    vec = pl.BlockSpec((1, HEAD_DIM), lambda b, h, i, s: (0, 0))
    (q, qg), (k, kg), (v, vg) = q, k, v
    in_specs = [pl.BlockSpec((None, tq, dv), lambda b, h, i, s: (qg, b * n_q + i, h)),
                pl.BlockSpec((None, t_new, dv), lambda b, h, i, s: (kg, b, h)),
                pl.BlockSpec((None, t_new, dv), lambda b, h, i, s: (vg, b, h))]
    args = [slopes, q, k, v]
    past = 0
    if cache is not None:
        past = cache[0].shape[1]
        in_specs += [
            pl.BlockSpec((past, HEAD_DIM), lambda b, h, i, s: (b, 2 * h)),
            pl.BlockSpec((past, HEAD_DIM), lambda b, h, i, s: (b, 2 * h + 1)),
            pl.BlockSpec((past, dv), lambda b, h, i, s: (b, h)),
        ]
        ck2d = cache[0].reshape(n_batch * past, H_DIFF * dv)
        args += [ck2d, ck2d, cache[1].reshape(n_batch * past, H_DIFF * dv)]
    n_valid = past + t_new
    assert n_valid <= t_s and tq % CHUNK == 0 or n_q == 1
    if n_q == 1:
        last_visible = min(((q_off + tq - 1) // CHUNK + 1) * CHUNK, n_valid)
        assert (q_off // tk) * tk <= n_valid and last_visible <= (q_off // tk + 1) * tk
    else:
        assert tq == tk and q_off % tk == 0
    in_specs += [vec, vec, vec, vec, pl.BlockSpec((1, dv), lambda b, h, i, s: (0, 0))]
    n_split = 1
    n_ch, rs = 2 * n_split, tq // n_split
    grid_spec = pltpu.PrefetchScalarGridSpec(
        num_scalar_prefetch=1,
        grid=(n_batch, H_DIFF, n_q),
        in_specs=in_specs,
        out_specs=pl.BlockSpec((tq, dv), lambda b, h, i, s: (b * n_q + i, h)),
        scratch_shapes=[pltpu.VMEM((2, t_s // tk, dv, tk), BF16), pltpu.VMEM((t_s, dv), BF16),
                        pltpu.VMEM((n_ch, rs, 1), F32), pltpu.VMEM((n_ch, rs, 1), F32),
                        pltpu.VMEM((n_ch, rs, dv), F32),
                        pltpu.VMEM((n_ch, rs, tk), F32), pltpu.VMEM((n_ch, rs, tk), F32)],
    )
    return pl.pallas_call(
        functools.partial(_diff_kernel, tq=tq, tk=tk, q_off=q_off, n_q=n_q, n_valid=n_valid,
                          has_cache=cache is not None),
        grid_spec=grid_spec,
        out_shape=jax.ShapeDtypeStruct((n_batch * t_q, H_DIFF * dv), BF16),
        compiler_params=_params(("parallel", "parallel", "arbitrary")),
        name="diff_attention",
    )(*args, *[x.reshape(1, HEAD_DIM) for x in lam_vecs], subln_g.reshape(1, dv))


REL_PAD = 384


def _band_bias_kernel(t_ref, o_ref, *, nk, n_valid):
    blk = pl.program_id(0)
    row = lax.broadcasted_iota(I32, (REL_PAD, nk), 0)
    j = lax.broadcasted_iota(I32, (REL_PAD, nk), 1)
    jc = lax.broadcasted_iota(I32, (1, nk), 1) >> 6
    jv = lax.broadcasted_iota(I32, (1, nk), 1) < n_valid
    for r in range(8):
        i = blk * 8 + r
        idx = jnp.clip(i - j + BAND_PAST, -REL_CLIP, REL_CLIP) + REL_CLIP
        onehot = jnp.where(row == idx, 1.0, 0.0)
        bias = jnp.dot(t_ref[...], onehot, preferred_element_type=F32, precision=lax.Precision.HIGHEST)
        ic = i >> 6
        visible = (jc - N_PREV_CHUNKS <= ic) & (jc >= ic) & jv
        o_ref[:, r, :] = jnp.where(visible, bias * LOG2E, NEG_INF)


def band_bias(table_padded, tq, nk, n_valid):
    assert CHUNK == 64
    return pl.pallas_call(
        functools.partial(_band_bias_kernel, nk=nk, n_valid=n_valid),
        grid=(tq // 8,),
        in_specs=[pl.BlockSpec((H_BAND, REL_PAD), lambda i: (0, 0))],
        out_specs=pl.BlockSpec((H_BAND, 8, nk), lambda i: (0, i, 0)),
        out_shape=jax.ShapeDtypeStruct((H_BAND, tq, nk), F32),
        compiler_params=_params(("parallel",)),
        name="band_bias",
    )(table_padded)


def _band_kernel(*refs, tq, nk, n_g, q_off, has_cache):
    if has_cache:
        q_ref, k_ref, v_ref, kc_ref, vc_ref, b_ref, o_ref, kb_ref, vb_ref, sa_ref, sb_ref = refs
        k_parts, v_parts = (kc_ref, k_ref), (vc_ref, v_ref)
        front = 0
    else:
        q_ref, k_ref, v_ref, b_ref, o_ref, kb_ref, vb_ref, sa_ref, sb_ref = refs
        k_parts, v_parts = (k_ref,), (v_ref,)
        front = kb_ref.shape[0] - k_ref.shape[0]
        kb_ref[0:front, :] = jnp.zeros((front, HEAD_DIM), BF16)
        vb_ref[0:front, 0:HEAD_DIM] = jnp.zeros((front, HEAD_DIM), BF16)
    t_s = kb_ref.shape[0]
    kb_ref[front:, :] = _rows_of(k_parts, 0, t_s - front).astype(BF16)
    vb_ref[front:, 0:HEAD_DIM] = _rows_of(v_parts, 0, t_s - front).astype(BF16)
    vb_ref[:, HEAD_DIM:] = _ones_col(t_s)
    jrow = lax.broadcasted_iota(I32, (1, nk), 1)

    def row0(g):
        return g * tq if isinstance(g, int) else pl.multiple_of(g * tq, tq)

    def scores(g):
        r0 = row0(g)
        return lax.dot_general(q_ref[pl.ds(r0, tq), :], kb_ref[pl.ds(r0, nk), :], (((1,), (1,)), ((), ())),
                               preferred_element_type=F32)

    def consume(s, g, clip_start):
        r0 = row0(g)
        s = s + b_ref[0]
        if clip_start:
            s = s + jnp.where(jrow >= BAND_PAST - (q_off + g * tq), 0.0, NEG_INF)
        m = jnp.max(s, axis=1, keepdims=True)
        p = jnp.exp2(s - m).astype(BF16)
        o = jnp.dot(p, vb_ref[pl.ds(r0, nk), :], preferred_element_type=F32)
        o_ref[pl.ds(r0, tq), :] = (o[:, :HEAD_DIM] / o[:, HEAD_DIM:HEAD_DIM + 1]).astype(o_ref.dtype)

    n_clip = min(n_g, max(0, -(-(BAND_PAST - q_off) // tq)))
    for g in range(n_clip):
        consume(scores(g), g, True)
    n_rest = n_g - n_clip
    if n_rest == 1:
        consume(scores(n_clip), n_clip, False)
    elif n_rest > 1:
        assert n_rest % 2 == 0
        sa_ref[...] = scores(n_clip)

        def pair(i, carry):
            g = n_clip + 2 * i
            sb_ref[...] = scores(g + 1)
            consume(sa_ref[...], g, False)
            sa_ref[...] = scores(jnp.minimum(g + 2, n_g - 1))
            consume(sb_ref[...], g + 1, False)
            return carry

        lax.fori_loop(0, n_rest // 2, pair, 0)


def band_attention(q, k, v, bias, n_batch, t_q, t_new, q_off, tq, cache=None):
    n_g = t_q // tq
    nk = bias.shape[2]
    t_s = (n_g - 1) * tq + nk
    rows = pl.BlockSpec((t_q, HEAD_DIM), lambda b, h: (b, h))
    (q, qg), (k, kg), (v, vg) = q, k, v
    in_specs = [pl.BlockSpec((None, t_q, HEAD_DIM), lambda b, h: (qg, b, h)),
                pl.BlockSpec((None, t_new, HEAD_DIM), lambda b, h: (kg, b, h)),
                pl.BlockSpec((None, t_new, HEAD_DIM), lambda b, h: (vg, b, h))]
    args = [q, k, v]
    if cache is not None:
        assert cache[0].shape[1] == BAND_PAST and BAND_PAST + t_new <= t_s and n_g == 1
        old = pl.BlockSpec((BAND_PAST, HEAD_DIM), lambda b, h: (b, h))
        in_specs += [old, old]
        args += [x.reshape(n_batch * BAND_PAST, H_BAND * HEAD_DIM) for x in cache]
    else:
        assert BAND_PAST + t_new == t_s
    in_specs.append(pl.BlockSpec((1, tq, nk), lambda b, h: (h, 0, 0)))
    return pl.pallas_call(
        functools.partial(_band_kernel, tq=tq, nk=nk, n_g=n_g, q_off=q_off, has_cache=cache is not None),
        grid=(n_batch, H_BAND),
        in_specs=in_specs,
        out_specs=rows,
        out_shape=jax.ShapeDtypeStruct((n_batch * t_q, H_BAND * HEAD_DIM), BF16),
        scratch_shapes=[pltpu.VMEM((t_s, HEAD_DIM), BF16), pltpu.VMEM((t_s, 2 * HEAD_DIM), BF16),
                        pltpu.VMEM((tq, nk), F32), pltpu.VMEM((tq, nk), F32)],
        compiler_params=_params(("parallel", "parallel")),
        name="band_attention",
    )(*args, bias)


def _even_layer(h, n_batch, t, caches, w, sample):
    (g_mix, w_q, w_kv, w_f, b_f, lam_vecs, subln_g, w_out_a, w_out_b, g_ffn, wg, wu, wd) = w
    hw = H_FOX * HEAD_DIM
    tm = 512 if not sample else h.shape[0]
    tm_in = 1024 if not sample else tm
    q, xn, logf = norm_proj(h, g_mix, w_q, hw, Q_SCALE, wf=w_f, bf=b_f, tm=tm_in)
    kv = proj(xn, w_kv, tm=tm_in)
    fk, fv, dk, dv = [(x[None], 0) for x in kv]
    logf_t = jnp.swapaxes(logf.reshape(n_batch, t, H_FOX), 1, 2)
    if not sample:
        c = cumsum_time(logf_t, 512)
        oa = fox_attention((q, 0), fk, fv, c, n_batch, t, t, 0, 512, 512)
        ob = diff_attention((q, 1), dk, dv, lam_vecs, subln_g, n_batch, t, t, t, 0, 512, 512)
    else:
        cfk, cfv, cfl, cdk, cdv = caches
        past = cfk.shape[1]
        t_k = past + t
        t_pad = -(-t_k // 384) * 384
        lf = jnp.concatenate([jnp.swapaxes(cfl, 1, 2), logf_t], axis=2)
        c = cumsum_time(jnp.pad(lf, ((0, 0), (0, 0), (0, t_pad - t_k))), 384)
        oa = fox_attention((q, 0), fk, fv, c, n_batch, t, t, past, t, 384, cache=(cfk, cfv))
        ob = diff_attention((q, 1), dk, dv, lam_vecs, subln_g, n_batch, t, t, t_pad, past, t, 384,
                            cache=(cdk, cdv))
    h = proj_residual([oa, ob], [w_out_a, w_out_b], h, tm=tm)
    h = ffn_dense(h, g_ffn, wg, wu, wd, tm=tm)
    return h, (kv[0], kv[1], logf, kv[2], kv[3])


def _odd_attention(h, n_batch, t, caches, w, sample):
    (g_mix, w_q, w_kv, table, w_out) = w
    d = D_MODEL
    tm = 512 if not sample else h.shape[0]
    tm_in = 1024 if not sample else tm
    q, xn = norm_proj(h, g_mix, w_q, d, Q_SCALE, tm=tm_in)
    kv = proj(xn, w_kv, tm=tm_in)
    bk, bv = [(x[None], 0) for x in kv]
    if not sample:
        tq = 2 * CHUNK
        bias = band_bias(table, tq, BAND_PAST + tq, BAND_PAST + tq)
        o = band_attention((q, 0), bk, bv, bias, n_batch, t, t, 0, tq)
    else:
        cbk, cbv, past_len = caches
        pc = cbk.shape[1]
        assert pc == BAND_PAST and t <= CHUNK and past_len % CHUNK == 0
        nk = -(-(pc + t) // LANES) * LANES
        bias = band_bias(table, t, nk, pc + t)
        o = band_attention((q, 0), bk, bv, bias, n_batch, t, t, past_len, t, cache=(cbk, cbv))
    h = proj_residual([o], [w_out], h, tm=tm)
    return h, (kv[0], kv[1])


def kernel(x_prompt, x_sample, cache_fox_k, cache_fox_v, cache_fox_logf, cache_diff_k, cache_diff_v, cache_band_k, cache_band_v, norm_mix_even, w_in_even, b_forget, lam_q1, lam_k1, lam_q2, lam_k2, subln_g, w_out_even, norm_ffn_even, w_gate, w_up, w_down, norm_mix_odd, w_in_odd, rel_bias, w_out_odd, norm_ffn_odd, w_router, w_gate_e, w_up_e, w_down_e, norm_final):
    bp, tp, d = x_prompt.shape
    bs, ts, _ = x_sample.shape
    past = cache_fox_k.shape[2]
    hw = H_FOX * HEAD_DIM
    fox_w = 3 * hw + H_FOX

    w_in0 = w_in_even[0]
    w_q0 = jnp.concatenate([w_in0[:, :hw], w_in0[:, fox_w:fox_w + hw]], axis=1).astype(BF16)
    w_kv0 = jnp.stack([w_in0[:, hw:2 * hw], w_in0[:, 2 * hw:3 * hw], w_in0[:, fox_w + hw:fox_w + 2 * hw],
                       w_in0[:, fox_w + 2 * hw:]]).astype(BF16)
    w_f = jnp.pad(w_in0[:, 3 * hw:fox_w], ((0, 0), (0, LANES - H_FOX))).astype(BF16)
    b_f = jnp.pad(b_forget[0], (0, LANES - H_FOX)).reshape(1, LANES)
    lam_vecs = (lam_q1[0], lam_k1[0], lam_q2[0], lam_k2[0])
    w_out0 = w_out_even[0].astype(BF16)
    even_w = (norm_mix_even[0], w_q0, w_kv0, w_f, b_f, lam_vecs, subln_g[0], w_out0[:hw], w_out0[hw:],
              norm_ffn_even[0], w_gate[0].astype(BF16), w_up[0].astype(BF16), w_down[0].astype(BF16))
    table = jnp.pad(rel_bias[0], ((0, 0), (0, REL_PAD - rel_bias.shape[2])))
    w_in1 = w_in_odd[0].astype(BF16)
    odd_w = (norm_mix_odd[0], w_in1[:, :d], jnp.stack([w_in1[:, d:2 * d], w_in1[:, 2 * d:]]), table,
             w_out_odd[0].astype(BF16))
    wr = jnp.pad(w_router[0], ((0, 0), (0, LANES - N_EXPERTS)))

    mp, ms = bp * tp, bs * ts
    h_p = x_prompt.reshape(mp, d)
    h_s = x_sample.reshape(ms, d)

    h_p, (fk_p, fv_p, fl_p, dk_p, dv_p) = _even_layer(h_p, bp, tp, None, even_w, False)
    caches_even = (cache_fox_k[0], cache_fox_v[0], cache_fox_logf[0], cache_diff_k[0], cache_diff_v[0])
    h_s, (fk_s, fv_s, fl_s, dk_s, dv_s) = _even_layer(h_s, bs, ts, caches_even, even_w, True)

    h_p, (bk_p, bv_p) = _odd_attention(h_p, bp, tp, None, odd_w, False)
    h_s, (bk_s, bv_s) = _odd_attention(h_s, bs, ts, (cache_band_k[0], cache_band_v[0], past), odd_w, True)

    m_all = mp + ms
    assert mp % 512 == 0 and mp % ms == 0
    zero_counts = jnp.zeros((1, LANES), F32)
    idx_p, gate_p, cnt_p, u_p = router(h_p, norm_ffn_odd[0], wr, zero_counts, 512)
    idx_s, gate_s, cnt, u_s = router(h_s, norm_ffn_odd[0], wr, cnt_p, ms)
    idx = jnp.concatenate([idx_p, idx_s], axis=0)
    u_all = jnp.concatenate([u_p, u_s], axis=0)
    counts = cnt[0, :N_EXPERTS].astype(I32)
    n_tiles_e = (counts + MOE_TILE - 1) // MOE_TILE
    tile_end = jnp.cumsum(n_tiles_e)
    tile_start = tile_end - n_tiles_e
    row_off = tile_start * MOE_TILE
    n_used = tile_end[-1:]
    nt = (2 * m_all) // MOE_TILE + N_EXPERTS
    tile_ids = jnp.arange(nt, dtype=I32)
    last = jnp.maximum(n_used[0] - 1, 0)
    tile_expert = jnp.sum((jnp.minimum(tile_ids, last)[:, None] >= tile_end[None, :]).astype(I32), axis=1)
    tile_expert = jnp.minimum(tile_expert, N_EXPERTS - 1)
    tile_rows = jnp.clip(counts[tile_expert] - (tile_ids - tile_start[tile_expert]) * MOE_TILE, 0, MOE_TILE)
    pos1 = row_off[idx[:, 0]] + idx[:, 2]
    pos2 = row_off[idx[:, 1]] + idx[:, 3]
    tokens = jnp.arange(m_all, dtype=I32)
    row_token = jnp.zeros((nt * MOE_TILE,), I32).at[pos1].set(tokens).at[pos2].set(tokens)
    y = ffn_moe(tile_expert, n_used.astype(I32), tile_rows.astype(I32), row_token, u_all,
                w_gate_e[0], w_up_e[0], w_down_e[0])
    y1 = jnp.take(y, pos1, axis=0, mode="clip")
    y2 = jnp.take(y, pos2, axis=0, mode="clip")
    y_p = combine_norm(h_p, y1, y2, gate_p, norm_final, 512, 0)
    y_s = combine_norm(h_s, y1, y2, gate_s, norm_final, ms, mp)

    keep = min(BAND_PAST, tp)
    bk_p = bk_p.reshape(bp, tp, H_BAND, HEAD_DIM)[:, tp - keep:][None]
    bv_p = bv_p.reshape(bp, tp, H_BAND, HEAD_DIM)[:, tp - keep:][None]
    return (
        y_p.reshape(bp, tp, d), y_s.reshape(bs, ts, d),
        fk_p.reshape(1, bp, tp, H_FOX, HEAD_DIM), fk_s.reshape(1, bs, ts, H_FOX, HEAD_DIM),
        fv_p.reshape(1, bp, tp, H_FOX, HEAD_DIM), fv_s.reshape(1, bs, ts, H_FOX, HEAD_DIM),
        fl_p.reshape(1, bp, tp, H_FOX), fl_s.reshape(1, bs, ts, H_FOX),
        dk_p.reshape(1, bp, tp, H_DIFF, 2, HEAD_DIM), dk_s.reshape(1, bs, ts, H_DIFF, 2, HEAD_DIM),
        dv_p.reshape(1, bp, tp, H_DIFF, 2 * HEAD_DIM), dv_s.reshape(1, bs, ts, H_DIFF, 2 * HEAD_DIM),
        bk_p, bk_s.reshape(1, bs, ts, H_BAND, HEAD_DIM),
        bv_p, bv_s.reshape(1, bs, ts, H_BAND, HEAD_DIM),
    )
```

```python
import functools
import math

import jax
import jax.numpy as jnp
from jax import lax
from jax.experimental import pallas as pl
from jax.experimental.pallas import tpu as pltpu

F32 = jnp.float32
BF16 = jnp.bfloat16
I32 = jnp.int32

D_MODEL = 2048
HEAD_DIM = 128
CHUNK = 64
H_FOX = 8
H_DIFF = 4
H_BAND = 16
N_PREV_CHUNKS = 8
BAND_PAST = N_PREV_CHUNKS * CHUNK
REL_CLIP = 128
D_FF = 5632
N_EXPERTS = 8
EPS = 1e-6
SCALE = HEAD_DIM ** -0.5
LAM_INIT_L0 = 0.8 - 0.6 * math.exp(-0.3 * 0)
NEG_INF = float("-inf")
LOG2E = 1.4426950408889634
Q_SCALE = SCALE * LOG2E

LANES = 128
VMEM_LIMIT = 56 * 1024 * 1024
MOE_TILE = 1024


def _params(sem):
    return pltpu.CompilerParams(dimension_semantics=sem, vmem_limit_bytes=VMEM_LIMIT)


def _rms(x, g):
    return (x * lax.rsqrt(jnp.mean(x * x, axis=-1, keepdims=True) + EPS)) * g


def _log_sigmoid(x):
    return jnp.minimum(x, 0.0) - jnp.log1p(jnp.exp(-jnp.abs(x)))


def _norm_proj_kernel(*refs, has_forget, scale):
    if has_forget:
        x_ref, g_ref, w_ref, wf_ref, bf_ref, o_ref, xn_ref, logf_ref = refs
    else:
        x_ref, g_ref, w_ref, o_ref, xn_ref = refs

    @pl.when(pl.program_id(1) == 0)
    def _():
        xn = _rms(x_ref[...], g_ref[...]).astype(BF16)
        xn_ref[...] = xn
        if has_forget:
            fa = jnp.dot(xn, wf_ref[...], preferred_element_type=F32)
            logf_ref[...] = _log_sigmoid(fa + bf_ref[...])[:, :H_FOX]

    z = jnp.dot(xn_ref[...], w_ref[...], preferred_element_type=F32)
    o_ref[...] = (z * scale).astype(o_ref.dtype)


def norm_proj(x, g, w, group_width, scale, wf=None, bf=None, tm=512, tn=512):
    m, d = x.shape
    n = w.shape[1]
    assert n % group_width == 0 and group_width % tn == 0 and m % tm == 0
    nb = group_width // tn
    has_forget = wf is not None
    in_specs = [
        pl.BlockSpec((tm, d), lambda i, j: (i, 0)),
        pl.BlockSpec((1, d), lambda i, j: (0, 0)),
        pl.BlockSpec((d, tn), lambda i, j: (0, j)),
    ]
    args = [x, g.reshape(1, d), w]
    out_specs = [pl.BlockSpec((None, tm, tn), lambda i, j: (j // nb, i, j % nb)),
                 pl.BlockSpec((tm, d), lambda i, j: (i, 0))]
    out_shape = [jax.ShapeDtypeStruct((n // group_width, m, group_width), BF16),
                 jax.ShapeDtypeStruct((m, d), BF16)]
    if has_forget:
        in_specs += [pl.BlockSpec((d, LANES), lambda i, j: (0, 0)),
                     pl.BlockSpec((1, LANES), lambda i, j: (0, 0))]
        args += [wf, bf]
        out_specs.append(pl.BlockSpec((tm, H_FOX), lambda i, j: (i, 0)))
        out_shape.append(jax.ShapeDtypeStruct((m, H_FOX), F32))
    return pl.pallas_call(
        functools.partial(_norm_proj_kernel, has_forget=has_forget, scale=float(scale)),
        grid=(m // tm, n // tn),
        in_specs=in_specs,
        out_specs=out_specs,
        out_shape=out_shape,
        compiler_params=_params(("parallel", "arbitrary")),
        name="norm_proj",
    )(*args)


def _proj_kernel(x_ref, w_ref, *o_refs):
    x = x_ref[...]
    for g, o_ref in enumerate(o_refs):
        o_ref[...] = jnp.dot(x, w_ref[g], preferred_element_type=F32)


def proj(x, w, tm=512, tn=256):
    m, d = x.shape
    n_groups, _, gw = w.shape
    assert gw % tn == 0 and m % tm == 0
    return pl.pallas_call(
        _proj_kernel,
        grid=(m // tm, gw // tn),
        in_specs=[pl.BlockSpec((tm, d), lambda i, j: (i, 0)),
                  pl.BlockSpec((n_groups, d, tn), lambda i, j: (0, 0, j))],
        out_specs=[pl.BlockSpec((tm, tn), lambda i, j: (i, j))] * n_groups,
        out_shape=[jax.ShapeDtypeStruct((m, gw), F32)] * n_groups,
        compiler_params=_params(("parallel", "arbitrary")),
        name="proj",
    )(x, w)


def _proj_res_kernel(*refs, n_in):
    xs = refs[:n_in]
    ws = refs[n_in:2 * n_in]
    h_ref = refs[2 * n_in]
    o_ref = refs[2 * n_in + 1]
    acc = h_ref[...]
    for x_ref, w_ref in zip(xs, ws):
        acc = acc + jnp.dot(x_ref[...], w_ref[...], preferred_element_type=F32)
    o_ref[...] = acc


def proj_residual(xs, ws, h, tm=512, tn=2048):
    m, n = h.shape
    assert m % tm == 0 and n % tn == 0
    n_in = len(xs)
    in_specs = [pl.BlockSpec((tm, x.shape[1]), lambda i, j: (i, 0)) for x in xs]
    in_specs += [pl.BlockSpec((w.shape[0], tn), lambda i, j: (0, j)) for w in ws]
    in_specs.append(pl.BlockSpec((tm, tn), lambda i, j: (i, j)))
    return pl.pallas_call(
        functools.partial(_proj_res_kernel, n_in=n_in),
        grid=(m // tm, n // tn),
        in_specs=in_specs,
        out_specs=pl.BlockSpec((tm, tn), lambda i, j: (i, j)),
        out_shape=jax.ShapeDtypeStruct((m, n), F32),
        compiler_params=_params(("parallel", "arbitrary")),
        name="proj_residual",
    )(*xs, *ws, h)


def _swiglu_step(x, wg_ref, wu_ref, wd_ref):
    a = jnp.dot(x, wg_ref[...], preferred_element_type=F32)
    b = jnp.dot(x, wu_ref[...], preferred_element_type=F32)
    mid = (a * jax.nn.sigmoid(a)) * b
    return jnp.dot(mid.astype(BF16), wd_ref[...], preferred_element_type=F32)


def _ffn_dense_kernel(h_ref, g_ref, wg_ref, wu_ref, wd_ref, o_ref, xn_ref, acc_ref):
    j = pl.program_id(1)

    @pl.when(j == 0)
    def _():
        h = h_ref[...]
        xn_ref[...] = _rms(h, g_ref[...]).astype(BF16)
        acc_ref[...] = h

    acc_ref[...] += _swiglu_step(xn_ref[...], wg_ref, wu_ref, wd_ref)

    @pl.when(j == pl.num_programs(1) - 1)
    def _():
        o_ref[...] = acc_ref[...]


def ffn_dense(h, g, wg, wu, wd, tm=512, tf=512):
    m, d = h.shape
    f = wg.shape[1]
    assert m % tm == 0 and f % tf == 0
    return pl.pallas_call(
        _ffn_dense_kernel,
        grid=(m // tm, f // tf),
        in_specs=[
            pl.BlockSpec((tm, d), lambda i, j: (i, 0)),
            pl.BlockSpec((1, d), lambda i, j: (0, 0)),
            pl.BlockSpec((d, tf), lambda i, j: (0, j)),
            pl.BlockSpec((d, tf), lambda i, j: (0, j)),
            pl.BlockSpec((tf, d), lambda i, j: (j, 0)),
        ],
        out_specs=pl.BlockSpec((tm, d), lambda i, j: (i, 0)),
        out_shape=jax.ShapeDtypeStruct((m, d), F32),
        scratch_shapes=[pltpu.VMEM((tm, d), BF16), pltpu.VMEM((tm, d), F32)],
        compiler_params=_params(("parallel", "arbitrary")),
        name="ffn_dense",
    )(h, g.reshape(1, d), wg, wu, wd)


GATHER_STEPS = 16


def _row_copy(u_hbm, xbuf_ref, sem_ref, src_row, dst_row):
    return pltpu.make_async_copy(u_hbm.at[pl.ds(src_row, 1)], xbuf_ref.at[pl.ds(dst_row, 1)], sem_ref.at[0])


def _ffn_moe_kernel(te_ref, nu_ref, tr_ref, tok_ref, u_hbm, wg_ref, wu_ref, wd_ref, o_ref,
                    mid_ref, xbuf_ref, x_ref, sem_ref, *, nj_a):
    t = pl.program_id(0)
    j = pl.program_id(1)
    half = MOE_TILE // 2
    per_step = MOE_TILE // GATHER_STEPS

    def issue(tile, first_row, n_rows):
        for r in range(n_rows):
            row = first_row + r
            _row_copy(u_hbm, xbuf_ref, sem_ref, tok_ref[tile * MOE_TILE + row], row).start()

    @pl.when((t == 0) & (j == 0))
    def _():
        def body(c, carry):
            issue(0, c * per_step, per_step)
            return carry
        lax.fori_loop(0, GATHER_STEPS, body, 0)

    @pl.when((t < nu_ref[0]) & (j == 0))
    def _():
        def body(r, carry):
            _row_copy(u_hbm, xbuf_ref, sem_ref, 0, r).wait()
            return carry
        lax.fori_loop(0, MOE_TILE, body, 0, unroll=8)
        x_ref[...] = xbuf_ref[...].astype(BF16)

    @pl.when((t + 1 < nu_ref[0]) & (j < GATHER_STEPS))
    def _():
        issue(t + 1, j * per_step, per_step)

    @pl.when((j < nj_a) & (tr_ref[t] > 0))
    def _():
        wg = wg_ref[...].astype(BF16)
        wu = wu_ref[...].astype(BF16)
        for s in range(2):
            @pl.when(s * half < tr_ref[t])
            def _(s=s):
                x = x_ref[s * half:(s + 1) * half, :]
                a = jnp.dot(x, wg, preferred_element_type=F32)
                b = jnp.dot(x, wu, preferred_element_type=F32)
                mid_ref[s, j] = ((a * jax.nn.sigmoid(a)) * b).astype(BF16)

    @pl.when(j >= nj_a)
    def _():
        wd = wd_ref[...].astype(BF16)
        for s in range(2):
            rows = slice(s * half, (s + 1) * half)

            @pl.when(s * half < tr_ref[t])
            def _(s=s, rows=rows):
                mid = jnp.concatenate([mid_ref[s, jm] for jm in range(nj_a)], axis=1)
                o_ref[rows, :] = jnp.dot(mid, wd, preferred_element_type=F32)

            @pl.when(s * half >= tr_ref[t])
            def _(rows=rows):
                o_ref[rows, :] = jnp.zeros((half, o_ref.shape[1]), F32)


def ffn_moe(tile_expert, n_used, tile_rows, row_token, u, wg, wu, wd, tf=256, tn=256):
    p = row_token.shape[0]
    d = u.shape[1]
    f = wg.shape[2]
    tm = MOE_TILE
    nt, nj_a, nj_b = p // tm, f // tf, d // tn
    assert nj_a + nj_b >= GATHER_STEPS and tm % GATHER_STEPS == 0

    def used(t, nu):
        return t < nu[0]

    def col_a(t, j, nu):
        return jnp.where(used(t, nu), jnp.minimum(j, nj_a - 1), nj_a - 1)

    def col_b(t, j, nu):
        return jnp.where(used(t, nu), jnp.clip(j - nj_a, 0, nj_b - 1), nj_b - 1)

    grid_spec = pltpu.PrefetchScalarGridSpec(
        num_scalar_prefetch=4,
        grid=(nt, nj_a + nj_b),
        in_specs=[
            pl.BlockSpec(memory_space=pl.ANY),
            pl.BlockSpec((None, d, tf), lambda t, j, te, nu, tr, tok: (te[t], 0, col_a(t, j, nu))),
            pl.BlockSpec((None, d, tf), lambda t, j, te, nu, tr, tok: (te[t], 0, col_a(t, j, nu))),
            pl.BlockSpec((None, f, tn), lambda t, j, te, nu, tr, tok: (te[t], 0, col_b(t, j, nu))),
        ],
        out_specs=pl.BlockSpec((tm, tn), lambda t, j, te, nu, tr, tok: (t, jnp.clip(j - nj_a, 0, nj_b - 1))),
        scratch_shapes=[pltpu.VMEM((2, nj_a, tm // 2, tf), BF16), pltpu.VMEM((tm, d), F32),
                        pltpu.VMEM((tm, d), BF16), pltpu.SemaphoreType.DMA((1,))],
    )
    return pl.pallas_call(
        functools.partial(_ffn_moe_kernel, nj_a=nj_a),
        grid_spec=grid_spec,
        out_shape=jax.ShapeDtypeStruct((p, d), F32),
        compiler_params=_params(("arbitrary", "arbitrary")),
        name="ffn_moe",
    )(tile_expert, n_used, tile_rows, row_token, u, wg, wu, wd)


def _router_kernel(h_ref, g_ref, wr_ref, c0_ref, idx_ref, gate_ref, cnt_ref, u_ref, base_ref, *, tm):
    i = pl.program_id(0)

    @pl.when(i == 0)
    def _():
        base_ref[...] = c0_ref[...]

    u = _rms(h_ref[...], g_ref[...])
    u_ref[...] = u
    logits = jnp.dot(u, wr_ref[...], preferred_element_type=F32, precision=lax.Precision.HIGHEST)
    lane = lax.broadcasted_iota(I32, (tm, LANES), 1)
    logits = jnp.where(lane < N_EXPERTS, logits, NEG_INF)
    m1 = jnp.max(logits, axis=1, keepdims=True)
    i1 = jnp.min(jnp.where(logits == m1, lane, LANES), axis=1, keepdims=True)
    rest = jnp.where(lane == i1, NEG_INF, logits)
    m2 = jnp.max(rest, axis=1, keepdims=True)
    i2 = jnp.min(jnp.where(rest == m2, lane, LANES), axis=1, keepdims=True)
    e2 = jnp.exp(m2 - m1)
    g1 = 1.0 / (1.0 + e2)
    g2 = e2 / (1.0 + e2)

    sel1 = lane == i1
    sel2 = lane == i2
    onehot = jnp.where(sel1 | sel2, 1.0, 0.0)
    r = lax.broadcasted_iota(I32, (tm, tm), 0)
    c = lax.broadcasted_iota(I32, (tm, tm), 1)
    strict_lower = jnp.where(c < r, 1.0, 0.0).astype(BF16)
    before = jnp.dot(strict_lower, onehot.astype(BF16), preferred_element_type=F32) + base_ref[...]
    r1 = jnp.sum(jnp.where(sel1, before, 0.0), axis=1, keepdims=True)
    r2 = jnp.sum(jnp.where(sel2, before, 0.0), axis=1, keepdims=True)
    base_ref[...] += jnp.sum(onehot, axis=0, keepdims=True)

    idx = jnp.where(lane == 0, i1, jnp.where(lane == 1, i2,
          jnp.where(lane == 2, r1.astype(I32), r2.astype(I32))))
    idx_ref[...] = idx[:, :8]
    gate_ref[...] = jnp.where(lane == 0, g1, g2)[:, :8]

    @pl.when(i == pl.num_programs(0) - 1)
    def _():
        cnt_ref[...] = base_ref[...]


def router(h, g, wr, count0, tm):
    m, d = h.shape
    assert m % tm == 0
    return pl.pallas_call(
        functools.partial(_router_kernel, tm=tm),
        grid=(m // tm,),
        in_specs=[
            pl.BlockSpec((tm, d), lambda i: (i, 0)),
            pl.BlockSpec((1, d), lambda i: (0, 0)),
            pl.BlockSpec((d, LANES), lambda i: (0, 0)),
            pl.BlockSpec((1, LANES), lambda i: (0, 0)),
        ],
        out_specs=[
            pl.BlockSpec((tm, 8), lambda i: (i, 0)),
            pl.BlockSpec((tm, 8), lambda i: (i, 0)),
            pl.BlockSpec((1, LANES), lambda i: (0, 0)),
            pl.BlockSpec((tm, d), lambda i: (i, 0)),
        ],
        out_shape=[
            jax.ShapeDtypeStruct((m, 8), I32),
            jax.ShapeDtypeStruct((m, 8), F32),
            jax.ShapeDtypeStruct((1, LANES), F32),
            jax.ShapeDtypeStruct((m, d), F32),
        ],
        scratch_shapes=[pltpu.VMEM((1, LANES), F32)],
        compiler_params=_params(("arbitrary",)),
        name="router",
    )(h, g.reshape(1, d), wr, count0)


def _combine_kernel(h_ref, y1_ref, y2_ref, gate_ref, g_ref, o_ref):
    gates = gate_ref[...]
    g1 = gates[:, 0:1]
    g2 = gates[:, 1:2]
    h = h_ref[...] + (g1 * y1_ref[...] + g2 * y2_ref[...])
    o_ref[...] = _rms(h, g_ref[...])


def combine_norm(h, y1, y2, gates, g, tm, y_row0):
    m, d = h.shape
    assert m % tm == 0 and y_row0 % tm == 0
    blk0 = y_row0 // tm
    row = pl.BlockSpec((tm, d), lambda i: (i, 0))
    yrow = pl.BlockSpec((tm, d), lambda i: (i + blk0, 0))
    return pl.pallas_call(
        _combine_kernel,
        grid=(m // tm,),
        in_specs=[row, yrow, yrow, pl.BlockSpec((tm, 8), lambda i: (i, 0)),
                  pl.BlockSpec((1, d), lambda i: (0, 0))],
        out_specs=row,
        out_shape=jax.ShapeDtypeStruct((m, d), F32),
        compiler_params=_params(("parallel",)),
        name="combine_norm",
    )(h, y1, y2, gates, g.reshape(1, d))


def _cumsum_kernel(x_ref, o_ref, *, tb, n_blk):
    r = lax.broadcasted_iota(I32, (tb, tb), 0)
    c = lax.broadcasted_iota(I32, (tb, tb), 1)
    upper = jnp.where(r <= c, 1.0, 0.0)
    carry = jnp.zeros((H_FOX, 1), F32)
    for b in range(n_blk):
        blk = x_ref[0, :, b * tb:(b + 1) * tb]
        cs = jnp.dot(blk, upper, preferred_element_type=F32, precision=lax.Precision.HIGHEST) + carry
        o_ref[0, :, b * tb:(b + 1) * tb] = cs
        carry = cs[:, tb - 1:tb]


def cumsum_time(x, tb):
    b, h, t = x.shape
    assert t % tb == 0
    spec = pl.BlockSpec((1, h, t), lambda i: (i, 0, 0))
    return pl.pallas_call(
        functools.partial(_cumsum_kernel, tb=tb, n_blk=t // tb),
        grid=(b,),
        in_specs=[spec],
        out_specs=spec,
        out_shape=jax.ShapeDtypeStruct((b, h, t), F32),
        compiler_params=_params(("parallel",)),
        name="cumsum_time",
    )(x)


def _rows_of(parts, r0, n):
    pieces, start = [], 0
    for part in parts:
        lo, hi = max(r0, start), min(r0 + n, start + part.shape[0])
        if lo < hi:
            pieces.append(part[lo - start:hi - start, :].astype(F32))
        start += part.shape[0]
    got = sum(p.shape[0] for p in pieces)
    if got < n:
        pieces.append(jnp.zeros((n - got, parts[0].shape[1]), F32))
    return pieces[0] if len(pieces) == 1 else jnp.concatenate(pieces, axis=0)


def _split3(x):
    hi = x.astype(BF16).astype(F32)
    r = x - hi
    mid = r.astype(BF16).astype(F32)
    return hi, mid, (r - mid).astype(BF16).astype(F32)


def _bias_cols(col):
    hi, mid, lo = _split3(col)
    lane = lax.broadcasted_iota(I32, (col.shape[0], LANES), 1)
    e = jnp.where(lane == 0, hi, jnp.where(lane == 1, mid, jnp.where(lane == 2, lo,
        jnp.where(lane < 6, 1.0, 0.0))))
    return e.astype(BF16)


def _bias_rows(row):
    hi, mid, lo = _split3(row)
    sub = lax.broadcasted_iota(I32, (LANES, row.shape[1]), 0)
    e = jnp.where(sub < 3, 1.0, jnp.where(sub == 3, hi, jnp.where(sub == 4, mid,
        jnp.where(sub == 5, lo, 0.0))))
    return e.astype(BF16)


def _ones_col(n):
    return jnp.where(lax.broadcasted_iota(I32, (n, LANES), 1) == 0, 1.0, 0.0).astype(BF16)


def _run_blocks(n_full, scores, consume, sa_ref, sb_ref):
    n = sa_ref.shape[0]

    def put(ref, arrays):
        for i, a in enumerate(arrays):
            ref[i] = a

    def get(ref):
        return tuple(ref[i] for i in range(n))

    def pair(i, carry):
        kv = 2 * i
        put(sb_ref, scores(kv + 1))
        consume(get(sa_ref), kv, False)
        put(sa_ref, scores(kv + 2))
        consume(get(sb_ref), kv + 1, False)
        return carry

    def tail_odd():
        put(sb_ref, scores(n_full))
        consume(get(sa_ref), n_full - 1, False)
        consume(get(sb_ref), n_full, True)

    def tail_even():
        consume(get(sa_ref), n_full, True)

    put(sa_ref, scores(0))
    lax.fori_loop(0, n_full // 2, pair, 0)
    if isinstance(n_full, int):
        (tail_odd if n_full % 2 else tail_even)()
    else:
        pl.when(n_full % 2 == 1)(tail_odd)
        pl.when(n_full % 2 == 0)(tail_even)


def _fox_kernel(*refs, tq, tk, q_off, n_q, has_cache):
    if has_cache:
        (q_ref, k_ref, v_ref, kc_ref, vc_ref, cq_ref, ck_ref, o_ref, kt_ref, vb_ref, m_ref, acc_ref,
         sa_ref, sb_ref) = refs
        k_parts, v_parts = (kc_ref, k_ref), (vc_ref, v_ref)
    else:
        q_ref, k_ref, v_ref, cq_ref, ck_ref, o_ref, kt_ref, vb_ref, m_ref, acc_ref, sa_ref, sb_ref = refs
        k_parts, v_parts = (k_ref,), (v_ref,)
    h = pl.program_id(1)
    qi = pl.program_id(2)

    @pl.when(qi == 0)
    def _():
        for blk in range(kt_ref.shape[0]):
            r0 = blk * tk
            kt_ref[blk, 0:HEAD_DIM, :] = _rows_of(k_parts, r0, tk).T.astype(BF16)
            kt_ref[blk, HEAD_DIM:, :] = _bias_rows(-LOG2E * ck_ref[0, h, blk])
            vb_ref[r0:r0 + tk, 0:HEAD_DIM] = _rows_of(v_parts, r0, tk).astype(BF16)
            vb_ref[r0:r0 + tk, HEAD_DIM:] = _ones_col(tk)

    qpos0 = q_off if n_q == 1 else q_off + qi * tq
    cq_row = LOG2E * cq_ref[0, h, q_off // tq + qi]
    eye = lax.broadcasted_iota(I32, (tq, tq), 0) == lax.broadcasted_iota(I32, (tq, tq), 1)
    cq = jnp.sum(jnp.where(eye, cq_row, 0.0), axis=1, keepdims=True)
    qx = jnp.concatenate([q_ref[...], _bias_cols(cq)], axis=1)

    n_split, rs = m_ref.shape[0], m_ref.shape[1]
    qxs = [qx[i * rs:(i + 1) * rs, :] for i in range(n_split)]
    row = lax.broadcasted_iota(I32, (rs, tk), 0)
    col = lax.broadcasted_iota(I32, (rs, tk), 1)

    def scores(kv):
        kt = kt_ref[kv]
        return tuple(jnp.dot(x, kt, preferred_element_type=F32) for x in qxs)

    def consume(ss, kv, masked):
        k0 = pl.multiple_of(kv * tk, tk)
        vblk = vb_ref[pl.ds(k0, tk), :]
        for i, s in enumerate(ss):
            if masked:
                s = jnp.where(kv * tk + col <= qpos0 + i * rs + row, s, NEG_INF)
            m = m_ref[i]
            m_new = jnp.maximum(m, jnp.max(s, axis=1, keepdims=True))
            p = jnp.exp2(s - m_new).astype(BF16)
            pv = jnp.dot(p, vblk, preferred_element_type=F32)
            acc_ref[i] = jnp.exp2(m - m_new) * acc_ref[i] + pv
            m_ref[i] = m_new

    m_ref[...] = jnp.full(m_ref.shape, NEG_INF, F32)
    acc_ref[...] = jnp.zeros(acc_ref.shape, F32)
    n_full = qpos0 // tk

    _run_blocks(n_full, scores, consume, sa_ref, sb_ref)
    for i in range(n_split):
        acc = acc_ref[i]
        o_ref[i * rs:(i + 1) * rs, :] = (acc[:, :HEAD_DIM] / acc[:, HEAD_DIM:HEAD_DIM + 1]).astype(o_ref.dtype)


def fox_attention(q, k, v, c, n_batch, t_q, t_new, q_off, tq, tk, cache=None):
    n_q = t_q // tq
    tc = c.shape[2]
    assert t_q % tq == 0 and q_off % tq == 0 and tc % tq == 0 and tc % tk == 0
    assert (q_off % tk) + tq <= tk if n_q == 1 else (tq == tk and q_off % tk == 0)
    c_q = c.reshape(n_batch, H_FOX, tc // tq, 1, tq)
    c_k = c.reshape(n_batch, H_FOX, tc // tk, 1, tk)
    n_split = 2 if tq % 256 == 0 else 1
    (q, qg), (k, kg), (v, vg) = q, k, v
    in_specs = [pl.BlockSpec((None, tq, HEAD_DIM), lambda b, h, i: (qg, b * n_q + i, h)),
                pl.BlockSpec((None, t_new, HEAD_DIM), lambda b, h, i: (kg, b, h)),
                pl.BlockSpec((None, t_new, HEAD_DIM), lambda b, h, i: (vg, b, h))]
    args = [q, k, v]
    if cache is not None:
        past = cache[0].shape[1]
        assert past + t_new <= tc
        old = pl.BlockSpec((past, HEAD_DIM), lambda b, h, i: (b, h))
        in_specs += [old, old]
        args += [x.reshape(n_batch * past, H_FOX * HEAD_DIM) for x in cache]
    else:
        assert t_new == tc
    in_specs += [pl.BlockSpec((1, H_FOX, tc // tq, 1, tq), lambda b, h, i: (b, 0, 0, 0, 0)),
                 pl.BlockSpec((1, H_FOX, tc // tk, 1, tk), lambda b, h, i: (b, 0, 0, 0, 0))]
    return pl.pallas_call(
        functools.partial(_fox_kernel, tq=tq, tk=tk, q_off=q_off, n_q=n_q, has_cache=cache is not None),
        grid=(n_batch, H_FOX, n_q),
        in_specs=in_specs,
        out_specs=pl.BlockSpec((tq, HEAD_DIM), lambda b, h, i: (b * n_q + i, h)),
        out_shape=jax.ShapeDtypeStruct((n_batch * t_q, H_FOX * HEAD_DIM), BF16),
        scratch_shapes=[pltpu.VMEM((tc // tk, 2 * HEAD_DIM, tk), BF16), pltpu.VMEM((tc, 2 * HEAD_DIM), BF16),
                        pltpu.VMEM((n_split, tq // n_split, 1), F32),
                        pltpu.VMEM((n_split, tq // n_split, 2 * HEAD_DIM), F32),
                        pltpu.VMEM((n_split, tq // n_split, tk), F32), pltpu.VMEM((n_split, tq // n_split, tk), F32)],
        compiler_params=_params(("parallel", "parallel", "arbitrary")),
        name="fox_attention",
    )(*args, c_q, c_k)


def _diff_kernel(slope_ref, q_ref, k_ref, v_ref, *refs, tq, tk, q_off, n_q, n_valid, has_cache):
    if has_cache:
        kc0_ref, kc1_ref, vc_ref = refs[:3]
        refs = refs[3:]
    (lq1_ref, lk1_ref, lq2_ref, lk2_ref, sg_ref, o_ref, kt_ref, vb_ref, m_ref, l_ref, acc_ref,
     sa_ref, sb_ref) = refs
    h = pl.program_id(1)
    qi = pl.program_id(2)
    dv = 2 * HEAD_DIM
    slope2 = slope_ref[h] * LOG2E

    @pl.when(qi == 0)
    def _():
        v_parts = (vc_ref, v_ref) if has_cache else (v_ref,)
        for blk in range(kt_ref.shape[1]):
            r0 = blk * tk
            kpos_row = (r0 + lax.broadcasted_iota(I32, (1, tk), 1)).astype(F32)
            key_bias = _bias_rows(slope2 * kpos_row)
            for mp in range(2):
                new_rows = k_ref.at[:, mp * HEAD_DIM:(mp + 1) * HEAD_DIM]
                k_parts = ((kc0_ref, kc1_ref)[mp], new_rows) if has_cache else (new_rows,)
                kt_ref[mp, blk, 0:HEAD_DIM, :] = _rows_of(k_parts, r0, tk).T.astype(BF16)
                kt_ref[mp, blk, HEAD_DIM:, :] = key_bias
            vb_ref[r0:r0 + tk, :] = _rows_of(v_parts, r0, tk).astype(BF16)

    lam = (jnp.exp(jnp.sum(lq1_ref[...] * lk1_ref[...], axis=1, keepdims=True))
           - jnp.exp(jnp.sum(lq2_ref[...] * lk2_ref[...], axis=1, keepdims=True)) + LAM_INIT_L0)
    qpos0 = q_off if n_q == 1 else q_off + qi * tq
    qpos_col = (qpos0 + lax.broadcasted_iota(I32, (tq, 1), 0)).astype(F32)
    query_bias = _bias_cols(-slope2 * qpos_col)
    q = q_ref[...]
    n_split, rs = m_ref.shape[0] // 2, m_ref.shape[1]
    qx = [jnp.concatenate([q[i * rs:(i + 1) * rs, mp * HEAD_DIM:(mp + 1) * HEAD_DIM],
                           query_bias[i * rs:(i + 1) * rs, :]], axis=1)
          for mp in range(2) for i in range(n_split)]
    row = lax.broadcasted_iota(I32, (rs, tk), 0)
    col = lax.broadcasted_iota(I32, (rs, tk), 1)

    def scores(kv):
        return tuple(jnp.dot(qx[c], kt_ref[c // n_split, kv], preferred_element_type=F32)
                     for c in range(2 * n_split))

    def consume(ss, kv, masked):
        k0 = pl.multiple_of(kv * tk, tk)
        vblk = vb_ref[pl.ds(k0, tk), :]
        for c, s in enumerate(ss):
            if masked:
                qpos = qpos0 + (c % n_split) * rs + row
                kpos = kv * tk + col
                visible = ((kpos >> 6) <= (qpos >> 6)) & (kpos < n_valid)
                fix = (2.0 * slope2) * jnp.maximum(kpos - qpos, 0).astype(F32)
                s = jnp.where(visible, s - fix, NEG_INF)
            m = m_ref[c]
            m_new = jnp.maximum(m, jnp.max(s, axis=1, keepdims=True))
            alpha = jnp.exp2(m - m_new)
            p = jnp.exp2(s - m_new)
            l_ref[c] = alpha * l_ref[c] + jnp.sum(p, axis=1, keepdims=True)
            acc_ref[c] = alpha * acc_ref[c] + jnp.dot(p.astype(BF16), vblk, preferred_element_type=F32)
            m_ref[c] = m_new

    m_ref[...] = jnp.full(m_ref.shape, NEG_INF, F32)
    l_ref[...] = jnp.zeros(l_ref.shape, F32)
    acc_ref[...] = jnp.zeros(acc_ref.shape, F32)
    n_full = qpos0 // tk

    _run_blocks(n_full, scores, consume, sa_ref, sb_ref)
    for i in range(n_split):
        o = acc_ref[i] / l_ref[i] - lam * (acc_ref[n_split + i] / l_ref[n_split + i])
        o_ref[i * rs:(i + 1) * rs, :] = (_rms(o, sg_ref[...]) * (1.0 - LAM_INIT_L0)).astype(o_ref.dtype)


def diff_attention(q, k, v, lam_vecs, subln_g, n_batch, t_q, t_new, t_s, q_off, tq, tk, cache=None):
    n_q = t_q // tq
    assert t_q % tq == 0 and t_s % tk == 0
    dv = 2 * HEAD_DIM
    slopes = 2.0 ** (-8.0 * jnp.arange(1, H_DIFF + 1, dtype=F32) / H_DIFF)
    vec = pl.BlockSpec((1, HEAD_DIM), lambda b, h, i, s: (0, 0))
    (q, qg), (k, kg), (v, vg) = q, k, v
    in_specs = [pl.BlockSpec((None, tq, dv), lambda b, h, i, s: (qg, b * n_q + i, h)),
                pl.BlockSpec((None, t_new, dv), lambda b, h, i, s: (kg, b, h)),
                pl.BlockSpec((None, t_new, dv), lambda b, h, i, s: (vg, b, h))]
    args = [slopes, q, k, v]
    past = 0
    if cache is not None:
        past = cache[0].shape[1]
        in_specs += [
            pl.BlockSpec((past, HEAD_DIM), lambda b, h, i, s: (b, 2 * h)),
            pl.BlockSpec((past, HEAD_DIM), lambda b, h, i, s: (b, 2 * h + 1)),
            pl.BlockSpec((past, dv), lambda b, h, i, s: (b, h)),
        ]
        ck2d = cache[0].reshape(n_batch * past, H_DIFF * dv)
        args += [ck2d, ck2d, cache[1].reshape(n_batch * past, H_DIFF * dv)]
    n_valid = past + t_new
    assert n_valid <= t_s and tq % CHUNK == 0 or n_q == 1
    if n_q == 1:
        last_visible = min(((q_off + tq - 1) // CHUNK + 1) * CHUNK, n_valid)
        assert (q_off // tk) * tk <= n_valid and last_visible <= (q_off // tk + 1) * tk
    else:
        assert tq == tk and q_off % tk == 0
    in_specs += [vec, vec, vec, vec, pl.BlockSpec((1, dv), lambda b, h, i, s: (0, 0))]
    n_split = 1
    n_ch, rs = 2 * n_split, tq // n_split
    grid_spec = pltpu.PrefetchScalarGridSpec(
        num_scalar_prefetch=1,
        grid=(n_batch, H_DIFF, n_q),
        in_specs=in_specs,
        out_specs=pl.BlockSpec((tq, dv), lambda b, h, i, s: (b * n_q + i, h)),
        scratch_shapes=[pltpu.VMEM((2, t_s // tk, dv, tk), BF16), pltpu.VMEM((t_s, dv), BF16),
                        pltpu.VMEM((n_ch, rs, 1), F32), pltpu.VMEM((n_ch, rs, 1), F32),
                        pltpu.VMEM((n_ch, rs, dv), F32),
                        pltpu.VMEM((n_ch, rs, tk), F32), pltpu.VMEM((n_ch, rs, tk), F32)],
    )
    return pl.pallas_call(
        functools.partial(_diff_kernel, tq=tq, tk=tk, q_off=q_off, n_q=n_q, n_valid=n_valid,
                          has_cache=cache is not None),
        grid_spec=grid_spec,
        out_shape=jax.ShapeDtypeStruct((n_batch * t_q, H_DIFF * dv), BF16),
        compiler_params=_params(("parallel", "parallel", "arbitrary")),
        name="diff_attention",
    )(*args, *[x.reshape(1, HEAD_DIM) for x in lam_vecs], subln_g.reshape(1, dv))


REL_PAD = 384


def _band_bias_kernel(t_ref, o_ref, *, nk, n_valid):
    blk = pl.program_id(0)
    row = lax.broadcasted_iota(I32, (REL_PAD, nk), 0)
    j = lax.broadcasted_iota(I32, (REL_PAD, nk), 1)
    jc = lax.broadcasted_iota(I32, (1, nk), 1) >> 6
    jv = lax.broadcasted_iota(I32, (1, nk), 1) < n_valid
    for r in range(8):
        i = blk * 8 + r
        idx = jnp.clip(i - j + BAND_PAST, -REL_CLIP, REL_CLIP) + REL_CLIP
        onehot = jnp.where(row == idx, 1.0, 0.0)
        bias = jnp.dot(t_ref[...], onehot, preferred_element_type=F32, precision=lax.Precision.HIGHEST)
        ic = i >> 6
        visible = (jc - N_PREV_CHUNKS <= ic) & (jc >= ic) & jv
        o_ref[:, r, :] = jnp.where(visible, bias * LOG2E, NEG_INF)


def band_bias(table_padded, tq, nk, n_valid):
    assert CHUNK == 64
    return pl.pallas_call(
        functools.partial(_band_bias_kernel, nk=nk, n_valid=n_valid),
        grid=(tq // 8,),
        in_specs=[pl.BlockSpec((H_BAND, REL_PAD), lambda i: (0, 0))],
        out_specs=pl.BlockSpec((H_BAND, 8, nk), lambda i: (0, i, 0)),
        out_shape=jax.ShapeDtypeStruct((H_BAND, tq, nk), F32),
        compiler_params=_params(("parallel",)),
        name="band_bias",
    )(table_padded)


def _band_kernel(*refs, tq, nk, n_g, q_off, has_cache):
    if has_cache:
        q_ref, k_ref, v_ref, kc_ref, vc_ref, b_ref, o_ref, kb_ref, vb_ref, sa_ref, sb_ref = refs
        k_parts, v_parts = (kc_ref, k_ref), (vc_ref, v_ref)
        front = 0
    else:
        q_ref, k_ref, v_ref, b_ref, o_ref, kb_ref, vb_ref, sa_ref, sb_ref = refs
        k_parts, v_parts = (k_ref,), (v_ref,)
        front = kb_ref.shape[0] - k_ref.shape[0]
        kb_ref[0:front, :] = jnp.zeros((front, HEAD_DIM), BF16)
        vb_ref[0:front, 0:HEAD_DIM] = jnp.zeros((front, HEAD_DIM), BF16)
    t_s = kb_ref.shape[0]
    kb_ref[front:, :] = _rows_of(k_parts, 0, t_s - front).astype(BF16)
    vb_ref[front:, 0:HEAD_DIM] = _rows_of(v_parts, 0, t_s - front).astype(BF16)
    vb_ref[:, HEAD_DIM:] = _ones_col(t_s)
    jrow = lax.broadcasted_iota(I32, (1, nk), 1)

    def row0(g):
        return g * tq if isinstance(g, int) else pl.multiple_of(g * tq, tq)

    def scores(g):
        r0 = row0(g)
        return lax.dot_general(q_ref[pl.ds(r0, tq), :], kb_ref[pl.ds(r0, nk), :], (((1,), (1,)), ((), ())),
                               preferred_element_type=F32)

    def consume(s, g, clip_start):
        r0 = row0(g)
        s = s + b_ref[0]
        if clip_start:
            s = s + jnp.where(jrow >= BAND_PAST - (q_off + g * tq), 0.0, NEG_INF)
        m = jnp.max(s, axis=1, keepdims=True)
        p = jnp.exp2(s - m).astype(BF16)
        o = jnp.dot(p, vb_ref[pl.ds(r0, nk), :], preferred_element_type=F32)
        o_ref[pl.ds(r0, tq), :] = (o[:, :HEAD_DIM] / o[:, HEAD_DIM:HEAD_DIM + 1]).astype(o_ref.dtype)

    n_clip = min(n_g, max(0, -(-(BAND_PAST - q_off) // tq)))
    for g in range(n_clip):
        consume(scores(g), g, True)
    n_rest = n_g - n_clip
    if n_rest == 1:
        consume(scores(n_clip), n_clip, False)
    elif n_rest > 1:
        assert n_rest % 2 == 0
        sa_ref[...] = scores(n_clip)

        def pair(i, carry):
            g = n_clip + 2 * i
            sb_ref[...] = scores(g + 1)
            consume(sa_ref[...], g, False)
            sa_ref[...] = scores(jnp.minimum(g + 2, n_g - 1))
            consume(sb_ref[...], g + 1, False)
            return carry

        lax.fori_loop(0, n_rest // 2, pair, 0)


def band_attention(q, k, v, bias, n_batch, t_q, t_new, q_off, tq, cache=None):
    n_g = t_q // tq
    nk = bias.shape[2]
    t_s = (n_g - 1) * tq + nk
    rows = pl.BlockSpec((t_q, HEAD_DIM), lambda b, h: (b, h))
    (q, qg), (k, kg), (v, vg) = q, k, v
    in_specs = [pl.BlockSpec((None, t_q, HEAD_DIM), lambda b, h: (qg, b, h)),
                pl.BlockSpec((None, t_new, HEAD_DIM), lambda b, h: (kg, b, h)),
                pl.BlockSpec((None, t_new, HEAD_DIM), lambda b, h: (vg, b, h))]
    args = [q, k, v]
    if cache is not None:
        assert cache[0].shape[1] == BAND_PAST and BAND_PAST + t_new <= t_s and n_g == 1
        old = pl.BlockSpec((BAND_PAST, HEAD_DIM), lambda b, h: (b, h))
        in_specs += [old, old]
        args += [x.reshape(n_batch * BAND_PAST, H_BAND * HEAD_DIM) for x in cache]
    else:
        assert BAND_PAST + t_new == t_s
    in_specs.append(pl.BlockSpec((1, tq, nk), lambda b, h: (h, 0, 0)))
    return pl.pallas_call(
        functools.partial(_band_kernel, tq=tq, nk=nk, n_g=n_g, q_off=q_off, has_cache=cache is not None),
        grid=(n_batch, H_BAND),
        in_specs=in_specs,
        out_specs=rows,
        out_shape=jax.ShapeDtypeStruct((n_batch * t_q, H_BAND * HEAD_DIM), BF16),
        scratch_shapes=[pltpu.VMEM((t_s, HEAD_DIM), BF16), pltpu.VMEM((t_s, 2 * HEAD_DIM), BF16),
                        pltpu.VMEM((tq, nk), F32), pltpu.VMEM((tq, nk), F32)],
        compiler_params=_params(("parallel", "parallel")),
        name="band_attention",
    )(*args, bias)


def _even_layer(h, n_batch, t, caches, w, sample):
    (g_mix, w_q, w_kv, w_f, b_f, lam_vecs, subln_g, w_out_a, w_out_b, g_ffn, wg, wu, wd) = w
    hw = H_FOX * HEAD_DIM
    tm = 512 if not sample else h.shape[0]
    tm_in = 1024 if not sample else tm
    q, xn, logf = norm_proj(h, g_mix, w_q, hw, Q_SCALE, wf=w_f, bf=b_f, tm=tm_in)
    kv = proj(xn, w_kv, tm=tm_in)
    fk, fv, dk, dv = [(x[None], 0) for x in kv]
    logf_t = jnp.swapaxes(logf.reshape(n_batch, t, H_FOX), 1, 2)
    if not sample:
        c = cumsum_time(logf_t, 512)
        oa = fox_attention((q, 0), fk, fv, c, n_batch, t, t, 0, 512, 512)
        ob = diff_attention((q, 1), dk, dv, lam_vecs, subln_g, n_batch, t, t, t, 0, 512, 512)
    else:
        cfk, cfv, cfl, cdk, cdv = caches
        past = cfk.shape[1]
        t_k = past + t
        t_pad = -(-t_k // 384) * 384
        lf = jnp.concatenate([jnp.swapaxes(cfl, 1, 2), logf_t], axis=2)
        c = cumsum_time(jnp.pad(lf, ((0, 0), (0, 0), (0, t_pad - t_k))), 384)
        oa = fox_attention((q, 0), fk, fv, c, n_batch, t, t, past, t, 384, cache=(cfk, cfv))
        ob = diff_attention((q, 1), dk, dv, lam_vecs, subln_g, n_batch, t, t, t_pad, past, t, 384,
                            cache=(cdk, cdv))
    h = proj_residual([oa, ob], [w_out_a, w_out_b], h, tm=tm)
    h = ffn_dense(h, g_ffn, wg, wu, wd, tm=tm)
    return h, (kv[0], kv[1], logf, kv[2], kv[3])


def _odd_attention(h, n_batch, t, caches, w, sample):
    (g_mix, w_q, w_kv, table, w_out) = w
    d = D_MODEL
    tm = 512 if not sample else h.shape[0]
    tm_in = 1024 if not sample else tm
    q, xn = norm_proj(h, g_mix, w_q, d, Q_SCALE, tm=tm_in)
    kv = proj(xn, w_kv, tm=tm_in)
    bk, bv = [(x[None], 0) for x in kv]
    if not sample:
        tq = 4 * CHUNK
        bias = band_bias(table, tq, BAND_PAST + tq, BAND_PAST + tq)
        o = band_attention((q, 0), bk, bv, bias, n_batch, t, t, 0, tq)
    else:
        cbk, cbv, past_len = caches
        pc = cbk.shape[1]
        assert pc == BAND_PAST and t <= CHUNK and past_len % CHUNK == 0
        nk = -(-(pc + t) // LANES) * LANES
        bias = band_bias(table, t, nk, pc + t)
        o = band_attention((q, 0), bk, bv, bias, n_batch, t, t, past_len, t, cache=(cbk, cbv))
    h = proj_residual([o], [w_out], h, tm=tm)
    return h, (kv[0], kv[1])


def kernel(x_prompt, x_sample, cache_fox_k, cache_fox_v, cache_fox_logf, cache_diff_k, cache_diff_v, cache_band_k, cache_band_v, norm_mix_even, w_in_even, b_forget, lam_q1, lam_k1, lam_q2, lam_k2, subln_g, w_out_even, norm_ffn_even, w_gate, w_up, w_down, norm_mix_odd, w_in_odd, rel_bias, w_out_odd, norm_ffn_odd, w_router, w_gate_e, w_up_e, w_down_e, norm_final):
    bp, tp, d = x_prompt.shape
    bs, ts, _ = x_sample.shape
    past = cache_fox_k.shape[2]
    hw = H_FOX * HEAD_DIM
    fox_w = 3 * hw + H_FOX

    w_in0 = w_in_even[0]
    w_q0 = jnp.concatenate([w_in0[:, :hw], w_in0[:, fox_w:fox_w + hw]], axis=1).astype(BF16)
    w_kv0 = jnp.stack([w_in0[:, hw:2 * hw], w_in0[:, 2 * hw:3 * hw], w_in0[:, fox_w + hw:fox_w + 2 * hw],
                       w_in0[:, fox_w + 2 * hw:]]).astype(BF16)
    w_f = jnp.pad(w_in0[:, 3 * hw:fox_w], ((0, 0), (0, LANES - H_FOX))).astype(BF16)
    b_f = jnp.pad(b_forget[0], (0, LANES - H_FOX)).reshape(1, LANES)
    lam_vecs = (lam_q1[0], lam_k1[0], lam_q2[0], lam_k2[0])
    w_out0 = w_out_even[0].astype(BF16)
    even_w = (norm_mix_even[0], w_q0, w_kv0, w_f, b_f, lam_vecs, subln_g[0], w_out0[:hw], w_out0[hw:],
              norm_ffn_even[0], w_gate[0].astype(BF16), w_up[0].astype(BF16), w_down[0].astype(BF16))
    table = jnp.pad(rel_bias[0], ((0, 0), (0, REL_PAD - rel_bias.shape[2])))
    w_in1 = w_in_odd[0].astype(BF16)
    odd_w = (norm_mix_odd[0], w_in1[:, :d], jnp.stack([w_in1[:, d:2 * d], w_in1[:, 2 * d:]]), table,
             w_out_odd[0].astype(BF16))
    wr = jnp.pad(w_router[0], ((0, 0), (0, LANES - N_EXPERTS)))

    mp, ms = bp * tp, bs * ts
    h_p = x_prompt.reshape(mp, d)
    h_s = x_sample.reshape(ms, d)

    h_p, (fk_p, fv_p, fl_p, dk_p, dv_p) = _even_layer(h_p, bp, tp, None, even_w, False)
    caches_even = (cache_fox_k[0], cache_fox_v[0], cache_fox_logf[0], cache_diff_k[0], cache_diff_v[0])
    h_s, (fk_s, fv_s, fl_s, dk_s, dv_s) = _even_layer(h_s, bs, ts, caches_even, even_w, True)

    h_p, (bk_p, bv_p) = _odd_attention(h_p, bp, tp, None, odd_w, False)
    h_s, (bk_s, bv_s) = _odd_attention(h_s, bs, ts, (cache_band_k[0], cache_band_v[0], past), odd_w, True)

    m_all = mp + ms
    assert mp % 512 == 0 and mp % ms == 0
    zero_counts = jnp.zeros((1, LANES), F32)
    idx_p, gate_p, cnt_p, u_p = router(h_p, norm_ffn_odd[0], wr, zero_counts, 512)
    idx_s, gate_s, cnt, u_s = router(h_s, norm_ffn_odd[0], wr, cnt_p, ms)
    idx = jnp.concatenate([idx_p, idx_s], axis=0)
    u_all = jnp.concatenate([u_p, u_s], axis=0)
    counts = cnt[0, :N_EXPERTS].astype(I32)
    n_tiles_e = (counts + MOE_TILE - 1) // MOE_TILE
    tile_end = jnp.cumsum(n_tiles_e)
    tile_start = tile_end - n_tiles_e
    row_off = tile_start * MOE_TILE
    n_used = tile_end[-1:]
    nt = (2 * m_all) // MOE_TILE + N_EXPERTS
    tile_ids = jnp.arange(nt, dtype=I32)
    last = jnp.maximum(n_used[0] - 1, 0)
    tile_expert = jnp.sum((jnp.minimum(tile_ids, last)[:, None] >= tile_end[None, :]).astype(I32), axis=1)
    tile_expert = jnp.minimum(tile_expert, N_EXPERTS - 1)
    tile_rows = jnp.clip(counts[tile_expert] - (tile_ids - tile_start[tile_expert]) * MOE_TILE, 0, MOE_TILE)
    pos1 = row_off[idx[:, 0]] + idx[:, 2]
    pos2 = row_off[idx[:, 1]] + idx[:, 3]
    tokens = jnp.arange(m_all, dtype=I32)
    row_token = jnp.zeros((nt * MOE_TILE,), I32).at[pos1].set(tokens).at[pos2].set(tokens)
    y = ffn_moe(tile_expert, n_used.astype(I32), tile_rows.astype(I32), row_token, u_all,
                w_gate_e[0], w_up_e[0], w_down_e[0])
    y1 = jnp.take(y, pos1, axis=0, mode="clip")
    y2 = jnp.take(y, pos2, axis=0, mode="clip")
    y_p = combine_norm(h_p, y1, y2, gate_p, norm_final, 512, 0)
    y_s = combine_norm(h_s, y1, y2, gate_s, norm_final, ms, mp)

    keep = min(BAND_PAST, tp)
    bk_p = bk_p.reshape(bp, tp, H_BAND, HEAD_DIM)[:, tp - keep:][None]
    bv_p = bv_p.reshape(bp, tp, H_BAND, HEAD_DIM)[:, tp - keep:][None]
    return (
        y_p.reshape(bp, tp, d), y_s.reshape(bs, ts, d),
        fk_p.reshape(1, bp, tp, H_FOX, HEAD_DIM), fk_s.reshape(1, bs, ts, H_FOX, HEAD_DIM),
        fv_p.reshape(1, bp, tp, H_FOX, HEAD_DIM), fv_s.reshape(1, bs, ts, H_FOX, HEAD_DIM),
        fl_p.reshape(1, bp, tp, H_FOX), fl_s.reshape(1, bs, ts, H_FOX),
        dk_p.reshape(1, bp, tp, H_DIFF, 2, HEAD_DIM), dk_s.reshape(1, bs, ts, H_DIFF, 2, HEAD_DIM),
        dv_p.reshape(1, bp, tp, H_DIFF, 2 * HEAD_DIM), dv_s.reshape(1, bs, ts, H_DIFF, 2 * HEAD_DIM),
        bk_p, bk_s.reshape(1, bs, ts, H_BAND, HEAD_DIM),
        bv_p, bv_s.reshape(1, bs, ts, H_BAND, HEAD_DIM),
    )
```
